```python
import math
import jax, jax.numpy as jnp
from jax import lax
import numpy as np

D_MODEL = 2048
BATCH = 4
SEQ = 4096
DEPTH = 2

N_SUBLAYERS = 3
N_BRANCHES = 4
MIX_WIDTH = D_MODEL // 4
HEAD_DIM = 64
ROPE_THETA = 10000.0
NORM_EPS = 1e-6
Q_BLOCK = 128
NEG_INF = -1e30
FORCE_SCORE = 1e30
D_FF = ((8 * D_MODEL // 3 + 255) // 256) * 256
MACARON_WEIGHT = 0.5

NSA_HEADS = MIX_WIDTH // HEAD_DIM
NSA_KV_HEADS = 2
NSA_GROUP = NSA_HEADS // NSA_KV_HEADS
CMP_BLOCK = 32
CMP_STRIDE = 16
CMP_HIDDEN = 256
SLC_BLOCK = 64
SLC_TOPK = 16
WIN_SIZE = 512
NSA_SPLITS = (NSA_HEADS * HEAD_DIM,) + (NSA_KV_HEADS * HEAD_DIM,) * 6 + (3 * NSA_HEADS,)

RWKV_HEADS = MIX_WIDTH // HEAD_DIM
RWKV_WIDTH = MIX_WIDTH
RWKV_DECAY_LORA = 64
RWKV_AAA_LORA = 64
RWKV_GATE_LORA = 128
RWKV_GN_EPS = 64e-5
RWKV_SPLITS = (RWKV_WIDTH, RWKV_DECAY_LORA, RWKV_WIDTH, RWKV_WIDTH, RWKV_AAA_LORA, RWKV_GATE_LORA)

GLA_HEADS = 4
GLA_DK = MIX_WIDTH // 2
GLA_DV = MIX_WIDTH
GLA_DK_HEAD = GLA_DK // GLA_HEADS
GLA_DV_HEAD = GLA_DV // GLA_HEADS
GLA_GATE_LORA = 16
GLA_TAU = 16.0
GLA_CHUNK = 64
GLA_SPLITS = (GLA_DK, GLA_DK, GLA_DV, GLA_GATE_LORA, GLA_DV)

MLA_HEADS = MIX_WIDTH // 64
MLA_Q_RANK = 384
MLA_KV_RANK = 128
MLA_NOPE = 64
MLA_ROPE = 32
MLA_V = MIX_WIDTH // MLA_HEADS
MLA_SPLITS = (MLA_Q_RANK, MLA_KV_RANK, MLA_ROPE)

BRANCH_SPLITS = (sum(NSA_SPLITS), sum(RWKV_SPLITS), sum(GLA_SPLITS), sum(MLA_SPLITS), N_BRANCHES * D_MODEL)
PROJ_IN = sum(BRANCH_SPLITS)

kernel_name = 'hybrid_nsa_rwkv7_gla_mla_macaron'


def _split(t, sizes):
    return jnp.split(t, np.cumsum(sizes)[:-1].tolist(), axis=-1)


def _rmsnorm(x, g):
    xf = x.astype(jnp.float32)
    y = xf * lax.rsqrt(jnp.mean(xf * xf, axis=-1, keepdims=True) + NORM_EPS)
    return (y * g.astype(jnp.float32)).astype(x.dtype)


def _rope(x, positions):
    d = x.shape[-1]
    inv = ROPE_THETA ** (-jnp.arange(0, d, 2, dtype=jnp.float32) / d)
    ang = positions.astype(jnp.float32)[:, None] * inv[None, :]
    cos, sin = jnp.cos(ang)[:, None, :], jnp.sin(ang)[:, None, :]
    x1, x2 = jnp.split(x.astype(jnp.float32), 2, axis=-1)
    return jnp.concatenate([x1 * cos - x2 * sin, x1 * sin + x2 * cos], axis=-1).astype(x.dtype)


def _modulation(c, w, b):
    m = jax.nn.silu(c) @ w + b
    shift, scale, gate = jnp.split(m[:, None, :], 3, axis=-1)
    return shift, scale, gate


def _sandwich(x, c, fn, w_mod, b_mod, g_pre, g_post, res_weight):
    shift, scale, gate = _modulation(c, w_mod, b_mod)
    u = _rmsnorm(x, g_pre) * (1.0 + scale) + shift
    y = _rmsnorm(fn(u), g_post)
    return x + res_weight * gate * y


def _swiglu(u, wg, wu, wd):
    return (jax.nn.silu(u @ wg) * (u @ wu)) @ wd


def _causal_attention_blocked(q, k, v, scale):
    B, H, S, dq = q.shape
    nb = S // Q_BLOCK
    qb = q.reshape(B, H, nb, Q_BLOCK, dq).transpose(2, 0, 1, 3, 4)
    k_pos = jnp.arange(S)

    def one_block(args):
        q_i, i = args
        s = jnp.einsum('bhqd,bhkd->bhqk', q_i, k).astype(jnp.float32) * scale
        q_pos = i * Q_BLOCK + jnp.arange(Q_BLOCK)
        s = jnp.where(k_pos[None, :] <= q_pos[:, None], s, NEG_INF)
        p = jax.nn.softmax(s, axis=-1)
        return jnp.einsum('bhqk,bhkd->bhqd', p.astype(v.dtype), v)

    o = lax.map(one_block, (qb, jnp.arange(nb)))
    return o.transpose(1, 2, 0, 3, 4).reshape(B, H, S, v.shape[-1])


def _nsa(q, k_cmp, v_cmp, k_slc, v_slc, k_win, v_win, gate_logits, cmp_pos, cmp_w1, cmp_w2, positions):
    B, S = q.shape[0], q.shape[1]
    Hk, G, d = NSA_KV_HEADS, NSA_GROUP, HEAD_DIM
    scale = d ** -0.5
    nq = S // Q_BLOCK
    t_pos = np.arange(S)
    qg = _rope(q, positions).reshape(B, S, Hk, G, d).transpose(0, 2, 3, 1, 4)
    heads_first = lambda t: t.transpose(0, 2, 1, 3)
    kc_in, vc_in = heads_first(_rope(k_cmp, positions)), heads_first(v_cmp)
    ks, vs = heads_first(_rope(k_slc, positions)), heads_first(v_slc)
    kw, vw = heads_first(_rope(k_win, positions)), heads_first(v_win)

    n_cmp = (S - CMP_BLOCK) // CMP_STRIDE + 1
    cmp_idx = np.arange(n_cmp)[:, None] * CMP_STRIDE + np.arange(CMP_BLOCK)[None, :]

    def compress(t, pos_emb, w1, w2):
        blocks = (t[:, :, cmp_idx] + pos_emb).reshape(B, Hk, n_cmp, CMP_BLOCK * d)
        return jax.nn.gelu(blocks @ w1) @ w2

    kc = compress(kc_in, cmp_pos[0], cmp_w1[0], cmp_w2[0])
    vc = compress(vc_in, cmp_pos[1], cmp_w1[1], cmp_w2[1])
    cmp_mask = cmp_idx[:, -1][None, :] <= t_pos[:, None]
    s_cmp = jnp.einsum('bhgsd,bhnd->bhgsn', qg, kc).astype(jnp.float32) * scale
    p_cmp = jax.nn.softmax(jnp.where(cmp_mask, s_cmp, NEG_INF), axis=-1) * cmp_mask
    o_cmp = jnp.einsum('bhgsn,bhnd->bhgsd', p_cmp.astype(vc.dtype), vc)

    n_sel = S // SLC_BLOCK
    blk = np.arange(n_sel)
    overlap = ((cmp_idx[:, :1] < (blk[None, :] + 1) * SLC_BLOCK) &
               (cmp_idx[:, -1:] >= blk[None, :] * SLC_BLOCK)).astype(np.float32)
    imp = jnp.einsum('bhgsn,nj->bhsj', p_cmp, overlap)
    cur = (t_pos // SLC_BLOCK)[:, None]
    valid = blk[None, :] <= cur
    forced = valid & ((blk[None, :] == 0) | (blk[None, :] >= cur - 1))
    imp = jnp.where(forced, FORCE_SCORE, jnp.where(valid, imp, NEG_INF))
    k_top = min(SLC_TOPK, n_sel)
    _, sel_idx = lax.top_k(imp, k_top)

    k_blocks = ks.reshape(B, Hk, n_sel, SLC_BLOCK, d)
    v_blocks = vs.reshape(B, Hk, n_sel, SLC_BLOCK, d)
    q_chunks = qg.reshape(B, Hk, G, nq, Q_BLOCK, d).transpose(3, 0, 1, 2, 4, 5)
    idx_chunks = sel_idx.reshape(B, Hk, nq, Q_BLOCK, k_top).transpose(2, 0, 1, 3, 4)
    b_ix = jnp.arange(B)[:, None, None, None]
    h_ix = jnp.arange(Hk)[None, :, None, None]

    def selected_block(args):
        q_c, idx_c, blk_id = args
        kb = k_blocks[b_ix, h_ix, idx_c]
        vb = v_blocks[b_ix, h_ix, idx_c]
        s = jnp.einsum('bhgqd,bhqnld->bhgqnl', q_c, kb).astype(jnp.float32) * scale
        q_pos = blk_id * Q_BLOCK + jnp.arange(Q_BLOCK)
        k_pos = idx_c[..., None] * SLC_BLOCK + jnp.arange(SLC_BLOCK)
        mask = (k_pos <= q_pos[:, None, None])[:, :, None]
        s = jnp.where(mask, s, NEG_INF).reshape(B, Hk, G, Q_BLOCK, k_top * SLC_BLOCK)
        p = jax.nn.softmax(s, axis=-1).reshape(B, Hk, G, Q_BLOCK, k_top, SLC_BLOCK)
        return jnp.einsum('bhgqnl,bhqnld->bhgqd', p.astype(vb.dtype), vb)

    o_slc = lax.map(selected_block, (q_chunks, idx_chunks, jnp.arange(nq)))
    o_slc = o_slc.transpose(1, 2, 3, 0, 4, 5).reshape(B, Hk, G, S, d)

    n_wb = WIN_SIZE // Q_BLOCK

    def band(t):
        tp = jnp.pad(t, ((0, 0), (0, 0), (WIN_SIZE, 0), (0, 0))).reshape(B, Hk, nq + n_wb, Q_BLOCK, d)
        return jnp.concatenate([tp[:, :, i:i + nq] for i in range(n_wb + 1)], axis=3)

    kwb, vwb = band(kw), band(vw)
    q_pos = t_pos.reshape(nq, Q_BLOCK, 1)
    k_pos = ((np.arange(nq)[:, None] - n_wb) * Q_BLOCK + np.arange((n_wb + 1) * Q_BLOCK)[None, :])[:, None, :]
    win_mask = (k_pos >= 0) & (k_pos <= q_pos) & (k_pos > q_pos - WIN_SIZE)
    s_win = jnp.einsum('bhgnqd,bhnkd->bhgnqk', qg.reshape(B, Hk, G, nq, Q_BLOCK, d), kwb).astype(jnp.float32) * scale
    p_win = jax.nn.softmax(jnp.where(win_mask, s_win, NEG_INF), axis=-1)
    o_win = jnp.einsum('bhgnqk,bhnkd->bhgnqd', p_win.astype(vwb.dtype), vwb).reshape(B, Hk, G, S, d)

    gates = jax.nn.sigmoid(gate_logits.reshape(B, S, Hk, G, 3).transpose(0, 2, 3, 1, 4))
    o = gates[..., 0:1] * o_cmp + gates[..., 1:2] * o_slc + gates[..., 2:3] * o_win
    return o.transpose(0, 3, 1, 2, 4).reshape(B, S, NSA_HEADS * d)


def _rwkv7(p_rwkv, mu, w0, w_w2, a0, a_w2, g_w2, k_k, k_a, r_k, ln_w, ln_b):
    B, S, _ = p_rwkv.shape
    H, N = RWKV_HEADS, HEAD_DIM
    prev = jnp.pad(p_rwkv, ((0, 0), (1, 0), (0, 0)))[:, :-1]
    xs = p_rwkv + (prev - p_rwkv) * mu
    r, w_lo, k, v, a_lo, g_lo = [t.astype(jnp.float32) for t in _split(xs, RWKV_SPLITS)]
    log_w = -math.exp(-0.5) * jax.nn.sigmoid(w0 + jnp.tanh(w_lo) @ w_w2)
    a = jax.nn.sigmoid(a0 + a_lo @ a_w2)
    g = jax.nn.sigmoid(g_lo) @ g_w2
    heads = lambda t: t.reshape(B, S, H, N)
    kk = heads(k * k_k)
    kk = kk / jnp.maximum(jnp.sqrt(jnp.sum(kk * kk, axis=-1, keepdims=True)), 1e-12)
    k = k * (1.0 + (a - 1.0) * k_a)
    r, log_w, k, v, a = heads(r), heads(log_w), heads(k), heads(v), heads(a)

    def step(state, inp):
        r_t, w_t, k_t, v_t, kk_t, a_t = inp
        s_kk = jnp.einsum('bhvk,bhk->bhv', state, kk_t)
        state = (state * w_t[:, :, None, :] - s_kk[..., None] * (kk_t * a_t)[:, :, None, :]
                 + v_t[..., None] * k_t[:, :, None, :])
        return state, jnp.einsum('bhvk,bhk->bhv', state, r_t)

    tm = lambda t: t.transpose(1, 0, 2, 3)
    state0 = jnp.zeros((B, H, N, N), jnp.float32)
    _, y = lax.scan(step, state0, (tm(r), tm(jnp.exp(log_w)), tm(k), tm(v), tm(kk), tm(a)))
    y = y.transpose(1, 0, 2, 3)
    mean = jnp.mean(y, axis=-1, keepdims=True)
    var = jnp.mean(jnp.square(y - mean), axis=-1, keepdims=True)
    y = ((y - mean) * lax.rsqrt(var + RWKV_GN_EPS)).reshape(B, S, H * N) * ln_w + ln_b
    bonus = jnp.sum(r * k * r_k.reshape(H, N), axis=-1, keepdims=True) * v
    y = (y + bonus.reshape(B, S, H * N)) * g
    return y.astype(p_rwkv.dtype)


def _gla(q, k, v, alpha_lo, r, alpha_w2, alpha_b, norm_g):
    B, S, _ = q.shape
    H, dk, dv, C = GLA_HEADS, GLA_DK_HEAD, GLA_DV_HEAD, GLA_CHUNK
    nc = S // C
    log_a = jax.nn.log_sigmoid((alpha_lo @ alpha_w2 + alpha_b).astype(jnp.float32)) / GLA_TAU

    def chunks(t, dh):
        return t.astype(jnp.float32).reshape(B, nc, C, H, dh).transpose(1, 0, 3, 2, 4)

    qc, kc, vc, gc = chunks(q, dk) * dk ** -0.5, chunks(k, dk), chunks(v, dv), chunks(log_a, dk)
    causal = np.tril(np.ones((C, C), dtype=bool))[:, :, None]

    def step(state, inp):
        q_t, k_t, v_t, g_t = inp
        b = jnp.cumsum(g_t, axis=2)
        decay = jnp.exp(jnp.where(causal, b[:, :, :, None, :] - b[:, :, None, :, :], -jnp.inf))
        attn = jnp.einsum('bhid,bhjd,bhijd->bhij', q_t, k_t, decay)
        b_last = b[:, :, -1:, :]
        o = attn @ v_t + (q_t * jnp.exp(b)) @ state
        state = (state * jnp.exp(b_last).transpose(0, 1, 3, 2)
                 + jnp.einsum('bhjd,bhje->bhde', k_t * jnp.exp(b_last - b), v_t))
        return state, o

    _, o = lax.scan(step, jnp.zeros((B, H, dk, dv), jnp.float32), (qc, kc, vc, gc))
    o = o.transpose(1, 0, 3, 2, 4).reshape(B, S, H, dv)
    o = o * lax.rsqrt(jnp.mean(o * o, axis=-1, keepdims=True) + NORM_EPS)
    o = o.reshape(B, S, H * dv) * norm_g * jax.nn.silu(r.astype(jnp.float32))
    return o.astype(q.dtype)


def _mla(c_q, c_kv, k_rope, q_norm, w_uq, kv_norm, w_ukv, positions):
    B, S, _ = c_q.shape
    H = MLA_HEADS
    q = (_rmsnorm(c_q, q_norm) @ w_uq).reshape(B, S, H, MLA_NOPE + MLA_ROPE)
    q = jnp.concatenate([q[..., :MLA_NOPE], _rope(q[..., MLA_NOPE:], positions)], axis=-1)
    kv = (_rmsnorm(c_kv, kv_norm) @ w_ukv).reshape(B, S, H, MLA_NOPE + MLA_V)
    k_pe = jnp.broadcast_to(_rope(k_rope[:, :, None, :], positions), (B, S, H, MLA_ROPE))
    k = jnp.concatenate([kv[..., :MLA_NOPE], k_pe], axis=-1)
    v = kv[..., MLA_NOPE:]
    o = _causal_attention_blocked(q.transpose(0, 2, 1, 3), k.transpose(0, 2, 1, 3), v.transpose(0, 2, 1, 3),
                                  (MLA_NOPE + MLA_ROPE) ** -0.5)
    return o.transpose(0, 2, 1, 3).reshape(B, S, H * MLA_V)


def _token_mixing(u, w_in, w_branch, w_out, cmp_pos, cmp_w1, cmp_w2,
                  rwkv_mu, rwkv_w0, rwkv_w_w2, rwkv_a0, rwkv_a_w2, rwkv_g_w2, rwkv_k_k, rwkv_k_a,
                  rwkv_r_k, rwkv_ln_w, rwkv_ln_b, gla_alpha_w2, gla_alpha_b, gla_norm_g,
                  mla_q_norm, mla_w_uq, mla_kv_norm, mla_w_ukv):
    B, S, _ = u.shape
    positions = jnp.arange(S)
    proj = u @ w_in
    p_nsa, p_rwkv, p_gla, p_mla, p_gate = _split(proj, BRANCH_SPLITS)
    nq_, kc_, vc_, ks_, vs_, kw_, vw_, gl_ = _split(p_nsa, NSA_SPLITS)
    kvh = lambda t: t.reshape(B, S, NSA_KV_HEADS, HEAD_DIM)
    y_nsa = _nsa(nq_.reshape(B, S, NSA_HEADS, HEAD_DIM), kvh(kc_), kvh(vc_), kvh(ks_), kvh(vs_),
                 kvh(kw_), kvh(vw_), gl_, cmp_pos, cmp_w1, cmp_w2, positions)
    y_rwkv = _rwkv7(p_rwkv, rwkv_mu, rwkv_w0, rwkv_w_w2, rwkv_a0, rwkv_a_w2, rwkv_g_w2,
                    rwkv_k_k, rwkv_k_a, rwkv_r_k, rwkv_ln_w, rwkv_ln_b)
    gq, gk, gv, ga, gr = _split(p_gla, GLA_SPLITS)
    y_gla = _gla(gq, gk, gv, ga, gr, gla_alpha_w2, gla_alpha_b, gla_norm_g)
    mq, mkv, mkr = _split(p_mla, MLA_SPLITS)
    y_mla = _mla(mq, mkv, mkr, mla_q_norm, mla_w_uq, mla_kv_norm, mla_w_ukv, positions)
    gates = jax.nn.sigmoid(p_gate.reshape(B, S, N_BRANCHES, D_MODEL))
    ys = (y_nsa, y_rwkv, y_gla, y_mla)
    merged = gates[:, :, 0] * (ys[0] @ w_branch[0])
    for i in range(1, N_BRANCHES):
        merged = merged + gates[:, :, i] * (ys[i] @ w_branch[i])
    return merged @ w_out


def setup_inputs(seed: int = 0) -> dict:
    key = jax.random.key(seed)
    ks = iter(jax.random.split(key, 40))
    nrm = lambda shape, s: jax.random.normal(next(ks), shape, jnp.float32) * s
    gain = lambda shape: 1.0 + nrm(shape, 0.02)
    L, D = DEPTH, D_MODEL
    return {
        'x': nrm((BATCH, SEQ, D), 1.0),
        'c': nrm((BATCH, D), 1.0),
        'ada_w': nrm((L, N_SUBLAYERS, D, 3 * D), 0.5 * D ** -0.5),
        'ada_b': nrm((L, N_SUBLAYERS, 3 * D), 0.01),
        'pre_g': gain((L, N_SUBLAYERS, D)),
        'post_g': gain((L, N_SUBLAYERS, D)),
        'ffn_wg': nrm((L, 2, D, D_FF), D ** -0.5),
        'ffn_wu': nrm((L, 2, D, D_FF), D ** -0.5),
        'ffn_wd': nrm((L, 2, D_FF, D), D_FF ** -0.5),
        'mix_w_in': nrm((L, D, PROJ_IN), D ** -0.5),
        'mix_w_branch': nrm((L, N_BRANCHES, MIX_WIDTH, D), MIX_WIDTH ** -0.5),
        'mix_w_out': nrm((L, D, D), D ** -0.5),
        'nsa_cmp_pos': nrm((L, 2, CMP_BLOCK, HEAD_DIM), 0.02),
        'nsa_cmp_w1': nrm((L, 2, CMP_BLOCK * HEAD_DIM, CMP_HIDDEN), (CMP_BLOCK * HEAD_DIM) ** -0.5),
        'nsa_cmp_w2': nrm((L, 2, CMP_HIDDEN, HEAD_DIM), CMP_HIDDEN ** -0.5),
        'rwkv_mu': jax.random.uniform(next(ks), (L, sum(RWKV_SPLITS)), jnp.float32),
        'rwkv_w0': nrm((L, RWKV_WIDTH), 0.5),
        'rwkv_w_w2': nrm((L, RWKV_DECAY_LORA, RWKV_WIDTH), RWKV_DECAY_LORA ** -0.5),
        'rwkv_a0': nrm((L, RWKV_WIDTH), 0.1),
        'rwkv_a_w2': nrm((L, RWKV_AAA_LORA, RWKV_WIDTH), 0.5 * RWKV_AAA_LORA ** -0.5),
        'rwkv_g_w2': nrm((L, RWKV_GATE_LORA, RWKV_WIDTH), RWKV_GATE_LORA ** -0.5),
        'rwkv_k_k': 0.85 + nrm((L, RWKV_WIDTH), 0.02),
        'rwkv_k_a': gain((L, RWKV_WIDTH)),
        'rwkv_r_k': nrm((L, RWKV_WIDTH), 0.1),
        'rwkv_ln_w': gain((L, RWKV_WIDTH)),
        'rwkv_ln_b': nrm((L, RWKV_WIDTH), 0.01),
        'gla_alpha_w2': nrm((L, GLA_GATE_LORA, GLA_DK), GLA_GATE_LORA ** -0.5),
        'gla_alpha_b': 1.0 + nrm((L, GLA_DK), 0.5),
        'gla_norm_g': gain((L, GLA_DV)),
        'mla_q_norm': gain((L, MLA_Q_RANK)),
        'mla_w_uq': nrm((L, MLA_Q_RANK, MLA_HEADS * (MLA_NOPE + MLA_ROPE)), MLA_Q_RANK ** -0.5),
        'mla_kv_norm': gain((L, MLA_KV_RANK)),
        'mla_w_ukv': nrm((L, MLA_KV_RANK, MLA_HEADS * (MLA_NOPE + MLA_V)), MLA_KV_RANK ** -0.5),
    }


def reference(x, c, ada_w, ada_b, pre_g, post_g, ffn_wg, ffn_wu, ffn_wd, mix_w_in, mix_w_branch,
              mix_w_out, nsa_cmp_pos, nsa_cmp_w1, nsa_cmp_w2, rwkv_mu, rwkv_w0, rwkv_w_w2, rwkv_a0,
              rwkv_a_w2, rwkv_g_w2, rwkv_k_k, rwkv_k_a, rwkv_r_k, rwkv_ln_w, rwkv_ln_b, gla_alpha_w2,
              gla_alpha_b, gla_norm_g, mla_q_norm, mla_w_uq, mla_kv_norm, mla_w_ukv):
    for l in range(DEPTH):
        x = _sandwich(x, c, lambda u: _swiglu(u, ffn_wg[l, 0], ffn_wu[l, 0], ffn_wd[l, 0]),
                      ada_w[l, 0], ada_b[l, 0], pre_g[l, 0], post_g[l, 0], MACARON_WEIGHT)
        x = _sandwich(x, c, lambda u: _token_mixing(
            u, mix_w_in[l], mix_w_branch[l], mix_w_out[l], nsa_cmp_pos[l], nsa_cmp_w1[l], nsa_cmp_w2[l],
            rwkv_mu[l], rwkv_w0[l], rwkv_w_w2[l], rwkv_a0[l], rwkv_a_w2[l], rwkv_g_w2[l], rwkv_k_k[l],
            rwkv_k_a[l], rwkv_r_k[l], rwkv_ln_w[l], rwkv_ln_b[l], gla_alpha_w2[l], gla_alpha_b[l],
            gla_norm_g[l], mla_q_norm[l], mla_w_uq[l], mla_kv_norm[l], mla_w_ukv[l]),
            ada_w[l, 1], ada_b[l, 1], pre_g[l, 1], post_g[l, 1], 1.0)
        x = _sandwich(x, c, lambda u: _swiglu(u, ffn_wg[l, 1], ffn_wu[l, 1], ffn_wd[l, 1]),
                      ada_w[l, 2], ada_b[l, 2], pre_g[l, 2], post_g[l, 2], MACARON_WEIGHT)
    return x
```

```python
import functools
import math

import jax
import jax.numpy as jnp
import numpy as np
from jax import lax
from jax.experimental import pallas as pl
from jax.experimental.pallas import tpu as pltpu

F32 = jnp.float32
BF16 = jnp.bfloat16
HI = lax.Precision.HIGHEST

NORM_EPS = 1e-6
NEG_INF = -1e30
FORCE_SCORE = 1e30
ROPE_THETA = 10000.0
HEAD_DIM = 64

NSA_HEADS = 8
NSA_KV_HEADS = 2
NSA_GROUP = NSA_HEADS // NSA_KV_HEADS
CMP_BLOCK = 32
CMP_STRIDE = 16
SLC_BLOCK = 64
SLC_SHIFT = 6
SLC_TOPK = 16
WIN_SIZE = 512
RWKV_HEADS = 8
RWKV_GN_EPS = 64e-5
RWKV_CHUNK = 64
GLA_HEADS = 4
GLA_TAU = 16.0
GLA_CHUNK = 64
MLA_HEADS = 8
MLA_NOPE = 64
MLA_ROPE = 32

VMEM_LIMIT_BYTES = 56 * 1024 * 1024


def _params(*sem):
    return pltpu.CompilerParams(dimension_semantics=sem, vmem_limit_bytes=VMEM_LIMIT_BYTES)


def _dot(a, b, precision=None):
    return jnp.dot(a, b, preferred_element_type=F32, precision=precision)


def _dot_nt(a, b, precision=None):
    return lax.dot_general(a, b, (((1,), (1,)), ((), ())), preferred_element_type=F32, precision=precision)


def _dot_tn(a, b, precision=None):
    return lax.dot_general(a, b, (((0,), (0,)), ((), ())), preferred_element_type=F32, precision=precision)


def _rms(x):
    return x * lax.rsqrt(jnp.mean(x * x, axis=-1, keepdims=True) + NORM_EPS)


def _tile(n, t):
    t = min(n, t)
    assert n % t == 0, (n, t)
    return t


def _mod_kernel(c_ref, w_ref, b_ref, o_ref):
    c = c_ref[...]
    o_ref[0] = _dot(c * jax.nn.sigmoid(c), w_ref[0], HI) + b_ref[0]


def _modulation(c, ada_w, ada_b):
    B, D = c.shape
    n_sub = ada_w.shape[0] * ada_w.shape[1]
    w = ada_w.reshape(n_sub, D, 3 * D)
    b = ada_b.reshape(n_sub, 1, 3 * D)
    rows = 8
    cp = jnp.zeros((rows, D), F32).at[:B].set(c)
    tn = _tile(3 * D, 512)
    out = pl.pallas_call(
        _mod_kernel,
        grid=(n_sub, 3 * D // tn),
        in_specs=[
            pl.BlockSpec((rows, D), lambda s, j: (0, 0)),
            pl.BlockSpec((1, D, tn), lambda s, j: (s, 0, j)),
            pl.BlockSpec((1, 1, tn), lambda s, j: (s, 0, j)),
        ],
        out_specs=pl.BlockSpec((1, rows, tn), lambda s, j: (s, 0, j)),
        out_shape=jax.ShapeDtypeStruct((n_sub, rows, 3 * D), F32),
        compiler_params=_params("parallel", "parallel"),
        name="modulation",
    )(cp, w, b)
    return out[:, :B]


def _prenorm_kernel(x_ref, g_ref, sc_ref, sh_ref, o_ref):
    y = _rms(x_ref[...]) * g_ref[...]
    o_ref[...] = (y * (1.0 + sc_ref[0]) + sh_ref[0]).astype(o_ref.dtype)


def _prenorm(x2, g, scale, shift, S):
    M, D = x2.shape
    B = M // S
    tm = _tile(S, 512)
    per_b = S // tm
    return pl.pallas_call(
        _prenorm_kernel,
        grid=(M // tm,),
        in_specs=[
            pl.BlockSpec((tm, D), lambda i: (i, 0)),
            pl.BlockSpec((1, D), lambda i: (0, 0)),
            pl.BlockSpec((1, 1, D), lambda i: (i // per_b, 0, 0)),
            pl.BlockSpec((1, 1, D), lambda i: (i // per_b, 0, 0)),
        ],
        out_specs=pl.BlockSpec((tm, D), lambda i: (i, 0)),
        out_shape=jax.ShapeDtypeStruct((M, D), BF16),
        compiler_params=_params("parallel"),
        name="prenorm",
    )(x2, g.reshape(1, D), scale.reshape(B, 1, D), shift.reshape(B, 1, D))


def _mm_kernel(*refs, norm, rope):
    it = iter(refs)
    a_ref = next(it)
    g_ref = next(it) if norm else None
    w_ref = next(it)
    if rope:
        w2_ref, cos_ref, sin_ref = next(it), next(it), next(it)
    o_ref = next(it)
    a = a_ref[...]
    if norm:
        a = _rms(a.astype(F32)) * g_ref[...]
    a = a.astype(BF16)
    out = _dot(a, w_ref[...])
    if rope:
        out = out * cos_ref[...] + _dot(a, w2_ref[...]) * sin_ref[...]
    o_ref[...] = out.astype(o_ref.dtype)


def _matmul(a, w, *, S, g=None, w_rot=None, cos=None, sin=None, out_dtype=F32, tm=512, tn=512, name="matmul"):
    M, K = a.shape
    N = w.shape[1]
    tm = _tile(S, tm)
    tn = _tile(N, tn)
    per_b = S // tm
    norm, rope = g is not None, w_rot is not None
    in_specs = [pl.BlockSpec((tm, K), lambda i, j: (i, 0))]
    args = [a]
    if norm:
        in_specs.append(pl.BlockSpec((1, K), lambda i, j: (0, 0)))
        args.append(g.reshape(1, K))
    in_specs.append(pl.BlockSpec((K, tn), lambda i, j: (0, j)))
    args.append(w.astype(BF16))
    if rope:
        in_specs += [
            pl.BlockSpec((K, tn), lambda i, j: (0, j)),
            pl.BlockSpec((tm, tn), lambda i, j: (i % per_b, j)),
            pl.BlockSpec((tm, tn), lambda i, j: (i % per_b, j)),
        ]
        args += [w_rot.astype(BF16), cos, sin]
    return pl.pallas_call(
        functools.partial(_mm_kernel, norm=norm, rope=rope),
        grid=(M // tm, N // tn),
        in_specs=in_specs,
        out_specs=pl.BlockSpec((tm, tn), lambda i, j: (i, j)),
        out_shape=jax.ShapeDtypeStruct((M, N), out_dtype),
        compiler_params=_params("parallel", "parallel"),
        name=name,
    )(*args)


def _ffn_up_kernel(u_ref, wg_ref, wu_ref, o_ref):
    u = u_ref[...]
    a = _dot(u, wg_ref[...])
    b = _dot(u, wu_ref[...])
    o_ref[...] = (a * jax.nn.sigmoid(a) * b).astype(o_ref.dtype)


def _ffn_up(u, wg, wu, tm=512, tn=512):
    M, K = u.shape
    N = wg.shape[1]
    tm, tn = _tile(M, tm), _tile(N, tn)
    return pl.pallas_call(
        _ffn_up_kernel,
        grid=(M // tm, N // tn),
        in_specs=[
            pl.BlockSpec((tm, K), lambda i, j: (i, 0)),
            pl.BlockSpec((K, tn), lambda i, j: (0, j)),
            pl.BlockSpec((K, tn), lambda i, j: (0, j)),
        ],
        out_specs=pl.BlockSpec((tm, tn), lambda i, j: (i, j)),
        out_shape=jax.ShapeDtypeStruct((M, N), BF16),
        compiler_params=_params("parallel", "parallel"),
        name="ffn_up",
    )(u, wg.astype(BF16), wu.astype(BF16))


def _down_post_kernel(h_ref, w_ref, x_ref, g_ref, gate_ref, o_ref, acc_ref, *, res_w, nk):
    k = pl.program_id(1)

    @pl.when(k == 0)
    def _():
        acc_ref[...] = jnp.zeros_like(acc_ref)

    acc_ref[...] += _dot(h_ref[...], w_ref[...])

    @pl.when(k == nk - 1)
    def _():
        y = _rms(acc_ref[...]) * g_ref[...]
        o_ref[...] = x_ref[...] + res_w * gate_ref[0] * y


def _down_post(h, wd, x2, g, gate, res_w, S, tm=512, tk=512):
    M, K = h.shape
    D = wd.shape[1]
    B = M // S
    tm, tk = _tile(S, tm), _tile(K, tk)
    per_b = S // tm
    nk = K // tk
    return pl.pallas_call(
        functools.partial(_down_post_kernel, res_w=res_w, nk=nk),
        grid=(M // tm, nk),
        in_specs=[
            pl.BlockSpec((tm, tk), lambda i, k: (i, k)),
            pl.BlockSpec((tk, D), lambda i, k: (k, 0)),
            pl.BlockSpec((tm, D), lambda i, k: (i, 0)),
            pl.BlockSpec((1, D), lambda i, k: (0, 0)),
            pl.BlockSpec((1, 1, D), lambda i, k: (i // per_b, 0, 0)),
        ],
        out_specs=pl.BlockSpec((tm, D), lambda i, k: (i, 0)),
        out_shape=jax.ShapeDtypeStruct((M, D), F32),
        scratch_shapes=[pltpu.VMEM((tm, D), F32)],
        compiler_params=_params("parallel", "arbitrary"),
        name="down_post",
    )(h, wd.astype(BF16), x2, g.reshape(1, D), gate.reshape(B, 1, D))


def _merge_kernel(u_ref, wg_ref, y0_ref, y1_ref, y2_ref, y3_ref, wb_ref, o_ref):
    u = u_ref[...]
    acc = None
    for i, y_ref in enumerate((y0_ref, y1_ref, y2_ref, y3_ref)):
        gate = jax.nn.sigmoid(_dot(u, wg_ref[i]))
        term = gate * _dot(y_ref[...].astype(BF16), wb_ref[i])
        acc = term if acc is None else acc + term
    o_ref[...] = acc.astype(o_ref.dtype)


def _merge(u, w_gate, ys, w_branch, tm=512, tn=256):
    M, D = u.shape
    W = w_branch.shape[1]
    tm, tn = _tile(M, tm), _tile(D, tn)
    y_spec = pl.BlockSpec((tm, W), lambda i, j: (i, 0))
    return pl.pallas_call(
        _merge_kernel,
        grid=(M // tm, D // tn),
        in_specs=[
            pl.BlockSpec((tm, D), lambda i, j: (i, 0)),
            pl.BlockSpec((4, D, tn), lambda i, j: (0, 0, j)),
            y_spec, y_spec, y_spec, y_spec,
            pl.BlockSpec((4, W, tn), lambda i, j: (0, 0, j)),
        ],
        out_specs=pl.BlockSpec((tm, tn), lambda i, j: (i, j)),
        out_shape=jax.ShapeDtypeStruct((M, D), BF16),
        compiler_params=_params("parallel", "parallel"),
        name="merge",
    )(u, w_gate.astype(BF16), *ys, w_branch.astype(BF16))


def _flash_kernel(*refs, mode, scale, tq, tk, G, n_kv_steps):
    if mode == "select":
        q_ref, k_ref, v_ref, sel_ref, o_ref, m_ref, l_ref, acc_ref = refs
    else:
        q_ref, k_ref, v_ref, o_ref, m_ref, l_ref, acc_ref = refs
    i = pl.program_id(2)
    jj = pl.program_id(3)
    if mode == "window":
        j = i - (n_kv_steps - 1) + jj
        live = j >= 0
    else:
        j = jj
        live = j <= i
    rows = G * tq
    dv = acc_ref.shape[-1]

    @pl.when(jj == 0)
    def _():
        m_ref[...] = jnp.full_like(m_ref, NEG_INF)
        l_ref[...] = jnp.zeros_like(l_ref)
        acc_ref[...] = jnp.zeros_like(acc_ref)

    @pl.when(live)
    def _():
        q = q_ref[0, 0].reshape(rows, q_ref.shape[-1]).astype(BF16)
        k = k_ref[0, 0].astype(BF16)
        v = v_ref[0, 0].astype(BF16)
        s = _dot_nt(q, k) * scale
        q_pos = i * tq + (lax.broadcasted_iota(jnp.int32, (rows, tk), 0) & (tq - 1))
        k_pos = j * tk + lax.broadcasted_iota(jnp.int32, (rows, tk), 1)
        mask = k_pos <= q_pos
        if mode == "window":
            mask = jnp.logical_and(mask, k_pos > q_pos - WIN_SIZE)
        if mode == "select":
            n_blk = sel_ref.shape[-1]
            blk = lax.broadcasted_iota(jnp.int32, (n_blk, tk), 0)
            key_blk = jnp.right_shift(j * tk + lax.broadcasted_iota(jnp.int32, (n_blk, tk), 1), SLC_SHIFT)
            expand = jnp.where(blk == key_blk, 1.0, 0.0).astype(BF16)
            picked = _dot(sel_ref[0, 0].astype(BF16), expand)
            picked = jnp.concatenate([picked] * G, axis=0) if G > 1 else picked
            mask = jnp.logical_and(mask, picked > 0.5)
        s = jnp.where(mask, s, NEG_INF)
        m_prev = m_ref[...]
        m_new = jnp.maximum(m_prev, jnp.max(s, axis=-1, keepdims=True))
        p = jnp.where(mask, jnp.exp(s - m_new), 0.0)
        alpha = jnp.exp(m_prev - m_new)
        l_ref[...] = alpha * l_ref[...] + jnp.sum(p, axis=-1, keepdims=True)
        acc_ref[...] = alpha * acc_ref[...] + _dot(p.astype(BF16), v)
        m_ref[...] = m_new

    @pl.when(jj == n_kv_steps - 1)
    def _():
        o_ref[0, 0] = (acc_ref[...] / l_ref[...]).reshape(G, tq, dv)


def _flash(q, k, v, *, mode, scale, sel=None, tile=256):
    B, Hk, G, S, Dq = q.shape
    Dv = v.shape[-1]
    t = _tile(S, tile)
    assert t & (t - 1) == 0
    nq = S // t
    if mode == "window":
        assert WIN_SIZE % t == 0
        n_kv_steps = WIN_SIZE // t + 1
        kv_map = lambda b, h, i, jj: (b, h, jnp.maximum(i - (n_kv_steps - 1) + jj, 0), 0)
    else:
        n_kv_steps = nq
        kv_map = lambda b, h, i, jj: (b, h, jnp.minimum(jj, i), 0)
    in_specs = [
        pl.BlockSpec((1, 1, G, t, Dq), lambda b, h, i, jj: (b, h, 0, i, 0)),
        pl.BlockSpec((1, 1, t, Dq), kv_map),
        pl.BlockSpec((1, 1, t, Dv), kv_map),
    ]
    args = [q, k, v]
    if mode == "select":
        in_specs.append(pl.BlockSpec((1, 1, t, sel.shape[-1]), lambda b, h, i, jj: (b, h, i, 0)))
        args.append(sel)
    return pl.pallas_call(
        functools.partial(_flash_kernel, mode=mode, scale=scale, tq=t, tk=t, G=G, n_kv_steps=n_kv_steps),
        grid=(B, Hk, nq, n_kv_steps),
        in_specs=in_specs,
        out_specs=pl.BlockSpec((1, 1, G, t, Dv), lambda b, h, i, jj: (b, h, 0, i, 0)),
        out_shape=jax.ShapeDtypeStruct((B, Hk, G, S, Dv), F32),
        scratch_shapes=[pltpu.VMEM((G * t, 1), F32), pltpu.VMEM((G * t, 1), F32), pltpu.VMEM((G * t, Dv), F32)],
        compiler_params=_params("parallel", "parallel", "parallel", "arbitrary"),
        name="attn_" + mode,
    )(*args)


def _nsa_compress_kernel(t_ref, pos_ref, w1_ref, w2_ref, o_ref):
    n_pieces, half = t_ref.shape[-2], t_ref.shape[-1]
    for which in range(2):
        t = t_ref[which, 0, 0]
        w1 = w1_ref[which]
        first = _dot(t, w1[:half], HI)
        second = _dot(t, w1[half:], HI)
        pos_bias = _dot(pos_ref[which], w1, HI)
        pre = first + pltpu.roll(second, n_pieces - 1, 0) + pos_bias
        o_ref[which, 0, 0] = _dot(jax.nn.gelu(pre), w2_ref[which], HI)


def _nsa_compress(kv_flat, pos, w1, w2):
    _, B, Hk, n_pieces, half = kv_flat.shape
    hid, d = w2.shape[1], w2.shape[2]
    return pl.pallas_call(
        _nsa_compress_kernel,
        grid=(B, Hk),
        in_specs=[
            pl.BlockSpec((2, 1, 1, n_pieces, half), lambda b, h: (0, b, h, 0, 0)),
            pl.BlockSpec((2, 1, 2 * half), lambda b, h: (0, 0, 0)),
            pl.BlockSpec((2, 2 * half, hid), lambda b, h: (0, 0, 0)),
            pl.BlockSpec((2, hid, d), lambda b, h: (0, 0, 0)),
        ],
        out_specs=pl.BlockSpec((2, 1, 1, n_pieces, d), lambda b, h: (0, b, h, 0, 0)),
        out_shape=jax.ShapeDtypeStruct((2, B, Hk, n_pieces, d), F32),
        compiler_params=_params("parallel", "parallel"),
        name="nsa_compress",
    )(kv_flat, pos, w1, w2)


def _nsa_cmp_kernel(q_ref, kc_ref, vc_ref, ov_ref, o_ref, sel_ref, *, scale, tq, G):
    i = pl.program_id(2)
    rows = G * tq
    d = q_ref.shape[-1]
    n_cmp = kc_ref.shape[-2]
    n_sel = ov_ref.shape[-1]
    q = q_ref[0, 0].reshape(rows, d)
    s = _dot_nt(q, kc_ref[0, 0], HI) * scale
    t_pos = i * tq + (lax.broadcasted_iota(jnp.int32, (rows, n_cmp), 0) & (tq - 1))
    blk_end = lax.broadcasted_iota(jnp.int32, (rows, n_cmp), 1) * CMP_STRIDE + (CMP_BLOCK - 1)
    mask = blk_end <= t_pos
    s = jnp.where(mask, s, NEG_INF)
    e = jnp.exp(s - jnp.max(s, axis=-1, keepdims=True))
    p = jnp.where(mask, e / jnp.sum(e, axis=-1, keepdims=True), 0.0)
    o_ref[0, 0] = _dot(p, vc_ref[0, 0], HI).reshape(G, tq, d)

    p_group = p[0:tq]
    for g in range(1, G):
        p_group = p_group + p[g * tq:(g + 1) * tq]
    imp = _dot(p_group, ov_ref[...], HI)
    blk = lax.broadcasted_iota(jnp.int32, (tq, n_sel), 1)
    cur = jnp.right_shift(i * tq + lax.broadcasted_iota(jnp.int32, (tq, n_sel), 0), SLC_SHIFT)
    imp = jnp.where(blk <= cur, jnp.where(blk == 0, FORCE_SCORE, jnp.where(blk >= cur - 1, FORCE_SCORE, imp)),
                    NEG_INF)
    rank = jnp.zeros((tq, n_sel), F32)
    for c in range(n_sel):
        col = imp[:, c:c + 1]
        later = jnp.where(blk > c, 1.0, 0.0)
        rank = rank + jnp.where(col > imp, 1.0, 0.0) + jnp.where(col == imp, later, 0.0)
    sel_ref[0, 0] = jnp.where(rank < float(min(SLC_TOPK, n_sel)), 1.0, 0.0)


def _nsa_cmp(q, kc, vc, overlap, tile=256):
    B, Hk, G, S, d = q.shape
    n_cmp = kc.shape[2]
    n_sel = overlap.shape[1]
    t = _tile(S, tile)
    return pl.pallas_call(
        functools.partial(_nsa_cmp_kernel, scale=d ** -0.5, tq=t, G=G),
        grid=(B, Hk, S // t),
        in_specs=[
            pl.BlockSpec((1, 1, G, t, d), lambda b, h, i: (b, h, 0, i, 0)),
            pl.BlockSpec((1, 1, n_cmp, d), lambda b, h, i: (b, h, 0, 0)),
            pl.BlockSpec((1, 1, n_cmp, d), lambda b, h, i: (b, h, 0, 0)),
            pl.BlockSpec((n_cmp, n_sel), lambda b, h, i: (0, 0)),
        ],
        out_specs=[
            pl.BlockSpec((1, 1, G, t, d), lambda b, h, i: (b, h, 0, i, 0)),
            pl.BlockSpec((1, 1, t, n_sel), lambda b, h, i: (b, h, i, 0)),
        ],
        out_shape=[
            jax.ShapeDtypeStruct((B, Hk, G, S, d), F32),
            jax.ShapeDtypeStruct((B, Hk, S, n_sel), F32),
        ],
        compiler_params=_params("parallel", "parallel", "parallel"),
        name="nsa_cmp",
    )(q, kc, vc, overlap)


def _nsa_combine_kernel(gl_ref, ex_ref, oc_ref, os_ref, ow_ref, o_ref):
    width = oc_ref.shape[-1]
    gates = _dot(jax.nn.sigmoid(gl_ref[...]), ex_ref[...], HI)
    o_ref[...] = (gates[:, :width] * oc_ref[...] + gates[:, width:2 * width] * os_ref[...]
                  + gates[:, 2 * width:] * ow_ref[...])


def _nsa_combine(gl, expand, o_cmp, o_slc, o_win, tm=512):
    M, width = o_cmp.shape
    tm = _tile(M, tm)
    o_spec = pl.BlockSpec((tm, width), lambda i: (i, 0))
    return pl.pallas_call(
        _nsa_combine_kernel,
        grid=(M // tm,),
        in_specs=[
            pl.BlockSpec((tm, gl.shape[1]), lambda i: (i, 0)),
            pl.BlockSpec(expand.shape, lambda i: (0, 0)),
            o_spec, o_spec, o_spec,
        ],
        out_specs=o_spec,
        out_shape=jax.ShapeDtypeStruct((M, width), F32),
        compiler_params=_params("parallel"),
        name="nsa_combine",
    )(gl, expand, o_cmp, o_slc, o_win)


def _rwkv_prep_kernel(p_ref, mu_ref, w0_ref, ww2_ref, a0_ref, aw2_ref, gw2_ref, kk_ref, ka_ref, seg_ref,
                      r_o, lw_o, k_o, v_o, kk_o, b_o, g_o, last_ref):
    tb = p_ref.shape[1]
    width = r_o.shape[-1]

    @pl.when(pl.program_id(1) == 0)
    def _():
        last_ref[...] = jnp.zeros_like(last_ref)

    p = p_ref[0]
    row = lax.broadcasted_iota(jnp.int32, p.shape, 0)
    prev = jnp.where(row == 0, last_ref[...], pltpu.roll(p, 1, 0))
    last_ref[...] = p[tb - 1:tb]
    xs = p + (prev - p) * mu_ref[...]
    r = xs[:, 0:width]
    k = xs[:, width:2 * width]
    v = xs[:, 2 * width:3 * width]
    lora = xs[:, 3 * width:3 * width + 128]
    g_lo = xs[:, 3 * width + 128:]
    lw = -math.exp(-0.5) * jax.nn.sigmoid(w0_ref[...] + _dot(jnp.tanh(lora), ww2_ref[...], HI))
    a = jax.nn.sigmoid(a0_ref[...] + _dot(lora, aw2_ref[...], HI))
    g = _dot(jax.nn.sigmoid(g_lo), gw2_ref[...], HI)
    kk = k * kk_ref[...]
    norm = jnp.sqrt(_dot(kk * kk, seg_ref[...], HI))
    kk = kk / jnp.maximum(norm, 1e-12)
    r_o[0] = r
    lw_o[0] = lw
    k_o[0] = k * (1.0 + (a - 1.0) * ka_ref[...])
    v_o[0] = v
    kk_o[0] = kk
    b_o[0] = kk * a
    g_o[0] = g


def _rwkv_prep(p, consts, tb=256):
    B, S, P = p.shape
    width = consts[1].shape[-1]
    tb = _tile(S, tb)
    const_specs = [pl.BlockSpec(c.shape, lambda b, t: (0, 0)) for c in consts]
    o_spec = pl.BlockSpec((1, tb, width), lambda b, t: (b, t, 0))
    return pl.pallas_call(
        _rwkv_prep_kernel,
        grid=(B, S // tb),
        in_specs=[pl.BlockSpec((1, tb, P), lambda b, t: (b, t, 0))] + const_specs,
        out_specs=[o_spec] * 7,
        out_shape=[jax.ShapeDtypeStruct((B, S, width), F32)] * 7,
        scratch_shapes=[pltpu.VMEM((1, P), F32)],
        compiler_params=_params("parallel", "arbitrary"),
        name="rwkv_prep",
    )(p, *consts)


def _rwkv_chunk(r, lw, k, v, kk, bb, state, ltri, strict, incl, eye):
    C, n = r.shape
    g = _dot(ltri, lw, HI)
    g_last = g[C - 1:C]
    decay_in = jnp.exp(g)
    decay_out = jnp.exp(-g)
    decay_end = jnp.exp(g_last - g)
    rg = r * decay_in
    kkg = kk * jnp.exp(g - lw)
    big = _dot_nt(jnp.concatenate([kkg, rg], axis=0), jnp.concatenate([bb * decay_out, k * decay_out], axis=0))
    a_mat = jnp.where(strict, big[:C, :C], 0.0)
    bk = jnp.where(strict, big[:C, C:], 0.0)
    ay = jnp.where(incl, big[C:, :C], 0.0)
    by = jnp.where(incl, big[C:, C:], 0.0)
    power = -a_mat
    inv = eye + power
    for _ in range(int(math.log2(C)) - 1):
        power = _dot(power, power)
        inv = inv + _dot(inv, power)
    tx = _dot(inv, jnp.concatenate([kkg, _dot(bk, v)], axis=1))
    ayx = _dot(ay, tx)
    q_hat = rg - ayx[:, :n]
    y_intra = _dot(by, v) - ayx[:, n:]
    qt = _dot_nt(jnp.concatenate([q_hat, tx[:, :n]], axis=0), state)
    y = qt[:C] + y_intra
    u = qt[C:] + tx[:, n:]
    new_state = state * jnp.exp(g_last) + _dot_tn(
        jnp.concatenate([v, u], axis=0), jnp.concatenate([k * decay_end, -(bb * decay_end)], axis=0))
    return y, new_state


def _rwkv_scan_kernel(r_ref, lw_ref, k_ref, v_ref, kk_ref, b_ref, g_ref, lnw_ref, lnb_ref, rk_ref, seg_ref,
                      ltri_ref, o_ref, state_ref, y_ref, *, heads, chunk):
    tb = r_ref.shape[1]
    n = r_ref.shape[2] // heads

    @pl.when(pl.program_id(1) == 0)
    def _():
        state_ref[...] = jnp.zeros_like(state_ref)

    row = lax.broadcasted_iota(jnp.int32, (chunk, chunk), 0)
    col = lax.broadcasted_iota(jnp.int32, (chunk, chunk), 1)
    strict, incl = col < row, col <= row
    eye = jnp.where(col == row, 1.0, 0.0)
    ltri = ltri_ref[...]
    for c in range(tb // chunk):
        rows = pl.ds(c * chunk, chunk)
        for h in range(heads):
            cols = pl.ds(h * n, n)
            y, new_state = _rwkv_chunk(r_ref[0, rows, cols], lw_ref[0, rows, cols], k_ref[0, rows, cols],
                                       v_ref[0, rows, cols], kk_ref[0, rows, cols], b_ref[0, rows, cols],
                                       state_ref[h], ltri, strict, incl, eye)
            state_ref[h] = new_state
            y_ref[rows, cols] = y
    y = y_ref[...]
    seg = seg_ref[...]
    mean = _dot(y, seg, HI) * (1.0 / n)
    dev = y - mean
    var = _dot(dev * dev, seg, HI) * (1.0 / n)
    yn = dev * lax.rsqrt(var + RWKV_GN_EPS) * lnw_ref[...] + lnb_ref[...]
    bonus = _dot(r_ref[0] * k_ref[0] * rk_ref[...], seg, HI) * v_ref[0]
    o_ref[0] = (yn + bonus) * g_ref[0]


def _rwkv_scan(r, lw, k, v, kk, bb, g, ln_w, ln_b, r_k, seg, tb=256):
    B, S, width = r.shape
    heads = RWKV_HEADS
    n = width // heads
    chunk = min(RWKV_CHUNK, S)
    tb = _tile(S, tb)
    ltri = jnp.asarray(np.tril(np.ones((chunk, chunk), np.float32)))
    x_spec = pl.BlockSpec((1, tb, width), lambda b, t: (b, t, 0))
    vec_spec = pl.BlockSpec((1, width), lambda b, t: (0, 0))
    return pl.pallas_call(
        functools.partial(_rwkv_scan_kernel, heads=heads, chunk=chunk),
        grid=(B, S // tb),
        in_specs=[x_spec] * 7 + [vec_spec] * 3 + [
            pl.BlockSpec((width, width), lambda b, t: (0, 0)),
            pl.BlockSpec((chunk, chunk), lambda b, t: (0, 0)),
        ],
        out_specs=x_spec,
        out_shape=jax.ShapeDtypeStruct((B, S, width), F32),
        scratch_shapes=[pltpu.VMEM((heads, n, n), F32), pltpu.VMEM((tb, width), F32)],
        compiler_params=_params("parallel", "arbitrary"),
        name="rwkv_scan",
    )(r, lw, k, v, kk, bb, g, ln_w, ln_b, r_k, seg, ltri)


def _gla_kernel(q_ref, k_ref, v_ref, r_ref, al_ref, aw_ref, ab_ref, ng_ref, ltri_ref, o_ref, state_ref,
                *, heads, chunk):
    tb = v_ref.shape[1]
    dk = q_ref.shape[-1]
    dv = v_ref.shape[-1] // heads

    @pl.when(pl.program_id(1) == 0)
    def _():
        state_ref[...] = jnp.zeros_like(state_ref)

    row = lax.broadcasted_iota(jnp.int32, (chunk, chunk), 0)
    col = lax.broadcasted_iota(jnp.int32, (chunk, chunk), 1)
    incl = col <= row
    ltri = ltri_ref[...]
    for c in range(tb // chunk):
        rows = pl.ds(c * chunk, chunk)
        alpha_lo = al_ref[0, rows, :]
        for h in range(heads):
            cols = pl.ds(h * dv, dv)
            x = _dot(alpha_lo, aw_ref[h], HI) + ab_ref[h]
            log_a = (jnp.minimum(x, 0.0) - jnp.log(1.0 + jnp.exp(-jnp.abs(x)))) * (1.0 / GLA_TAU)
            b = _dot(ltri, log_a, HI)
            b_last = b[chunk - 1:chunk]
            q = q_ref[0, h, rows, :]
            k = k_ref[0, h, rows, :]
            v = v_ref[0, rows, cols]
            qe = q * (dk ** -0.5) * jnp.exp(b)
            attn = jnp.where(incl, _dot_nt(qe, k * jnp.exp(-b)), 0.0)
            state = state_ref[h]
            o = _dot(attn, v) + _dot_nt(qe, state)
            state_ref[h] = state * jnp.exp(b_last) + _dot_tn(v, k * jnp.exp(b_last - b))
            o = o * lax.rsqrt(jnp.mean(o * o, axis=-1, keepdims=True) + NORM_EPS)
            gate = r_ref[0, rows, cols]
            o_ref[0, rows, cols] = o * ng_ref[:, cols] * (gate * jax.nn.sigmoid(gate))


def _gla(q, k, v, r, alpha_lo, aw, ab, ng, tb=256):
    B, H, S, dk = q.shape
    width = v.shape[-1]
    lora = alpha_lo.shape[-1]
    chunk = min(GLA_CHUNK, S)
    tb = _tile(S, tb)
    ltri = jnp.asarray(np.tril(np.ones((chunk, chunk), np.float32)))
    qk_spec = pl.BlockSpec((1, H, tb, dk), lambda b, t: (b, 0, t, 0))
    x_spec = pl.BlockSpec((1, tb, width), lambda b, t: (b, t, 0))
    return pl.pallas_call(
        functools.partial(_gla_kernel, heads=H, chunk=chunk),
        grid=(B, S // tb),
        in_specs=[
            qk_spec, qk_spec, x_spec, x_spec,
            pl.BlockSpec((1, tb, lora), lambda b, t: (b, t, 0)),
            pl.BlockSpec(aw.shape, lambda b, t: (0, 0, 0)),
            pl.BlockSpec(ab.shape, lambda b, t: (0, 0, 0)),
            pl.BlockSpec(ng.shape, lambda b, t: (0, 0)),
            pl.BlockSpec((chunk, chunk), lambda b, t: (0, 0)),
        ],
        out_specs=x_spec,
        out_shape=jax.ShapeDtypeStruct((B, S, width), F32),
        scratch_shapes=[pltpu.VMEM((H, width // H, dk), F32)],
        compiler_params=_params("parallel", "arbitrary"),
        name="gla",
    )(q, k, v, r, alpha_lo, aw, ab, ng, ltri)


def _rope_tables(S, d):
    inv = ROPE_THETA ** (-jnp.arange(0, d, 2, dtype=F32) / d)
    ang = jnp.arange(S).astype(F32)[:, None] * inv[None, :]
    cos, sin = jnp.cos(ang), jnp.sin(ang)
    return jnp.concatenate([cos, cos], axis=-1), jnp.concatenate([sin, sin], axis=-1)


def _rot_cols(w, d):
    K = w.shape[0]
    w = w.reshape(K, -1, 2, d // 2)
    return jnp.concatenate([-w[:, :, 1], w[:, :, 0]], axis=-1).reshape(K, -1)


def _nsa_constants(S):
    n_pieces = S // CMP_STRIDE
    n_cmp = (S - CMP_BLOCK) // CMP_STRIDE + 1
    n_sel = S // SLC_BLOCK
    start = np.arange(n_cmp)[:, None] * CMP_STRIDE
    end = start + CMP_BLOCK - 1
    blk = np.arange(n_sel)[None, :]
    overlap = np.zeros((n_pieces, n_sel), np.float32)
    overlap[:n_cmp] = ((start < (blk + 1) * SLC_BLOCK) & (end >= blk * SLC_BLOCK)).astype(np.float32)
    width = NSA_HEADS * HEAD_DIM
    expand = np.zeros((128, 3 * width), np.float32)
    for h in range(NSA_HEADS):
        for i in range(3):
            expand[3 * h + i, i * width + h * HEAD_DIM:i * width + (h + 1) * HEAD_DIM] = 1.0
    return jnp.asarray(overlap), jnp.asarray(expand)


def _token_mixing(u, B, S, w_in, w_branch, cmp_pos, cmp_w1, cmp_w2, rwkv_mu, rwkv_w0, rwkv_w_w2, rwkv_a0,
                  rwkv_a_w2, rwkv_g_w2, rwkv_k_k, rwkv_k_a, rwkv_r_k, rwkv_ln_w, rwkv_ln_b, gla_alpha_w2,
                  gla_alpha_b, gla_norm_g, mla_q_norm, mla_w_uq, mla_kv_norm, mla_w_ukv):
    M, D = u.shape
    d = HEAD_DIM
    width = D // 4
    sizes = [width + 6 * NSA_KV_HEADS * d + 3 * NSA_HEADS, 3 * width + 256, 2 * (width // 2) + 2 * width + 16,
             384 + 128 + MLA_ROPE, 4 * D]
    offs = np.cumsum([0] + sizes)
    assert offs[-1] == w_in.shape[1]
    w_nsa, w_rwkv, w_gla, w_mla, w_gate = [w_in[:, offs[i]:offs[i + 1]] for i in range(5)]
    zeros = lambda n: jnp.zeros((D, n), F32)
    kvw = NSA_KV_HEADS * d

    wq = w_nsa[:, :width]
    wkc, wvc, wks, wvs, wkw, wvw = [w_nsa[:, width + i * kvw:width + (i + 1) * kvw] for i in range(6)]
    wgl = w_nsa[:, width + 6 * kvw:]
    wkr = w_mla[:, 512:512 + MLA_ROPE]
    n_rope = width + 3 * kvw + MLA_ROPE
    pad = (-n_rope) % 512
    w_a = jnp.concatenate([wq, wkc, wks, wkw, wkr, zeros(pad)], axis=1)
    w_b = jnp.concatenate([_rot_cols(wq, d), _rot_cols(wkc, d), _rot_cols(wks, d), _rot_cols(wkw, d),
                           _rot_cols(wkr, MLA_ROPE), zeros(pad)], axis=1)
    cos_d, sin_d = _rope_tables(S, d)
    cos_r, sin_r = _rope_tables(S, MLA_ROPE)
    reps = (width + 3 * kvw) // d
    cos_t = jnp.concatenate([jnp.tile(cos_d, (1, reps)), cos_r, jnp.ones((S, pad), F32)], axis=1)
    sin_t = jnp.concatenate([jnp.tile(sin_d, (1, reps)), sin_r, jnp.zeros((S, pad), F32)], axis=1)
    roped = _matmul(u, w_a, S=S, w_rot=w_b, cos=cos_t, sin=sin_t, name="proj_rope").reshape(B, S, -1)
    n_plain = 3 * kvw + 3 * NSA_HEADS
    plain = _matmul(u, jnp.concatenate([wvc, wvs, wvw, wgl, zeros((-n_plain) % 128)], axis=1), S=S,
                    name="proj_nsa_v")

    q = roped[..., :width].reshape(B, S, NSA_KV_HEADS, NSA_GROUP, d).transpose(0, 2, 3, 1, 4)
    kv_heads = lambda t: t.reshape(B, S, NSA_KV_HEADS, d).transpose(0, 2, 1, 3)
    kc_in = kv_heads(roped[..., width:width + kvw])
    ks = kv_heads(roped[..., width + kvw:width + 2 * kvw])
    kw = kv_heads(roped[..., width + 2 * kvw:width + 3 * kvw])
    k_pe = roped[..., width + 3 * kvw:width + 3 * kvw + MLA_ROPE]
    plain3 = plain.reshape(B, S, -1)
    vc_in, vs, vw = [kv_heads(plain3[..., i * kvw:(i + 1) * kvw]) for i in range(3)]
    gate_logits = plain[:, 3 * kvw:3 * kvw + 128]

    overlap, expand = _nsa_constants(S)
    n_pieces = S // CMP_STRIDE
    kv_flat = jnp.stack([kc_in, vc_in]).reshape(2, B, NSA_KV_HEADS, n_pieces, CMP_STRIDE * d)
    cmp = _nsa_compress(kv_flat, cmp_pos.reshape(2, 1, CMP_BLOCK * d), cmp_w1, cmp_w2)
    o_cmp, sel = _nsa_cmp(q, cmp[0], cmp[1], overlap)
    o_slc = _flash(q, ks, vs, mode="select", scale=d ** -0.5, sel=sel)
    o_win = _flash(q, kw, vw, mode="window", scale=d ** -0.5)
    to_tokens = lambda o: o.transpose(0, 3, 1, 2, 4).reshape(M, width)
    y_nsa = _nsa_combine(gate_logits, expand, to_tokens(o_cmp), to_tokens(o_slc), to_tokens(o_win))

    regroup = lambda t: jnp.concatenate(
        [t[..., 0:width], t[..., width + 64:2 * width + 64], t[..., 2 * width + 64:3 * width + 64],
         t[..., width:width + 64], t[..., 3 * width + 64:3 * width + 128], t[..., 3 * width + 128:]], axis=-1)
    p_rwkv = _matmul(u, regroup(w_rwkv), S=S, tn=256, name="proj_rwkv").reshape(B, S, -1)
    row = lambda t: t.reshape(1, -1)
    seg = jnp.asarray(np.kron(np.eye(RWKV_HEADS), np.ones((d, d))).astype(np.float32))
    lora_pad = jnp.zeros((64, width), F32)
    consts = [row(regroup(rwkv_mu)), row(rwkv_w0), jnp.concatenate([rwkv_w_w2, lora_pad], axis=0), row(rwkv_a0),
              jnp.concatenate([lora_pad, rwkv_a_w2], axis=0), rwkv_g_w2, row(rwkv_k_k), row(rwkv_k_a), seg]
    r, lw, k2, v, kk, bb, g = _rwkv_prep(p_rwkv, consts)
    y_rwkv = _rwkv_scan(r, lw, k2, v, kk, bb, g, row(rwkv_ln_w), row(rwkv_ln_b), row(rwkv_r_k), seg)
    y_rwkv = y_rwkv.reshape(M, width)

    dk_all = width // 2
    w_gla_p = jnp.concatenate([w_gla[:, :2 * dk_all + width], w_gla[:, 2 * dk_all + width + 16:],
                               w_gla[:, 2 * dk_all + width:2 * dk_all + width + 16], zeros(240)], axis=1)
    p_gla = _matmul(u, w_gla_p, S=S, tn=256, name="proj_gla").reshape(B, S, -1)
    dkh = dk_all // GLA_HEADS
    gla_heads = lambda t: t.reshape(B, S, GLA_HEADS, dkh).transpose(0, 2, 1, 3)
    y_gla = _gla(gla_heads(p_gla[..., :dk_all]), gla_heads(p_gla[..., dk_all:2 * dk_all]),
                 p_gla[..., 2 * dk_all:2 * dk_all + width], p_gla[..., 2 * dk_all + width:2 * dk_all + 2 * width],
                 p_gla[..., 2 * dk_all + 2 * width:2 * dk_all + 2 * width + 16],
                 gla_alpha_w2.reshape(16, GLA_HEADS, dkh).transpose(1, 0, 2), gla_alpha_b.reshape(GLA_HEADS, 1, dkh),
                 row(gla_norm_g)).reshape(M, width)

    c_q = _matmul(u, w_mla[:, :384], S=S, name="proj_mla_q")
    c_kv = _matmul(u, w_mla[:, 384:512], S=S, name="proj_mla_kv")
    H = MLA_HEADS
    wq3 = mla_w_uq.reshape(384, H, MLA_NOPE + MLA_ROPE)
    rp = wq3[:, :, MLA_NOPE:]
    half = MLA_ROPE // 2
    z = lambda n: jnp.zeros((384, H, n), F32)
    wq_a = jnp.concatenate([wq3, z(128 - MLA_NOPE - MLA_ROPE)], axis=-1).reshape(384, H * 128)
    wq_b = jnp.concatenate([z(MLA_NOPE), -rp[..., half:], rp[..., :half], z(128 - MLA_NOPE - MLA_ROPE)],
                           axis=-1).reshape(384, H * 128)
    tail = 128 - MLA_NOPE - MLA_ROPE
    cos_h = jnp.concatenate([jnp.ones((S, MLA_NOPE), F32), cos_r, jnp.ones((S, tail), F32)], axis=1)
    sin_h = jnp.concatenate([jnp.zeros((S, MLA_NOPE), F32), sin_r, jnp.zeros((S, tail), F32)], axis=1)
    q_m = _matmul(c_q, wq_a, S=S, g=mla_q_norm, w_rot=wq_b, cos=jnp.tile(cos_h, (1, H)), sin=jnp.tile(sin_h, (1, H)),
                  name="mla_q")
    q_m = q_m.reshape(B, S, H, 128).transpose(0, 2, 1, 3)[:, :, None]
    kv_m = _matmul(c_kv, mla_w_ukv, S=S, g=mla_kv_norm, name="mla_kv").reshape(B, S, H, MLA_NOPE + width // H)
    k_m = jnp.concatenate([kv_m[..., :MLA_NOPE], jnp.broadcast_to(k_pe[:, :, None, :], (B, S, H, MLA_ROPE)),
                           jnp.zeros((B, S, H, tail), F32)], axis=-1).transpose(0, 2, 1, 3)
    v_m = kv_m[..., MLA_NOPE:].transpose(0, 2, 1, 3)
    o_mla = _flash(q_m, k_m, v_m, mode="causal", scale=(MLA_NOPE + MLA_ROPE) ** -0.5)
    y_mla = o_mla[:, :, 0].transpose(0, 2, 1, 3).reshape(M, width)

    w_gate4 = w_gate.reshape(D, 4, D).transpose(1, 0, 2)
    return _merge(u, w_gate4, (y_nsa, y_rwkv, y_gla, y_mla), w_branch)


def kernel(x, c, ada_w, ada_b, pre_g, post_g, ffn_wg, ffn_wu, ffn_wd, mix_w_in, mix_w_branch, mix_w_out, nsa_cmp_pos, nsa_cmp_w1, nsa_cmp_w2, rwkv_mu, rwkv_w0, rwkv_w_w2, rwkv_a0, rwkv_a_w2, rwkv_g_w2, rwkv_k_k, rwkv_k_a, rwkv_r_k, rwkv_ln_w, rwkv_ln_b, gla_alpha_w2, gla_alpha_b, gla_norm_g, mla_q_norm, mla_w_uq, mla_kv_norm, mla_w_ukv):
    B, S, D = x.shape
    M = B * S
    depth = ada_w.shape[0]
    mod = _modulation(c, ada_w, ada_b)
    x2 = x.reshape(M, D)
    for l in range(depth):
        for i in range(3):
            m = mod[3 * l + i]
            shift, scale, gate = m[:, :D], m[:, D:2 * D], m[:, 2 * D:]
            u = _prenorm(x2, pre_g[l, i], scale, shift, S)
            if i == 1:
                y = _token_mixing(
                    u, B, S, mix_w_in[l], mix_w_branch[l], nsa_cmp_pos[l], nsa_cmp_w1[l], nsa_cmp_w2[l], rwkv_mu[l],
                    rwkv_w0[l], rwkv_w_w2[l], rwkv_a0[l], rwkv_a_w2[l], rwkv_g_w2[l], rwkv_k_k[l], rwkv_k_a[l],
                    rwkv_r_k[l], rwkv_ln_w[l], rwkv_ln_b[l], gla_alpha_w2[l], gla_alpha_b[l], gla_norm_g[l],
                    mla_q_norm[l], mla_w_uq[l], mla_kv_norm[l], mla_w_ukv[l])
                x2 = _down_post(y, mix_w_out[l], x2, post_g[l, i], gate, 1.0, S)
            else:
                j = 0 if i == 0 else 1
                h = _ffn_up(u, ffn_wg[l, j], ffn_wu[l, j])
                x2 = _down_post(h, ffn_wd[l, j], x2, post_g[l, i], gate, 0.5, S)
    return x2.reshape(B, S, D)
```

```python
import functools
import math

import jax
import jax.numpy as jnp
import numpy as np
from jax import lax
from jax.experimental import pallas as pl
from jax.experimental.pallas import tpu as pltpu

F32 = jnp.float32
BF16 = jnp.bfloat16
HI = lax.Precision.HIGHEST

NORM_EPS = 1e-6
NEG_INF = -1e30
FORCE_SCORE = 1e30
ROPE_THETA = 10000.0
HEAD_DIM = 64

NSA_HEADS = 8
NSA_KV_HEADS = 2
NSA_GROUP = NSA_HEADS // NSA_KV_HEADS
CMP_BLOCK = 32
CMP_STRIDE = 16
SLC_BLOCK = 64
SLC_SHIFT = 6
SLC_TOPK = 16
WIN_SIZE = 512
RWKV_HEADS = 8
RWKV_GN_EPS = 64e-5
RWKV_CHUNK = 64
GLA_HEADS = 4
GLA_TAU = 16.0
GLA_CHUNK = 64
MLA_HEADS = 8
MLA_NOPE = 64
MLA_ROPE = 32

VMEM_LIMIT_BYTES = 56 * 1024 * 1024


def _params(*sem):
    return pltpu.CompilerParams(dimension_semantics=sem, vmem_limit_bytes=VMEM_LIMIT_BYTES)


def _dot(a, b, precision=None):
    return jnp.dot(a, b, preferred_element_type=F32, precision=precision)


def _dot_nt(a, b, precision=None):
    return lax.dot_general(a, b, (((1,), (1,)), ((), ())), preferred_element_type=F32, precision=precision)


def _dot_tn(a, b, precision=None):
    return lax.dot_general(a, b, (((0,), (0,)), ((), ())), preferred_element_type=F32, precision=precision)


def _rms(x):
    return x * lax.rsqrt(jnp.mean(x * x, axis=-1, keepdims=True) + NORM_EPS)


def _tile(n, t):
    t = min(n, t)
    assert n % t == 0, (n, t)
    return t


def _mod_kernel(c_ref, w_ref, b_ref, o_ref):
    c = c_ref[...]
    o_ref[0] = _dot(c * jax.nn.sigmoid(c), w_ref[0], HI) + b_ref[0]


def _modulation(c, ada_w, ada_b):
    B, D = c.shape
    n_sub = ada_w.shape[0] * ada_w.shape[1]
    w = ada_w.reshape(n_sub, D, 3 * D)
    b = ada_b.reshape(n_sub, 1, 3 * D)
    rows = 8
    cp = jnp.zeros((rows, D), F32).at[:B].set(c)
    tn = _tile(3 * D, 512)
    out = pl.pallas_call(
        _mod_kernel,
        grid=(n_sub, 3 * D // tn),
        in_specs=[
            pl.BlockSpec((rows, D), lambda s, j: (0, 0)),
            pl.BlockSpec((1, D, tn), lambda s, j: (s, 0, j)),
            pl.BlockSpec((1, 1, tn), lambda s, j: (s, 0, j)),
        ],
        out_specs=pl.BlockSpec((1, rows, tn), lambda s, j: (s, 0, j)),
        out_shape=jax.ShapeDtypeStruct((n_sub, rows, 3 * D), F32),
        compiler_params=_params("parallel", "parallel"),
        name="modulation",
    )(cp, w, b)
    return out[:, :B]


def _prenorm_kernel(x_ref, g_ref, sc_ref, sh_ref, o_ref):
    y = _rms(x_ref[...]) * g_ref[...]
    o_ref[...] = (y * (1.0 + sc_ref[0]) + sh_ref[0]).astype(o_ref.dtype)


def _prenorm(x2, g, scale, shift, S):
    M, D = x2.shape
    B = M // S
    tm = _tile(S, 512)
    per_b = S // tm
    return pl.pallas_call(
        _prenorm_kernel,
        grid=(M // tm,),
        in_specs=[
            pl.BlockSpec((tm, D), lambda i: (i, 0)),
            pl.BlockSpec((1, D), lambda i: (0, 0)),
            pl.BlockSpec((1, 1, D), lambda i: (i // per_b, 0, 0)),
            pl.BlockSpec((1, 1, D), lambda i: (i // per_b, 0, 0)),
        ],
        out_specs=pl.BlockSpec((tm, D), lambda i: (i, 0)),
        out_shape=jax.ShapeDtypeStruct((M, D), BF16),
        compiler_params=_params("parallel"),
        name="prenorm",
    )(x2, g.reshape(1, D), scale.reshape(B, 1, D), shift.reshape(B, 1, D))


def _mm_kernel(*refs, norm, rope):
    it = iter(refs)
    a_ref = next(it)
    g_ref = next(it) if norm else None
    w_ref = next(it)
    if rope:
        w2_ref, cos_ref, sin_ref = next(it), next(it), next(it)
    o_ref = next(it)
    a = a_ref[...]
    if norm:
        a = _rms(a.astype(F32)) * g_ref[...]
    a = a.astype(BF16)
    out = _dot(a, w_ref[...])
    if rope:
        out = out * cos_ref[...] + _dot(a, w2_ref[...]) * sin_ref[...]
    o_ref[...] = out.astype(o_ref.dtype)


def _matmul(a, w, *, S, g=None, w_rot=None, cos=None, sin=None, out_dtype=F32, tm=512, tn=512, name="matmul"):
    M, K = a.shape
    N = w.shape[1]
    tm = _tile(S, tm)
    tn = _tile(N, tn)
    per_b = S // tm
    norm, rope = g is not None, w_rot is not None
    in_specs = [pl.BlockSpec((tm, K), lambda i, j: (i, 0))]
    args = [a]
    if norm:
        in_specs.append(pl.BlockSpec((1, K), lambda i, j: (0, 0)))
        args.append(g.reshape(1, K))
    in_specs.append(pl.BlockSpec((K, tn), lambda i, j: (0, j)))
    args.append(w.astype(BF16))
    if rope:
        in_specs += [
            pl.BlockSpec((K, tn), lambda i, j: (0, j)),
            pl.BlockSpec((tm, tn), lambda i, j: (i % per_b, j)),
            pl.BlockSpec((tm, tn), lambda i, j: (i % per_b, j)),
        ]
        args += [w_rot.astype(BF16), cos, sin]
    return pl.pallas_call(
        functools.partial(_mm_kernel, norm=norm, rope=rope),
        grid=(M // tm, N // tn),
        in_specs=in_specs,
        out_specs=pl.BlockSpec((tm, tn), lambda i, j: (i, j)),
        out_shape=jax.ShapeDtypeStruct((M, N), out_dtype),
        compiler_params=_params("parallel", "parallel"),
        name=name,
    )(*args)


def _ffn_up_kernel(u_ref, wg_ref, wu_ref, o_ref):
    u = u_ref[...]
    a = _dot(u, wg_ref[...])
    b = _dot(u, wu_ref[...])
    o_ref[...] = (a * jax.nn.sigmoid(a) * b).astype(o_ref.dtype)


def _ffn_up(u, wg, wu, tm=1024, tn=512):
    M, K = u.shape
    N = wg.shape[1]
    tm, tn = _tile(M, tm), _tile(N, tn)
    return pl.pallas_call(
        _ffn_up_kernel,
        grid=(M // tm, N // tn),
        in_specs=[
            pl.BlockSpec((tm, K), lambda i, j: (i, 0)),
            pl.BlockSpec((K, tn), lambda i, j: (0, j)),
            pl.BlockSpec((K, tn), lambda i, j: (0, j)),
        ],
        out_specs=pl.BlockSpec((tm, tn), lambda i, j: (i, j)),
        out_shape=jax.ShapeDtypeStruct((M, N), BF16),
        compiler_params=_params("parallel", "parallel"),
        name="ffn_up",
    )(u, wg.astype(BF16), wu.astype(BF16))


def _down_post_kernel(h_ref, w_ref, x_ref, g_ref, gate_ref, o_ref, acc_ref, *, res_w, nk):
    k = pl.program_id(1)

    @pl.when(k == 0)
    def _():
        acc_ref[...] = jnp.zeros_like(acc_ref)

    acc_ref[...] += _dot(h_ref[...], w_ref[...])

    @pl.when(k == nk - 1)
    def _():
        y = _rms(acc_ref[...]) * g_ref[...]
        o_ref[...] = x_ref[...] + res_w * gate_ref[0] * y


def _down_post(h, wd, x2, g, gate, res_w, S, tm=512):
    M, K = h.shape
    D = wd.shape[1]
    B = M // S
    tk = max(t for t in range(128, 1409, 128) if K % t == 0)
    tm = _tile(S, tm)
    per_b = S // tm
    nk = K // tk
    return pl.pallas_call(
        functools.partial(_down_post_kernel, res_w=res_w, nk=nk),
        grid=(M // tm, nk),
        in_specs=[
            pl.BlockSpec((tm, tk), lambda i, k: (i, k)),
            pl.BlockSpec((tk, D), lambda i, k: (k, 0)),
            pl.BlockSpec((tm, D), lambda i, k: (i, 0)),
            pl.BlockSpec((1, D), lambda i, k: (0, 0)),
            pl.BlockSpec((1, 1, D), lambda i, k: (i // per_b, 0, 0)),
        ],
        out_specs=pl.BlockSpec((tm, D), lambda i, k: (i, 0)),
        out_shape=jax.ShapeDtypeStruct((M, D), F32),
        scratch_shapes=[pltpu.VMEM((tm, D), F32)],
        compiler_params=_params("parallel", "arbitrary"),
        name="down_post",
    )(h, wd.astype(BF16), x2, g.reshape(1, D), gate.reshape(B, 1, D))


def _merge_kernel(u_ref, wg_ref, y0_ref, y1_ref, y2_ref, y3_ref, wb_ref, o_ref):
    u = u_ref[...]
    acc = None
    for i, y_ref in enumerate((y0_ref, y1_ref, y2_ref, y3_ref)):
        gate = jax.nn.sigmoid(_dot(u, wg_ref[i]))
        term = gate * _dot(y_ref[...].astype(BF16), wb_ref[i])
        acc = term if acc is None else acc + term
    o_ref[...] = acc.astype(o_ref.dtype)


def _merge(u, w_gate, ys, w_branch, tm=512, tn=256):
    M, D = u.shape
    W = w_branch.shape[1]
    tm, tn = _tile(M, tm), _tile(D, tn)
    y_spec = pl.BlockSpec((tm, W), lambda i, j: (i, 0))
    return pl.pallas_call(
        _merge_kernel,
        grid=(M // tm, D // tn),
        in_specs=[
            pl.BlockSpec((tm, D), lambda i, j: (i, 0)),
            pl.BlockSpec((4, D, tn), lambda i, j: (0, 0, j)),
            y_spec, y_spec, y_spec, y_spec,
            pl.BlockSpec((4, W, tn), lambda i, j: (0, 0, j)),
        ],
        out_specs=pl.BlockSpec((tm, tn), lambda i, j: (i, j)),
        out_shape=jax.ShapeDtypeStruct((M, D), BF16),
        compiler_params=_params("parallel", "parallel"),
        name="merge",
    )(u, w_gate.astype(BF16), *ys, w_branch.astype(BF16))


def _flash_kernel(*refs, mode, t, G):
    if mode == "select":
        q_ref, k_ref, v_ref, sel_ref, o_ref, m_ref, l_ref, acc_ref = refs
    else:
        q_ref, k_ref, v_ref, o_ref, m_ref, l_ref, acc_ref = refs
    i = pl.program_id(2)
    rows = G * t
    dv = acc_ref.shape[-1]
    q = q_ref[0, 0].reshape(rows, q_ref.shape[-1])
    q_off = lax.broadcasted_iota(jnp.int32, (rows, t), 0) & (t - 1)
    k_off = lax.broadcasted_iota(jnp.int32, (rows, t), 1)

    def scores(j, mask):
        start = pl.multiple_of(j * t, t)
        s = _dot_nt(q, k_ref[0, 0, pl.ds(start, t), :])
        if mode == "select":
            n_blk = sel_ref.shape[-1]
            blk = lax.broadcasted_iota(jnp.int32, (n_blk, t), 0)
            key_blk = jnp.right_shift(j * t + lax.broadcasted_iota(jnp.int32, (n_blk, t), 1), SLC_SHIFT)
            expand = jnp.where(blk == key_blk, 1.0, 0.0).astype(BF16)
            picked = _dot(sel_ref[0, 0], expand)
            picked = jnp.concatenate([picked] * G, axis=0) if G > 1 else picked
            s = jnp.where(picked > 0.5, s, NEG_INF)
        if mask is not None:
            s = jnp.where(mask, s, NEG_INF)
        return s, v_ref[0, 0, pl.ds(start, t), :]

    s, v = scores(i, k_off <= q_off)
    m = jnp.max(s, axis=-1, keepdims=True)
    p = jnp.exp(s - m)
    m_ref[...] = m
    l_ref[...] = jnp.sum(p, axis=-1, keepdims=True)
    acc_ref[...] = _dot(p.astype(BF16), v)

    def accumulate(j, mask=None):
        s, v = scores(j, mask)
        m_prev = m_ref[...]
        m_new = jnp.maximum(m_prev, jnp.max(s, axis=-1, keepdims=True))
        p = jnp.exp(s - m_new)
        alpha = jnp.exp(m_prev - m_new)
        l_ref[...] = alpha * l_ref[...] + jnp.sum(p, axis=-1, keepdims=True)
        acc_ref[...] = alpha * acc_ref[...] + _dot(p.astype(BF16), v)
        m_ref[...] = m_new

    if mode == "window":
        n_back = WIN_SIZE // t
        for back in range(1, n_back + 1):
            mask = (k_off > q_off) if back == n_back else None

            @pl.when(i >= back)
            def _(back=back, mask=mask):
                accumulate(i - back, mask)
    else:
        def body(j, carry):
            accumulate(j)
            return carry

        lax.fori_loop(0, i, body, 0)

    o_ref[0, 0] = (acc_ref[...] / l_ref[...]).reshape(G, t, dv)


def _flash(q, k, v, *, mode, sel=None, tile=256):
    B, Hk, G, S, Dq = q.shape
    Dv = v.shape[-1]
    t = _tile(S, tile)
    assert t & (t - 1) == 0 and t % SLC_BLOCK == 0
    if mode == "window":
        assert WIN_SIZE % t == 0
    in_specs = [
        pl.BlockSpec((1, 1, G, t, Dq), lambda b, h, i: (b, h, 0, i, 0)),
        pl.BlockSpec((1, 1, S, Dq), lambda b, h, i: (b, h, 0, 0)),
        pl.BlockSpec((1, 1, S, Dv), lambda b, h, i: (b, h, 0, 0)),
    ]
    args = [q, k, v]
    if mode == "select":
        in_specs.append(pl.BlockSpec((1, 1, t, sel.shape[-1]), lambda b, h, i: (b, h, i, 0)))
        args.append(sel)
    return pl.pallas_call(
        functools.partial(_flash_kernel, mode=mode, t=t, G=G),
        grid=(B, Hk, S // t),
        in_specs=in_specs,
        out_specs=pl.BlockSpec((1, 1, G, t, Dv), lambda b, h, i: (b, h, 0, i, 0)),
        out_shape=jax.ShapeDtypeStruct((B, Hk, G, S, Dv), F32),
        scratch_shapes=[pltpu.VMEM((G * t, 1), F32), pltpu.VMEM((G * t, 1), F32), pltpu.VMEM((G * t, Dv), F32)],
        compiler_params=_params("parallel", "parallel", "arbitrary"),
        name="attn_" + mode,
    )(*args)


def _nsa_compress_kernel(t_ref, pos_ref, w1_ref, w2_ref, o_ref):
    n_pieces, half = t_ref.shape[-2], t_ref.shape[-1]
    for which in range(2):
        t = t_ref[which, 0, 0]
        w1 = w1_ref[which]
        first = _dot(t, w1[:half], HI)
        second = _dot(t, w1[half:], HI)
        pos_bias = _dot(pos_ref[which], w1, HI)
        pre = first + pltpu.roll(second, n_pieces - 1, 0) + pos_bias
        o_ref[which, 0, 0] = _dot(jax.nn.gelu(pre), w2_ref[which], HI)


def _nsa_compress(kv_flat, pos, w1, w2):
    _, B, Hk, n_pieces, half = kv_flat.shape
    hid, d = w2.shape[1], w2.shape[2]
    return pl.pallas_call(
        _nsa_compress_kernel,
        grid=(B, Hk),
        in_specs=[
            pl.BlockSpec((2, 1, 1, n_pieces, half), lambda b, h: (0, b, h, 0, 0)),
            pl.BlockSpec((2, 1, 2 * half), lambda b, h: (0, 0, 0)),
            pl.BlockSpec((2, 2 * half, hid), lambda b, h: (0, 0, 0)),
            pl.BlockSpec((2, hid, d), lambda b, h: (0, 0, 0)),
        ],
        out_specs=pl.BlockSpec((2, 1, 1, n_pieces, d), lambda b, h: (0, b, h, 0, 0)),
        out_shape=jax.ShapeDtypeStruct((2, B, Hk, n_pieces, d), F32),
        compiler_params=_params("parallel", "parallel"),
        name="nsa_compress",
    )(kv_flat, pos, w1, w2)


def _nsa_cmp_kernel(q_ref, kc_ref, vc_ref, ov_ref, o_ref, sel_ref, *, scale, tq, G):
    i = pl.program_id(2)
    rows = G * tq
    d = q_ref.shape[-1]
    n_cmp = kc_ref.shape[-2]
    n_sel = ov_ref.shape[-1]
    q = q_ref[0, 0].reshape(rows, d)
    s = _dot_nt(q, kc_ref[0, 0], HI) * scale
    t_pos = i * tq + (lax.broadcasted_iota(jnp.int32, (rows, n_cmp), 0) & (tq - 1))
    blk_end = lax.broadcasted_iota(jnp.int32, (rows, n_cmp), 1) * CMP_STRIDE + (CMP_BLOCK - 1)
    mask = blk_end <= t_pos
    s = jnp.where(mask, s, NEG_INF)
    e = jnp.exp(s - jnp.max(s, axis=-1, keepdims=True))
    p = jnp.where(mask, e / jnp.sum(e, axis=-1, keepdims=True), 0.0)
    o_ref[0, 0] = _dot(p, vc_ref[0, 0], HI).reshape(G, tq, d)

    p_group = p[0:tq]
    for g in range(1, G):
        p_group = p_group + p[g * tq:(g + 1) * tq]
    imp = _dot(p_group, ov_ref[...], HI)
    blk = lax.broadcasted_iota(jnp.int32, (tq, n_sel), 1)
    cur = jnp.right_shift(i * tq + lax.broadcasted_iota(jnp.int32, (tq, n_sel), 0), SLC_SHIFT)
    imp = jnp.where(blk <= cur, jnp.where(blk == 0, FORCE_SCORE, jnp.where(blk >= cur - 1, FORCE_SCORE, imp)),
                    NEG_INF)
    rank = jnp.zeros((tq, n_sel), F32)
    for c in range(n_sel):
        col = imp[:, c:c + 1]
        later = jnp.where(blk > c, 1.0, 0.0)
        rank = rank + jnp.where(col > imp, 1.0, 0.0) + jnp.where(col == imp, later, 0.0)
    sel_ref[0, 0] = jnp.where(rank < float(min(SLC_TOPK, n_sel)), 1.0, 0.0).astype(sel_ref.dtype)


def _nsa_cmp(q, kc, vc, overlap, tile=256):
    B, Hk, G, S, d = q.shape
    n_cmp = kc.shape[2]
    n_sel = overlap.shape[1]
    t = _tile(S, tile)
    return pl.pallas_call(
        functools.partial(_nsa_cmp_kernel, scale=d ** -0.5, tq=t, G=G),
        grid=(B, Hk, S // t),
        in_specs=[
            pl.BlockSpec((1, 1, G, t, d), lambda b, h, i: (b, h, 0, i, 0)),
            pl.BlockSpec((1, 1, n_cmp, d), lambda b, h, i: (b, h, 0, 0)),
            pl.BlockSpec((1, 1, n_cmp, d), lambda b, h, i: (b, h, 0, 0)),
            pl.BlockSpec((n_cmp, n_sel), lambda b, h, i: (0, 0)),
        ],
        out_specs=[
            pl.BlockSpec((1, 1, G, t, d), lambda b, h, i: (b, h, 0, i, 0)),
            pl.BlockSpec((1, 1, t, n_sel), lambda b, h, i: (b, h, i, 0)),
        ],
        out_shape=[
            jax.ShapeDtypeStruct((B, Hk, G, S, d), F32),
            jax.ShapeDtypeStruct((B, Hk, S, n_sel), BF16),
        ],
        compiler_params=_params("parallel", "parallel", "parallel"),
        name="nsa_cmp",
    )(q, kc, vc, overlap)


def _nsa_combine_kernel(gl_ref, ex_ref, oc_ref, os_ref, ow_ref, o_ref):
    width = oc_ref.shape[-1]
    gates = _dot(jax.nn.sigmoid(gl_ref[...]), ex_ref[...], HI)
    o_ref[...] = (gates[:, :width] * oc_ref[...] + gates[:, width:2 * width] * os_ref[...]
                  + gates[:, 2 * width:] * ow_ref[...])


def _nsa_combine(gl, expand, o_cmp, o_slc, o_win, tm=512):
    M, width = o_cmp.shape
    tm = _tile(M, tm)
    o_spec = pl.BlockSpec((tm, width), lambda i: (i, 0))
    return pl.pallas_call(
        _nsa_combine_kernel,
        grid=(M // tm,),
        in_specs=[
            pl.BlockSpec((tm, gl.shape[1]), lambda i: (i, 0)),
            pl.BlockSpec(expand.shape, lambda i: (0, 0)),
            o_spec, o_spec, o_spec,
        ],
        out_specs=o_spec,
        out_shape=jax.ShapeDtypeStruct((M, width), F32),
        compiler_params=_params("parallel"),
        name="nsa_combine",
    )(gl, expand, o_cmp, o_slc, o_win)


def _rwkv_prep_kernel(p_ref, mu_ref, w0_ref, ww2_ref, a0_ref, aw2_ref, gw2_ref, kk_ref, ka_ref, seg_ref,
                      r_o, lw_o, k_o, v_o, kk_o, b_o, g_o, last_ref):
    tb = p_ref.shape[1]
    width = r_o.shape[-1]

    @pl.when(pl.program_id(1) == 0)
    def _():
        last_ref[...] = jnp.zeros_like(last_ref)

    p = p_ref[0]
    row = lax.broadcasted_iota(jnp.int32, p.shape, 0)
    prev = jnp.where(row == 0, last_ref[...], pltpu.roll(p, 1, 0))
    last_ref[...] = p[tb - 1:tb]
    xs = p + (prev - p) * mu_ref[...]
    r = xs[:, 0:width]
    k = xs[:, width:2 * width]
    v = xs[:, 2 * width:3 * width]
    lora = xs[:, 3 * width:3 * width + 128]
    g_lo = xs[:, 3 * width + 128:]
    lw = -math.exp(-0.5) * jax.nn.sigmoid(w0_ref[...] + _dot(jnp.tanh(lora), ww2_ref[...], HI))
    a = jax.nn.sigmoid(a0_ref[...] + _dot(lora, aw2_ref[...], HI))
    g = _dot(jax.nn.sigmoid(g_lo), gw2_ref[...], HI)
    kk = k * kk_ref[...]
    norm = jnp.sqrt(_dot(kk * kk, seg_ref[...], HI))
    kk = kk / jnp.maximum(norm, 1e-12)
    r_o[0] = r
    lw_o[0] = lw
    k_o[0] = k * (1.0 + (a - 1.0) * ka_ref[...])
    v_o[0] = v
    kk_o[0] = kk
    b_o[0] = kk * a
    g_o[0] = g


def _rwkv_prep(p, consts, tb=256):
    B, S, P = p.shape
    width = consts[1].shape[-1]
    tb = _tile(S, tb)
    const_specs = [pl.BlockSpec(c.shape, lambda b, t: (0, 0)) for c in consts]
    o_spec = pl.BlockSpec((1, tb, width), lambda b, t: (b, t, 0))
    return pl.pallas_call(
        _rwkv_prep_kernel,
        grid=(B, S // tb),
        in_specs=[pl.BlockSpec((1, tb, P), lambda b, t: (b, t, 0))] + const_specs,
        out_specs=[o_spec] * 7,
        out_shape=[jax.ShapeDtypeStruct((B, S, width), F32)] * 7,
        scratch_shapes=[pltpu.VMEM((1, P), F32)],
        compiler_params=_params("parallel", "arbitrary"),
        name="rwkv_prep",
    )(p, *consts)


def _cumsum_rows(x):
    n = x.shape[0]
    row = lax.broadcasted_iota(jnp.int32, x.shape, 0)
    step = 1
    while step < n:
        x = x + jnp.where(row >= step, pltpu.roll(x, step, 0), 0.0)
        step *= 2
    return x


def _seg_sum(x, seg):
    hi = x.astype(BF16)
    lo = (x - hi.astype(F32)).astype(BF16)
    return _dot(hi, seg) + _dot(lo, seg)


def _rwkv_pair_chunk(rg, kkg, bd, kd, kc, bc, v, gamma, state, m0, m1, strict, incl, eye):
    C = rg.shape[0]
    n2 = 2 * C
    stack = lambda x: jnp.concatenate([x * m0, x * m1], axis=0)
    kks, rs, vs = stack(kkg), stack(rg), stack(v)
    big = _dot_nt(jnp.concatenate([kks, rs], axis=0), jnp.concatenate([stack(bd), stack(kd)], axis=0))
    a_mat = jnp.where(strict, big[:n2, :n2], 0.0)
    bk = jnp.where(strict, big[:n2, n2:], 0.0)
    ay = jnp.where(incl, big[n2:, :n2], 0.0)
    by = jnp.where(incl, big[n2:, n2:], 0.0)
    power = -a_mat
    inv = eye + power
    for _ in range(int(math.log2(C)) - 1):
        power = _dot(power, power)
        inv = inv + _dot(inv, power)
    tx = _dot(inv, jnp.concatenate([kks, _dot(bk, vs)], axis=1))
    ayx = _dot(ay, tx)
    w = kks.shape[1]
    q_hat = rs - ayx[:, :w]
    y_intra = _dot(by, vs) - ayx[:, w:]
    qt = _dot_nt(jnp.concatenate([q_hat, tx[:, :w]], axis=0), state)
    ys = qt[:n2] + y_intra
    us = qt[n2:] + tx[:, w:]
    new_state = state * gamma + _dot_tn(jnp.concatenate([vs, us], axis=0),
                                        jnp.concatenate([stack(kc), -stack(bc)], axis=0))
    return ys[:C] + ys[C:], new_state


def _rwkv_scan_kernel(r_ref, lw_ref, k_ref, v_ref, kk_ref, b_ref, g_ref, lnw_ref, lnb_ref, rk_ref, seg_ref,
                      o_ref, state_ref, *, heads, chunk):
    tb, width = r_ref.shape[1], r_ref.shape[2]
    n = width // heads
    pairs, pw = heads // 2, 2 * n

    @pl.when(pl.program_id(1) == 0)
    def _():
        state_ref[...] = jnp.zeros_like(state_ref)

    lane = lax.broadcasted_iota(jnp.int32, (1, pw), 1)
    m0 = jnp.where(lane < n, 1.0, 0.0)
    m1 = 1.0 - m0
    row = lax.broadcasted_iota(jnp.int32, (2 * chunk, 2 * chunk), 0) & (chunk - 1)
    col = lax.broadcasted_iota(jnp.int32, (2 * chunk, 2 * chunk), 1) & (chunk - 1)
    strict, incl = col < row, col <= row
    eye = jnp.where(lax.broadcasted_iota(jnp.int32, (2 * chunk, 2 * chunk), 0)
                    == lax.broadcasted_iota(jnp.int32, (2 * chunk, 2 * chunk), 1), 1.0, 0.0)
    seg = seg_ref[...]
    states = [state_ref[p] for p in range(pairs)]
    for c in range(tb // chunk):
        rows = pl.ds(c * chunk, chunk)
        r, lw, k, v = r_ref[0, rows, :], lw_ref[0, rows, :], k_ref[0, rows, :], v_ref[0, rows, :]
        kk, bb = kk_ref[0, rows, :], b_ref[0, rows, :]
        g = _cumsum_rows(lw)
        g_last = g[chunk - 1:chunk]
        decay_out = jnp.exp(-g)
        decay_end = jnp.exp(g_last - g)
        rg, kkg = r * jnp.exp(g), kk * jnp.exp(g - lw)
        bd, kd = bb * decay_out, k * decay_out
        kc, bc = k * decay_end, bb * decay_end
        gamma = jnp.exp(g_last)
        ys = []
        for p in range(pairs):
            sl = slice(p * pw, (p + 1) * pw)
            y, states[p] = _rwkv_pair_chunk(rg[:, sl], kkg[:, sl], bd[:, sl], kd[:, sl], kc[:, sl], bc[:, sl],
                                            v[:, sl], gamma[:, sl], states[p], m0, m1, strict, incl, eye)
            ys.append(y)
        y = jnp.concatenate(ys, axis=1)
        mean = _seg_sum(y, seg) * (1.0 / n)
        dev = y - mean
        var = _seg_sum(dev * dev, seg) * (1.0 / n)
        yn = dev * lax.rsqrt(var + RWKV_GN_EPS) * lnw_ref[...] + lnb_ref[...]
        bonus = _seg_sum(r * k * rk_ref[...], seg) * v
        o_ref[0, rows, :] = (yn + bonus) * g_ref[0, rows, :]
    for p in range(pairs):
        state_ref[p] = states[p]


def _rwkv_scan(r, lw, k, v, kk, bb, g, ln_w, ln_b, r_k, seg, tb=256):
    B, S, width = r.shape
    heads = RWKV_HEADS
    n = width // heads
    chunk = min(RWKV_CHUNK, S)
    assert chunk == n and heads % 2 == 0
    tb = _tile(S, tb)
    x_spec = pl.BlockSpec((1, tb, width), lambda b, t: (b, t, 0))
    vec_spec = pl.BlockSpec((1, width), lambda b, t: (0, 0))
    return pl.pallas_call(
        functools.partial(_rwkv_scan_kernel, heads=heads, chunk=chunk),
        grid=(B, S // tb),
        in_specs=[x_spec] * 7 + [vec_spec] * 3 + [pl.BlockSpec((width, width), lambda b, t: (0, 0))],
        out_specs=x_spec,
        out_shape=jax.ShapeDtypeStruct((B, S, width), F32),
        scratch_shapes=[pltpu.VMEM((heads // 2, 2 * n, 2 * n), F32)],
        compiler_params=_params("parallel", "arbitrary"),
        name="rwkv_scan",
    )(r, lw, k, v, kk, bb, g, ln_w, ln_b, r_k, seg.astype(BF16))


def _gla_kernel(q_ref, k_ref, v_ref, r_ref, al_ref, aw_ref, ab_ref, ng_ref, o_ref, state_ref,
                *, heads, chunk):
    tb = v_ref.shape[1]
    dk = q_ref.shape[-1]
    dv = v_ref.shape[-1] // heads

    @pl.when(pl.program_id(1) == 0)
    def _():
        state_ref[...] = jnp.zeros_like(state_ref)

    row = lax.broadcasted_iota(jnp.int32, (chunk, chunk), 0)
    col = lax.broadcasted_iota(jnp.int32, (chunk, chunk), 1)
    incl = col <= row
    states = [state_ref[h] for h in range(heads)]
    for c in range(tb // chunk):
        rows = pl.ds(c * chunk, chunk)
        alpha_lo = al_ref[0, rows, :]
        outs = []
        for h in range(heads):
            cols = pl.ds(h * dv, dv)
            x = _dot(alpha_lo, aw_ref[h], HI) + ab_ref[h]
            log_a = (jnp.minimum(x, 0.0) - jnp.log(1.0 + jnp.exp(-jnp.abs(x)))) * (1.0 / GLA_TAU)
            b = _cumsum_rows(log_a)
            b_last = b[chunk - 1:chunk]
            q = q_ref[0, h, rows, :]
            k = k_ref[0, h, rows, :]
            v = v_ref[0, rows, cols]
            qe = q * (dk ** -0.5) * jnp.exp(b)
            attn = jnp.where(incl, _dot_nt(qe, k * jnp.exp(-b)), 0.0)
            o = _dot(attn, v) + _dot_nt(qe, states[h])
            states[h] = states[h] * jnp.exp(b_last) + _dot_tn(v, k * jnp.exp(b_last - b))
            outs.append(o * lax.rsqrt(jnp.mean(o * o, axis=-1, keepdims=True) + NORM_EPS))
        gate = r_ref[0, rows, :]
        o_ref[0, rows, :] = jnp.concatenate(outs, axis=1) * ng_ref[...] * (gate * jax.nn.sigmoid(gate))
    for h in range(heads):
        state_ref[h] = states[h]


def _gla(q, k, v, r, alpha_lo, aw, ab, ng, tb=256):
    B, H, S, dk = q.shape
    width = v.shape[-1]
    lora = alpha_lo.shape[-1]
    chunk = min(GLA_CHUNK, S)
    tb = _tile(S, tb)
    qk_spec = pl.BlockSpec((1, H, tb, dk), lambda b, t: (b, 0, t, 0))
    x_spec = pl.BlockSpec((1, tb, width), lambda b, t: (b, t, 0))
    return pl.pallas_call(
        functools.partial(_gla_kernel, heads=H, chunk=chunk),
        grid=(B, S // tb),
        in_specs=[
            qk_spec, qk_spec, x_spec, x_spec,
            pl.BlockSpec((1, tb, lora), lambda b, t: (b, t, 0)),
            pl.BlockSpec(aw.shape, lambda b, t: (0, 0, 0)),
            pl.BlockSpec(ab.shape, lambda b, t: (0, 0, 0)),
            pl.BlockSpec(ng.shape, lambda b, t: (0, 0)),
        ],
        out_specs=x_spec,
        out_shape=jax.ShapeDtypeStruct((B, S, width), F32),
        scratch_shapes=[pltpu.VMEM((H, width // H, dk), F32)],
        compiler_params=_params("parallel", "arbitrary"),
        name="gla",
    )(q, k, v, r, alpha_lo, aw, ab, ng)


def _rope_tables(S, d):
    inv = ROPE_THETA ** (-jnp.arange(0, d, 2, dtype=F32) / d)
    ang = jnp.arange(S).astype(F32)[:, None] * inv[None, :]
    cos, sin = jnp.cos(ang), jnp.sin(ang)
    return jnp.concatenate([cos, cos], axis=-1), jnp.concatenate([sin, sin], axis=-1)


def _rot_cols(w, d):
    K = w.shape[0]
    w = w.reshape(K, -1, 2, d // 2)
    return jnp.concatenate([-w[:, :, 1], w[:, :, 0]], axis=-1).reshape(K, -1)


def _nsa_constants(S):
    n_pieces = S // CMP_STRIDE
    n_cmp = (S - CMP_BLOCK) // CMP_STRIDE + 1
    n_sel = S // SLC_BLOCK
    start = np.arange(n_cmp)[:, None] * CMP_STRIDE
    end = start + CMP_BLOCK - 1
    blk = np.arange(n_sel)[None, :]
    overlap = np.zeros((n_pieces, n_sel), np.float32)
    overlap[:n_cmp] = ((start < (blk + 1) * SLC_BLOCK) & (end >= blk * SLC_BLOCK)).astype(np.float32)
    width = NSA_HEADS * HEAD_DIM
    expand = np.zeros((128, 3 * width), np.float32)
    for h in range(NSA_HEADS):
        for i in range(3):
            expand[3 * h + i, i * width + h * HEAD_DIM:i * width + (h + 1) * HEAD_DIM] = 1.0
    return jnp.asarray(overlap), jnp.asarray(expand)


def _token_mixing(u, B, S, w_in, w_branch, cmp_pos, cmp_w1, cmp_w2, rwkv_mu, rwkv_w0, rwkv_w_w2, rwkv_a0,
                  rwkv_a_w2, rwkv_g_w2, rwkv_k_k, rwkv_k_a, rwkv_r_k, rwkv_ln_w, rwkv_ln_b, gla_alpha_w2,
                  gla_alpha_b, gla_norm_g, mla_q_norm, mla_w_uq, mla_kv_norm, mla_w_ukv):
    M, D = u.shape
    d = HEAD_DIM
    width = D // 4
    sizes = [width + 6 * NSA_KV_HEADS * d + 3 * NSA_HEADS, 3 * width + 256, 2 * (width // 2) + 2 * width + 16,
             384 + 128 + MLA_ROPE, 4 * D]
    offs = np.cumsum([0] + sizes)
    assert offs[-1] == w_in.shape[1]
    w_nsa, w_rwkv, w_gla, w_mla, w_gate = [w_in[:, offs[i]:offs[i + 1]] for i in range(5)]
    zeros = lambda n: jnp.zeros((D, n), F32)
    kvw = NSA_KV_HEADS * d

    wq = w_nsa[:, :width]
    wkc, wvc, wks, wvs, wkw, wvw = [w_nsa[:, width + i * kvw:width + (i + 1) * kvw] for i in range(6)]
    wgl = w_nsa[:, width + 6 * kvw:]
    wkr = w_mla[:, 512:512 + MLA_ROPE]
    n_rope = width + 3 * kvw + MLA_ROPE
    pad = (-n_rope) % 512
    w_a = jnp.concatenate([wq, wkc, wks, wkw, wkr, zeros(pad)], axis=1)
    w_b = jnp.concatenate([_rot_cols(wq, d), _rot_cols(wkc, d), _rot_cols(wks, d), _rot_cols(wkw, d),
                           _rot_cols(wkr, MLA_ROPE), zeros(pad)], axis=1)
    cos_d, sin_d = _rope_tables(S, d)
    cos_r, sin_r = _rope_tables(S, MLA_ROPE)
    reps = (width + 3 * kvw) // d
    cos_t = jnp.concatenate([jnp.tile(cos_d, (1, reps)), cos_r, jnp.ones((S, pad), F32)], axis=1)
    sin_t = jnp.concatenate([jnp.tile(sin_d, (1, reps)), sin_r, jnp.zeros((S, pad), F32)], axis=1)
    roped = _matmul(u, w_a, S=S, w_rot=w_b, cos=cos_t, sin=sin_t, name="proj_rope").reshape(B, S, -1)
    n_plain = 3 * kvw + 3 * NSA_HEADS
    plain = _matmul(u, jnp.concatenate([wvc, wvs, wvw, wgl, zeros((-n_plain) % 128)], axis=1), S=S,
                    name="proj_nsa_v")

    q = roped[..., :width].reshape(B, S, NSA_KV_HEADS, NSA_GROUP, d).transpose(0, 2, 3, 1, 4)
    kv_heads = lambda t: t.reshape(B, S, NSA_KV_HEADS, d).transpose(0, 2, 1, 3)
    kc_in = kv_heads(roped[..., width:width + kvw])
    ks = kv_heads(roped[..., width + kvw:width + 2 * kvw])
    kw = kv_heads(roped[..., width + 2 * kvw:width + 3 * kvw])
    k_pe = roped[..., width + 3 * kvw:width + 3 * kvw + MLA_ROPE]
    plain3 = plain.reshape(B, S, -1)
    vc_in, vs, vw = [kv_heads(plain3[..., i * kvw:(i + 1) * kvw]) for i in range(3)]
    gate_logits = plain[:, 3 * kvw:3 * kvw + 128]

    overlap, expand = _nsa_constants(S)
    n_pieces = S // CMP_STRIDE
    kv_flat = jnp.stack([kc_in, vc_in]).reshape(2, B, NSA_KV_HEADS, n_pieces, CMP_STRIDE * d)
    cmp = _nsa_compress(kv_flat, cmp_pos.reshape(2, 1, CMP_BLOCK * d), cmp_w1, cmp_w2)
    o_cmp, sel = _nsa_cmp(q, cmp[0], cmp[1], overlap)
    q_att = (q * d ** -0.5).astype(BF16)
    o_slc = _flash(q_att, ks.astype(BF16), vs.astype(BF16), mode="select", sel=sel)
    o_win = _flash(q_att, kw.astype(BF16), vw.astype(BF16), mode="window")
    to_tokens = lambda o: o.transpose(0, 3, 1, 2, 4).reshape(M, width)
    y_nsa = _nsa_combine(gate_logits, expand, to_tokens(o_cmp), to_tokens(o_slc), to_tokens(o_win))

    regroup = lambda t: jnp.concatenate(
        [t[..., 0:width], t[..., width + 64:2 * width + 64], t[..., 2 * width + 64:3 * width + 64],
         t[..., width:width + 64], t[..., 3 * width + 64:3 * width + 128], t[..., 3 * width + 128:]], axis=-1)
    p_rwkv = _matmul(u, regroup(w_rwkv), S=S, tn=256, name="proj_rwkv").reshape(B, S, -1)
    row = lambda t: t.reshape(1, -1)
    seg = jnp.asarray(np.kron(np.eye(RWKV_HEADS), np.ones((d, d))).astype(np.float32))
    lora_pad = jnp.zeros((64, width), F32)
    consts = [row(regroup(rwkv_mu)), row(rwkv_w0), jnp.concatenate([rwkv_w_w2, lora_pad], axis=0), row(rwkv_a0),
              jnp.concatenate([lora_pad, rwkv_a_w2], axis=0), rwkv_g_w2, row(rwkv_k_k), row(rwkv_k_a), seg]
    r, lw, k2, v, kk, bb, g = _rwkv_prep(p_rwkv, consts)
    y_rwkv = _rwkv_scan(r, lw, k2, v, kk, bb, g, row(rwkv_ln_w), row(rwkv_ln_b), row(rwkv_r_k), seg)
    y_rwkv = y_rwkv.reshape(M, width)

    dk_all = width // 2
    w_gla_p = jnp.concatenate([w_gla[:, :2 * dk_all + width], w_gla[:, 2 * dk_all + width + 16:],
                               w_gla[:, 2 * dk_all + width:2 * dk_all + width + 16], zeros(240)], axis=1)
    p_gla = _matmul(u, w_gla_p, S=S, tn=256, name="proj_gla").reshape(B, S, -1)
    dkh = dk_all // GLA_HEADS
    gla_heads = lambda t: t.reshape(B, S, GLA_HEADS, dkh).transpose(0, 2, 1, 3)
    y_gla = _gla(gla_heads(p_gla[..., :dk_all]), gla_heads(p_gla[..., dk_all:2 * dk_all]),
                 p_gla[..., 2 * dk_all:2 * dk_all + width], p_gla[..., 2 * dk_all + width:2 * dk_all + 2 * width],
                 p_gla[..., 2 * dk_all + 2 * width:2 * dk_all + 2 * width + 16],
                 gla_alpha_w2.reshape(16, GLA_HEADS, dkh).transpose(1, 0, 2), gla_alpha_b.reshape(GLA_HEADS, 1, dkh),
                 row(gla_norm_g)).reshape(M, width)

    c_q = _matmul(u, w_mla[:, :384], S=S, name="proj_mla_q")
    c_kv = _matmul(u, w_mla[:, 384:512], S=S, name="proj_mla_kv")
    H = MLA_HEADS
    wq3 = mla_w_uq.reshape(384, H, MLA_NOPE + MLA_ROPE)
    rp = wq3[:, :, MLA_NOPE:]
    half = MLA_ROPE // 2
    z = lambda n: jnp.zeros((384, H, n), F32)
    wq_a = jnp.concatenate([wq3, z(128 - MLA_NOPE - MLA_ROPE)], axis=-1).reshape(384, H * 128)
    wq_b = jnp.concatenate([z(MLA_NOPE), -rp[..., half:], rp[..., :half], z(128 - MLA_NOPE - MLA_ROPE)],
                           axis=-1).reshape(384, H * 128)
    tail = 128 - MLA_NOPE - MLA_ROPE
    cos_h = jnp.concatenate([jnp.ones((S, MLA_NOPE), F32), cos_r, jnp.ones((S, tail), F32)], axis=1)
    sin_h = jnp.concatenate([jnp.zeros((S, MLA_NOPE), F32), sin_r, jnp.zeros((S, tail), F32)], axis=1)
    q_m = _matmul(c_q, wq_a, S=S, g=mla_q_norm, w_rot=wq_b, cos=jnp.tile(cos_h, (1, H)), sin=jnp.tile(sin_h, (1, H)),
                  name="mla_q")
    q_m = (q_m * (MLA_NOPE + MLA_ROPE) ** -0.5).astype(BF16).reshape(B, S, H, 128).transpose(0, 2, 1, 3)[:, :, None]
    kv_m = _matmul(c_kv, mla_w_ukv, S=S, g=mla_kv_norm, name="mla_kv").reshape(B, S, H, MLA_NOPE + width // H)
    k_m = jnp.concatenate([kv_m[..., :MLA_NOPE], jnp.broadcast_to(k_pe[:, :, None, :], (B, S, H, MLA_ROPE)),
                           jnp.zeros((B, S, H, tail), F32)], axis=-1).astype(BF16).transpose(0, 2, 1, 3)
    v_m = kv_m[..., MLA_NOPE:].astype(BF16).transpose(0, 2, 1, 3)
    o_mla = _flash(q_m, k_m, v_m, mode="causal", tile=512)
    y_mla = o_mla[:, :, 0].transpose(0, 2, 1, 3).reshape(M, width)

    w_gate4 = w_gate.reshape(D, 4, D).transpose(1, 0, 2)
    return _merge(u, w_gate4, (y_nsa, y_rwkv, y_gla, y_mla), w_branch)


def kernel(x, c, ada_w, ada_b, pre_g, post_g, ffn_wg, ffn_wu, ffn_wd, mix_w_in, mix_w_branch, mix_w_out, nsa_cmp_pos, nsa_cmp_w1, nsa_cmp_w2, rwkv_mu, rwkv_w0, rwkv_w_w2, rwkv_a0, rwkv_a_w2, rwkv_g_w2, rwkv_k_k, rwkv_k_a, rwkv_r_k, rwkv_ln_w, rwkv_ln_b, gla_alpha_w2, gla_alpha_b, gla_norm_g, mla_q_norm, mla_w_uq, mla_kv_norm, mla_w_ukv):
    B, S, D = x.shape
    M = B * S
    depth = ada_w.shape[0]
    mod = _modulation(c, ada_w, ada_b)
    x2 = x.reshape(M, D)
    for l in range(depth):
        for i in range(3):
            m = mod[3 * l + i]
            shift, scale, gate = m[:, :D], m[:, D:2 * D], m[:, 2 * D:]
            u = _prenorm(x2, pre_g[l, i], scale, shift, S)
            if i == 1:
                y = _token_mixing(
                    u, B, S, mix_w_in[l], mix_w_branch[l], nsa_cmp_pos[l], nsa_cmp_w1[l], nsa_cmp_w2[l], rwkv_mu[l],
                    rwkv_w0[l], rwkv_w_w2[l], rwkv_a0[l], rwkv_a_w2[l], rwkv_g_w2[l], rwkv_k_k[l], rwkv_k_a[l],
                    rwkv_r_k[l], rwkv_ln_w[l], rwkv_ln_b[l], gla_alpha_w2[l], gla_alpha_b[l], gla_norm_g[l],
                    mla_q_norm[l], mla_w_uq[l], mla_kv_norm[l], mla_w_ukv[l])
                x2 = _down_post(y, mix_w_out[l], x2, post_g[l, i], gate, 1.0, S)
            else:
                j = 0 if i == 0 else 1
                h = _ffn_up(u, ffn_wg[l, j], ffn_wu[l, j])
                x2 = _down_post(h, ffn_wd[l, j], x2, post_g[l, i], gate, 0.5, S)
    return x2.reshape(B, S, D)
```

```python
import functools
import math

import jax
import jax.numpy as jnp
import numpy as np
from jax import lax
from jax.experimental import pallas as pl
from jax.experimental.pallas import tpu as pltpu

F32 = jnp.float32
BF16 = jnp.bfloat16
HI = lax.Precision.HIGHEST

NORM_EPS = 1e-6
NEG_INF = -1e30
FORCE_SCORE = 1e30
ROPE_THETA = 10000.0
HEAD_DIM = 64

NSA_HEADS = 8
NSA_KV_HEADS = 2
NSA_GROUP = NSA_HEADS // NSA_KV_HEADS
CMP_BLOCK = 32
CMP_STRIDE = 16
SLC_BLOCK = 64
SLC_SHIFT = 6
SLC_TOPK = 16
WIN_SIZE = 512
RWKV_HEADS = 8
RWKV_GN_EPS = 64e-5
RWKV_CHUNK = 64
GLA_HEADS = 4
GLA_TAU = 16.0
GLA_CHUNK = 64
MLA_HEADS = 8
MLA_NOPE = 64
MLA_ROPE = 32

VMEM_LIMIT_BYTES = 56 * 1024 * 1024


def _params(*sem):
    return pltpu.CompilerParams(dimension_semantics=sem, vmem_limit_bytes=VMEM_LIMIT_BYTES)


def _dot(a, b, precision=None):
    return jnp.dot(a, b, preferred_element_type=F32, precision=precision)


def _dot_nt(a, b, precision=None):
    return lax.dot_general(a, b, (((1,), (1,)), ((), ())), preferred_element_type=F32, precision=precision)


def _dot_tn(a, b, precision=None):
    return lax.dot_general(a, b, (((0,), (0,)), ((), ())), preferred_element_type=F32, precision=precision)


def _rms(x):
    return x * lax.rsqrt(jnp.mean(x * x, axis=-1, keepdims=True) + NORM_EPS)


def _tile(n, t):
    t = min(n, t)
    assert n % t == 0, (n, t)
    return t


def _mod_kernel(c_ref, w_ref, b_ref, o_ref):
    c = c_ref[...]
    o_ref[0] = _dot(c * jax.nn.sigmoid(c), w_ref[0], HI) + b_ref[0]


def _modulation(c, ada_w, ada_b):
    B, D = c.shape
    n_sub = ada_w.shape[0] * ada_w.shape[1]
    w = ada_w.reshape(n_sub, D, 3 * D)
    b = ada_b.reshape(n_sub, 1, 3 * D)
    rows = 8
    cp = jnp.zeros((rows, D), F32).at[:B].set(c)
    tn = _tile(3 * D, 512)
    out = pl.pallas_call(
        _mod_kernel,
        grid=(n_sub, 3 * D // tn),
        in_specs=[
            pl.BlockSpec((rows, D), lambda s, j: (0, 0)),
            pl.BlockSpec((1, D, tn), lambda s, j: (s, 0, j)),
            pl.BlockSpec((1, 1, tn), lambda s, j: (s, 0, j)),
        ],
        out_specs=pl.BlockSpec((1, rows, tn), lambda s, j: (s, 0, j)),
        out_shape=jax.ShapeDtypeStruct((n_sub, rows, 3 * D), F32),
        compiler_params=_params("parallel", "parallel"),
        name="modulation",
    )(cp, w, b)
    return out[:, :B]


def _prenorm_kernel(x_ref, g_ref, sc_ref, sh_ref, o_ref):
    y = _rms(x_ref[...]) * g_ref[...]
    o_ref[...] = (y * (1.0 + sc_ref[0]) + sh_ref[0]).astype(o_ref.dtype)


def _prenorm(x2, g, scale, shift, S):
    M, D = x2.shape
    B = M // S
    tm = _tile(S, 512)
    per_b = S // tm
    return pl.pallas_call(
        _prenorm_kernel,
        grid=(M // tm,),
        in_specs=[
            pl.BlockSpec((tm, D), lambda i: (i, 0)),
            pl.BlockSpec((1, D), lambda i: (0, 0)),
            pl.BlockSpec((1, 1, D), lambda i: (i // per_b, 0, 0)),
            pl.BlockSpec((1, 1, D), lambda i: (i // per_b, 0, 0)),
        ],
        out_specs=pl.BlockSpec((tm, D), lambda i: (i, 0)),
        out_shape=jax.ShapeDtypeStruct((M, D), BF16),
        compiler_params=_params("parallel"),
        name="prenorm",
    )(x2, g.reshape(1, D), scale.reshape(B, 1, D), shift.reshape(B, 1, D))


def _mm_kernel(*refs, norm, rope):
    it = iter(refs)
    a_ref = next(it)
    g_ref = next(it) if norm else None
    w_ref = next(it)
    if rope:
        w2_ref, cos_ref, sin_ref = next(it), next(it), next(it)
    o_ref = next(it)
    a = a_ref[...]
    if norm:
        a = _rms(a.astype(F32)) * g_ref[...]
    a = a.astype(BF16)
    out = _dot(a, w_ref[...])
    if rope:
        out = out * cos_ref[...] + _dot(a, w2_ref[...]) * sin_ref[...]
    o_ref[...] = out.astype(o_ref.dtype)


def _matmul(a, w, *, S, g=None, w_rot=None, cos=None, sin=None, out_dtype=F32, tm=512, tn=512, name="matmul"):
    M, K = a.shape
    N = w.shape[1]
    tm = _tile(S, tm)
    tn = _tile(N, tn)
    per_b = S // tm
    norm, rope = g is not None, w_rot is not None
    in_specs = [pl.BlockSpec((tm, K), lambda i, j: (i, 0))]
    args = [a]
    if norm:
        in_specs.append(pl.BlockSpec((1, K), lambda i, j: (0, 0)))
        args.append(g.reshape(1, K))
    in_specs.append(pl.BlockSpec((K, tn), lambda i, j: (0, j)))
    args.append(w.astype(BF16))
    if rope:
        in_specs += [
            pl.BlockSpec((K, tn), lambda i, j: (0, j)),
            pl.BlockSpec((tm, tn), lambda i, j: (i % per_b, j)),
            pl.BlockSpec((tm, tn), lambda i, j: (i % per_b, j)),
        ]
        args += [w_rot.astype(BF16), cos, sin]
    return pl.pallas_call(
        functools.partial(_mm_kernel, norm=norm, rope=rope),
        grid=(M // tm, N // tn),
        in_specs=in_specs,
        out_specs=pl.BlockSpec((tm, tn), lambda i, j: (i, j)),
        out_shape=jax.ShapeDtypeStruct((M, N), out_dtype),
        compiler_params=_params("parallel", "parallel"),
        name=name,
    )(*args)


def _ffn_up_kernel(u_ref, wg_ref, wu_ref, o_ref):
    u = u_ref[...]
    a = _dot(u, wg_ref[...])
    b = _dot(u, wu_ref[...])
    o_ref[...] = (a * jax.nn.sigmoid(a) * b).astype(o_ref.dtype)


def _ffn_up(u, wg, wu, tm=1024, tn=512):
    M, K = u.shape
    N = wg.shape[1]
    tm, tn = _tile(M, tm), _tile(N, tn)
    return pl.pallas_call(
        _ffn_up_kernel,
        grid=(M // tm, N // tn),
        in_specs=[
            pl.BlockSpec((tm, K), lambda i, j: (i, 0)),
            pl.BlockSpec((K, tn), lambda i, j: (0, j)),
            pl.BlockSpec((K, tn), lambda i, j: (0, j)),
        ],
        out_specs=pl.BlockSpec((tm, tn), lambda i, j: (i, j)),
        out_shape=jax.ShapeDtypeStruct((M, N), BF16),
        compiler_params=_params("parallel", "parallel"),
        name="ffn_up",
    )(u, wg.astype(BF16), wu.astype(BF16))


def _down_post_kernel(h_ref, w_ref, x_ref, g_ref, gate_ref, o_ref, acc_ref, *, res_w, nk):
    k = pl.program_id(1)

    @pl.when(k == 0)
    def _():
        acc_ref[...] = jnp.zeros_like(acc_ref)

    acc_ref[...] += _dot(h_ref[...], w_ref[...])

    @pl.when(k == nk - 1)
    def _():
        y = _rms(acc_ref[...]) * g_ref[...]
        o_ref[...] = x_ref[...] + res_w * gate_ref[0] * y


def _down_post(h, wd, x2, g, gate, res_w, S, tm=512):
    M, K = h.shape
    D = wd.shape[1]
    B = M // S
    tk = max(t for t in range(128, 1409, 128) if K % t == 0)
    tm = _tile(S, tm)
    per_b = S // tm
    nk = K // tk
    return pl.pallas_call(
        functools.partial(_down_post_kernel, res_w=res_w, nk=nk),
        grid=(M // tm, nk),
        in_specs=[
            pl.BlockSpec((tm, tk), lambda i, k: (i, k)),
            pl.BlockSpec((tk, D), lambda i, k: (k, 0)),
            pl.BlockSpec((tm, D), lambda i, k: (i, 0)),
            pl.BlockSpec((1, D), lambda i, k: (0, 0)),
            pl.BlockSpec((1, 1, D), lambda i, k: (i // per_b, 0, 0)),
        ],
        out_specs=pl.BlockSpec((tm, D), lambda i, k: (i, 0)),
        out_shape=jax.ShapeDtypeStruct((M, D), F32),
        scratch_shapes=[pltpu.VMEM((tm, D), F32)],
        compiler_params=_params("parallel", "arbitrary"),
        name="down_post",
    )(h, wd.astype(BF16), x2, g.reshape(1, D), gate.reshape(B, 1, D))


def _merge_kernel(u_ref, wg_ref, y0_ref, y1_ref, y2_ref, y3_ref, wb_ref, o_ref):
    u = u_ref[...]
    acc = None
    for i, y_ref in enumerate((y0_ref, y1_ref, y2_ref, y3_ref)):
        gate = jax.nn.sigmoid(_dot(u, wg_ref[i]))
        term = gate * _dot(y_ref[...].astype(BF16), wb_ref[i])
        acc = term if acc is None else acc + term
    o_ref[...] = acc.astype(o_ref.dtype)


def _merge(u, w_gate, ys, w_branch, tm=512, tn=256):
    M, D = u.shape
    W = w_branch.shape[1]
    tm, tn = _tile(M, tm), _tile(D, tn)
    y_spec = pl.BlockSpec((tm, W), lambda i, j: (i, 0))
    return pl.pallas_call(
        _merge_kernel,
        grid=(M // tm, D // tn),
        in_specs=[
            pl.BlockSpec((tm, D), lambda i, j: (i, 0)),
            pl.BlockSpec((4, D, tn), lambda i, j: (0, 0, j)),
            y_spec, y_spec, y_spec, y_spec,
            pl.BlockSpec((4, W, tn), lambda i, j: (0, 0, j)),
        ],
        out_specs=pl.BlockSpec((tm, tn), lambda i, j: (i, j)),
        out_shape=jax.ShapeDtypeStruct((M, D), BF16),
        compiler_params=_params("parallel", "parallel"),
        name="merge",
    )(u, w_gate.astype(BF16), *ys, w_branch.astype(BF16))


def _flash_kernel(*refs, mode, t, G):
    if mode == "select":
        q_ref, k_ref, v_ref, sel_ref, o_ref, m_ref, l_ref, acc_ref = refs
    else:
        q_ref, k_ref, v_ref, o_ref, m_ref, l_ref, acc_ref = refs
    i = pl.program_id(2)
    cols = G * t
    q = q_ref[0, 0, 0]
    k_off = lax.broadcasted_iota(jnp.int32, (t, cols), 0)
    q_off = lax.broadcasted_iota(jnp.int32, (t, cols), 1) & (t - 1)
    blocks_per_tile = t // SLC_BLOCK

    def scores(j, mask):
        start = pl.multiple_of(j * t, t)
        s = _dot(k_ref[0, 0, pl.ds(start, t), :], q)
        if mode == "select":
            parts = []
            for kb in range(blocks_per_tile):
                picked = sel_ref[0, 0, pl.ds(j * blocks_per_tile + kb, 1), :]
                picked = jnp.concatenate([picked] * G, axis=1) if G > 1 else picked
                parts.append(jnp.where(picked > 0.5, s[kb * SLC_BLOCK:(kb + 1) * SLC_BLOCK], NEG_INF))
            s = jnp.concatenate(parts, axis=0)
        if mask is not None:
            s = jnp.where(mask, s, NEG_INF)
        return s, v_ref[0, 0, j]

    s, v = scores(i, k_off <= q_off)
    m = jnp.max(s, axis=0, keepdims=True)
    p = jnp.exp(s - m)
    m_ref[...] = m
    l_ref[...] = jnp.sum(p, axis=0, keepdims=True)
    acc_ref[...] = _dot(v, p.astype(BF16))

    def accumulate(j, mask=None):
        s, v = scores(j, mask)
        m_prev = m_ref[...]
        m_new = jnp.maximum(m_prev, jnp.max(s, axis=0, keepdims=True))
        p = jnp.exp(s - m_new)
        alpha = jnp.exp(m_prev - m_new)
        l_ref[...] = alpha * l_ref[...] + jnp.sum(p, axis=0, keepdims=True)
        acc_ref[...] = alpha * acc_ref[...] + _dot(v, p.astype(BF16))
        m_ref[...] = m_new

    if mode == "window":
        n_back = WIN_SIZE // t
        for back in range(1, n_back + 1):
            mask = (k_off > q_off) if back == n_back else None

            @pl.when(i >= back)
            def _(back=back, mask=mask):
                accumulate(i - back, mask)
    else:
        def body(j, carry):
            accumulate(j)
            return carry

        lax.fori_loop(0, i, body, 0)

    o_ref[0, 0, 0] = acc_ref[...] * (1.0 / l_ref[...])


def _flash(q, k, v, *, mode, sel=None):
    B, Hk, nq, Dq, cols = q.shape
    S = k.shape[2]
    Dv, t = v.shape[3], v.shape[4]
    G = cols // t
    assert t & (t - 1) == 0 and t % SLC_BLOCK == 0 and nq * t == S
    if mode == "window":
        assert WIN_SIZE % t == 0
    in_specs = [
        pl.BlockSpec((1, 1, 1, Dq, cols), lambda b, h, i: (b, h, i, 0, 0)),
        pl.BlockSpec((1, 1, S, Dq), lambda b, h, i: (b, h, 0, 0)),
        pl.BlockSpec((1, 1, nq, Dv, t), lambda b, h, i: (b, h, 0, 0, 0)),
    ]
    args = [q, k, v]
    if mode == "select":
        in_specs.append(pl.BlockSpec((1, 1, sel.shape[2], t), lambda b, h, i: (b, h, 0, i)))
        args.append(sel)
    return pl.pallas_call(
        functools.partial(_flash_kernel, mode=mode, t=t, G=G),
        grid=(B, Hk, nq),
        in_specs=in_specs,
        out_specs=pl.BlockSpec((1, 1, 1, Dv, cols), lambda b, h, i: (b, h, i, 0, 0)),
        out_shape=jax.ShapeDtypeStruct((B, Hk, nq, Dv, cols), F32),
        scratch_shapes=[pltpu.VMEM((1, cols), F32), pltpu.VMEM((1, cols), F32), pltpu.VMEM((Dv, cols), F32)],
        compiler_params=_params("parallel", "parallel", "arbitrary"),
        name="attn_" + mode,
    )(*args)


def _nsa_compress_kernel(t_ref, pos_ref, w1_ref, w2_ref, o_ref):
    n_pieces, half = t_ref.shape[-2], t_ref.shape[-1]
    for which in range(2):
        t = t_ref[which, 0, 0]
        w1 = w1_ref[which]
        first = _dot(t, w1[:half], HI)
        second = _dot(t, w1[half:], HI)
        pos_bias = _dot(pos_ref[which], w1, HI)
        pre = first + pltpu.roll(second, n_pieces - 1, 0) + pos_bias
        o_ref[which, 0, 0] = _dot(jax.nn.gelu(pre), w2_ref[which], HI)


def _nsa_compress(kv_flat, pos, w1, w2):
    _, B, Hk, n_pieces, half = kv_flat.shape
    hid, d = w2.shape[1], w2.shape[2]
    return pl.pallas_call(
        _nsa_compress_kernel,
        grid=(B, Hk),
        in_specs=[
            pl.BlockSpec((2, 1, 1, n_pieces, half), lambda b, h: (0, b, h, 0, 0)),
            pl.BlockSpec((2, 1, 2 * half), lambda b, h: (0, 0, 0)),
            pl.BlockSpec((2, 2 * half, hid), lambda b, h: (0, 0, 0)),
            pl.BlockSpec((2, hid, d), lambda b, h: (0, 0, 0)),
        ],
        out_specs=pl.BlockSpec((2, 1, 1, n_pieces, d), lambda b, h: (0, b, h, 0, 0)),
        out_shape=jax.ShapeDtypeStruct((2, B, Hk, n_pieces, d), F32),
        compiler_params=_params("parallel", "parallel"),
        name="nsa_compress",
    )(kv_flat, pos, w1, w2)


def _nsa_cmp_kernel(q_ref, kc_ref, vct_ref, ovt_ref, o_ref, sel_ref, *, t, G):
    i = pl.program_id(2)
    cols = G * t
    n_cmp = kc_ref.shape[-2]
    n_sel = ovt_ref.shape[0]
    s = _dot(kc_ref[0, 0], q_ref[0, 0, 0].astype(F32), HI)
    t_pos = i * t + (lax.broadcasted_iota(jnp.int32, (n_cmp, cols), 1) & (t - 1))
    blk_end = lax.broadcasted_iota(jnp.int32, (n_cmp, cols), 0) * CMP_STRIDE + (CMP_BLOCK - 1)
    mask = blk_end <= t_pos
    s = jnp.where(mask, s, NEG_INF)
    e = jnp.exp(s - jnp.max(s, axis=0, keepdims=True))
    p = jnp.where(mask, e * (1.0 / jnp.sum(e, axis=0, keepdims=True)), 0.0)
    o_ref[0, 0, 0] = _dot(vct_ref[0, 0], p, HI)

    p_group = p[:, 0:t]
    for g in range(1, G):
        p_group = p_group + p[:, g * t:(g + 1) * t]
    imp = _dot(ovt_ref[...], p_group, HI)
    blk = lax.broadcasted_iota(jnp.int32, (n_sel, t), 0)
    cur = jnp.right_shift(i * t + lax.broadcasted_iota(jnp.int32, (n_sel, t), 1), SLC_SHIFT)
    imp = jnp.where(blk <= cur, jnp.where(blk == 0, FORCE_SCORE, jnp.where(blk >= cur - 1, FORCE_SCORE, imp)),
                    NEG_INF)
    rank = jnp.zeros((n_sel, t), F32)
    for c in range(n_sel):
        other = imp[c:c + 1, :]
        later = jnp.where(blk > c, 1.0, 0.0)
        rank = rank + jnp.where(other > imp, 1.0, 0.0) + jnp.where(other == imp, later, 0.0)
    sel_ref[0, 0] = jnp.where(rank < float(min(SLC_TOPK, n_sel)), 1.0, 0.0)


def _nsa_cmp(q, kc, vct, overlap_t):
    B, Hk, nq, d, cols = q.shape
    n_cmp = kc.shape[2]
    n_sel = overlap_t.shape[0]
    t = n_sel * SLC_BLOCK // nq
    G = cols // t
    o_spec = pl.BlockSpec((1, 1, 1, d, cols), lambda b, h, i: (b, h, i, 0, 0))
    return pl.pallas_call(
        functools.partial(_nsa_cmp_kernel, t=t, G=G),
        grid=(B, Hk, nq),
        in_specs=[
            o_spec,
            pl.BlockSpec((1, 1, n_cmp, d), lambda b, h, i: (b, h, 0, 0)),
            pl.BlockSpec((1, 1, d, n_cmp), lambda b, h, i: (b, h, 0, 0)),
            pl.BlockSpec((n_sel, n_cmp), lambda b, h, i: (0, 0)),
        ],
        out_specs=[o_spec, pl.BlockSpec((1, 1, n_sel, t), lambda b, h, i: (b, h, 0, i))],
        out_shape=[
            jax.ShapeDtypeStruct((B, Hk, nq, d, cols), F32),
            jax.ShapeDtypeStruct((B, Hk, n_sel, nq * t), F32),
        ],
        compiler_params=_params("parallel", "parallel", "parallel"),
        name="nsa_cmp",
    )(q, kc, vct, overlap_t)


def _nsa_combine_kernel(gl_ref, oc_ref, os_ref, ow_ref, o_ref):
    g = jax.nn.sigmoid(gl_ref[0, 0, 0])
    o_ref[0, 0, 0] = g[0:1] * oc_ref[0, 0, 0] + g[1:2] * os_ref[0, 0, 0] + g[2:3] * ow_ref[0, 0, 0]


def _nsa_combine(gl, o_cmp, o_slc, o_win):
    B, Hk, nq, d, cols = o_cmp.shape
    o_spec = pl.BlockSpec((1, 1, 1, d, cols), lambda b, h, i: (b, h, i, 0, 0))
    return pl.pallas_call(
        _nsa_combine_kernel,
        grid=(B, Hk, nq),
        in_specs=[pl.BlockSpec((1, 1, 1, 3, cols), lambda b, h, i: (b, h, i, 0, 0)), o_spec, o_spec, o_spec],
        out_specs=o_spec,
        out_shape=jax.ShapeDtypeStruct(o_cmp.shape, F32),
        compiler_params=_params("parallel", "parallel", "parallel"),
        name="nsa_combine",
    )(gl, o_cmp, o_slc, o_win)


def _rwkv_prep_kernel(p_ref, mu_ref, w0_ref, ww2_ref, a0_ref, aw2_ref, gw2_ref, kk_ref, ka_ref, seg_ref,
                      r_o, lw_o, k_o, v_o, kk_o, b_o, g_o, last_ref):
    tb = p_ref.shape[1]
    width = r_o.shape[-1]

    @pl.when(pl.program_id(1) == 0)
    def _():
        last_ref[...] = jnp.zeros_like(last_ref)

    p = p_ref[0]
    row = lax.broadcasted_iota(jnp.int32, p.shape, 0)
    prev = jnp.where(row == 0, last_ref[...], pltpu.roll(p, 1, 0))
    last_ref[...] = p[tb - 1:tb]
    xs = p + (prev - p) * mu_ref[...]
    r = xs[:, 0:width]
    k = xs[:, width:2 * width]
    v = xs[:, 2 * width:3 * width]
    lora = xs[:, 3 * width:3 * width + 128]
    g_lo = xs[:, 3 * width + 128:]
    lw = -math.exp(-0.5) * jax.nn.sigmoid(w0_ref[...] + _dot(jnp.tanh(lora), ww2_ref[...], HI))
    a = jax.nn.sigmoid(a0_ref[...] + _dot(lora, aw2_ref[...], HI))
    g = _dot(jax.nn.sigmoid(g_lo), gw2_ref[...], HI)
    kk = k * kk_ref[...]
    norm = jnp.sqrt(_dot(kk * kk, seg_ref[...], HI))
    kk = kk / jnp.maximum(norm, 1e-12)
    r_o[0] = r
    lw_o[0] = lw
    k_o[0] = k * (1.0 + (a - 1.0) * ka_ref[...])
    v_o[0] = v
    kk_o[0] = kk
    b_o[0] = kk * a
    g_o[0] = g


def _rwkv_prep(p, consts, tb=256):
    B, S, P = p.shape
    width = consts[1].shape[-1]
    tb = _tile(S, tb)
    const_specs = [pl.BlockSpec(c.shape, lambda b, t: (0, 0)) for c in consts]
    o_spec = pl.BlockSpec((1, tb, width), lambda b, t: (b, t, 0))
    return pl.pallas_call(
        _rwkv_prep_kernel,
        grid=(B, S // tb),
        in_specs=[pl.BlockSpec((1, tb, P), lambda b, t: (b, t, 0))] + const_specs,
        out_specs=[o_spec] * 7,
        out_shape=[jax.ShapeDtypeStruct((B, S, width), F32)] * 7,
        scratch_shapes=[pltpu.VMEM((1, P), F32)],
        compiler_params=_params("parallel", "arbitrary"),
        name="rwkv_prep",
    )(p, *consts)


def _cumsum_rows(x):
    n = x.shape[0]
    row = lax.broadcasted_iota(jnp.int32, x.shape, 0)
    step = 1
    while step < n:
        x = x + jnp.where(row >= step, pltpu.roll(x, step, 0), 0.0)
        step *= 2
    return x


def _seg_sum(x, seg):
    rows = x.shape[0]
    hi = x.astype(BF16)
    lo = (x - hi.astype(F32)).astype(BF16)
    both = _dot(jnp.concatenate([hi, lo], axis=0), seg)
    return both[:rows] + both[rows:]


def _rwkv_group_chunk(rg, kkg, bd, kd, kc, bc, v, gamma, state, lane_masks, strict, incl, eye, chunk):
    stack = lambda x: jnp.concatenate([x * m for m in lane_masks], axis=0)
    kks, rs, vs = stack(kkg), stack(rg), stack(v)
    n, w = kks.shape
    big = _dot_nt(jnp.concatenate([kks, rs], axis=0), jnp.concatenate([stack(bd), stack(kd)], axis=0))
    a_mat = jnp.where(strict, big[:n, :n], 0.0)
    bk = jnp.where(strict, big[:n, n:], 0.0)
    ay = jnp.where(incl, big[n:, :n], 0.0)
    by = jnp.where(incl, big[n:, n:], 0.0)
    power = -a_mat
    inv = eye + power
    power = _dot(power, power)
    for _ in range(int(math.log2(chunk)) - 2):
        both = _dot(jnp.concatenate([power, inv], axis=0), power)
        power, inv = both[:n], inv + both[n:]
    inv = inv + _dot(inv, power)
    bv = _dot(jnp.concatenate([bk, by], axis=0), vs)
    tx = _dot(inv, jnp.concatenate([kks, bv[:n]], axis=1))
    ayx = _dot(ay, tx)
    q_hat = rs - ayx[:, :w]
    y_intra = bv[n:] - ayx[:, w:]
    qt = _dot_nt(jnp.concatenate([q_hat, tx[:, :w]], axis=0), state)
    ys = qt[:n] + y_intra
    us = qt[n:] + tx[:, w:]
    new_state = state * gamma + _dot_tn(jnp.concatenate([vs, us], axis=0),
                                        jnp.concatenate([stack(kc), -stack(bc)], axis=0))
    y = ys[:chunk]
    for h in range(1, len(lane_masks)):
        y = y + ys[h * chunk:(h + 1) * chunk]
    return y, new_state


def _rwkv_scan_kernel(r_ref, lw_ref, k_ref, v_ref, kk_ref, b_ref, g_ref, lnw_ref, lnb_ref, rk_ref, seg_ref,
                      o_ref, state_ref, *, heads, chunk, group):
    tb, width = r_ref.shape[1], r_ref.shape[2]
    n = width // heads
    groups, gw, gn = heads // group, group * n, group * chunk

    @pl.when(pl.program_id(1) == 0)
    def _():
        state_ref[...] = jnp.zeros_like(state_ref)

    lane = lax.broadcasted_iota(jnp.int32, (1, gw), 1)
    lane_masks = [jnp.where(jnp.logical_and(lane >= h * n, lane < (h + 1) * n), 1.0, 0.0) for h in range(group)]
    row_id = lax.broadcasted_iota(jnp.int32, (gn, gn), 0)
    col_id = lax.broadcasted_iota(jnp.int32, (gn, gn), 1)
    row, col = row_id & (chunk - 1), col_id & (chunk - 1)
    strict, incl = col < row, col <= row
    eye = jnp.where(row_id == col_id, 1.0, 0.0)
    states = [state_ref[p] for p in range(groups)]
    y_chunks = []
    for c in range(tb // chunk):
        rows = pl.ds(c * chunk, chunk)
        r, lw, k, v = r_ref[0, rows, :], lw_ref[0, rows, :], k_ref[0, rows, :], v_ref[0, rows, :]
        kk, bb = kk_ref[0, rows, :], b_ref[0, rows, :]
        g = _cumsum_rows(lw)
        g_last = g[chunk - 1:chunk]
        decay_out = jnp.exp(-g)
        decay_end = jnp.exp(g_last - g)
        rg, kkg = r * jnp.exp(g), kk * jnp.exp(g - lw)
        bd, kd = bb * decay_out, k * decay_out
        kc, bc = k * decay_end, bb * decay_end
        gamma = jnp.exp(g_last)
        ys = []
        for p in range(groups):
            sl = slice(p * gw, (p + 1) * gw)
            y, states[p] = _rwkv_group_chunk(rg[:, sl], kkg[:, sl], bd[:, sl], kd[:, sl], kc[:, sl], bc[:, sl],
                                             v[:, sl], gamma[:, sl], states[p], lane_masks, strict, incl, eye, chunk)
            ys.append(y)
        y_chunks.append(jnp.concatenate(ys, axis=1))
    for p in range(groups):
        state_ref[p] = states[p]
    y = jnp.concatenate(y_chunks, axis=0)
    seg = seg_ref[...]
    r, k, v = r_ref[0], k_ref[0], v_ref[0]
    sums = _seg_sum(jnp.concatenate([y, r * k * rk_ref[...]], axis=0), seg)
    dev = y - sums[:tb] * (1.0 / n)
    var = _seg_sum(dev * dev, seg) * (1.0 / n)
    yn = dev * lax.rsqrt(var + RWKV_GN_EPS) * lnw_ref[...] + lnb_ref[...]
    o_ref[0] = (yn + sums[tb:] * v) * g_ref[0]


def _rwkv_scan(r, lw, k, v, kk, bb, g, ln_w, ln_b, r_k, seg, tb=256, group=4):
    B, S, width = r.shape
    heads = RWKV_HEADS
    n = width // heads
    chunk = min(RWKV_CHUNK, S)
    assert chunk == n and heads % group == 0
    tb = _tile(S, tb)
    x_spec = pl.BlockSpec((1, tb, width), lambda b, t: (b, t, 0))
    vec_spec = pl.BlockSpec((1, width), lambda b, t: (0, 0))
    return pl.pallas_call(
        functools.partial(_rwkv_scan_kernel, heads=heads, chunk=chunk, group=group),
        grid=(B, S // tb),
        in_specs=[x_spec] * 7 + [vec_spec] * 3 + [pl.BlockSpec((width, width), lambda b, t: (0, 0))],
        out_specs=x_spec,
        out_shape=jax.ShapeDtypeStruct((B, S, width), F32),
        scratch_shapes=[pltpu.VMEM((heads // group, group * n, group * n), F32)],
        compiler_params=_params("parallel", "arbitrary"),
        name="rwkv_scan",
    )(r, lw, k, v, kk, bb, g, ln_w, ln_b, r_k, seg.astype(BF16))


def _gla_kernel(q_ref, k_ref, v_ref, r_ref, al_ref, aw_ref, ab_ref, ng_ref, o_ref, state_ref,
                *, heads, chunk):
    tb = v_ref.shape[1]
    dk = q_ref.shape[-1]
    dv = v_ref.shape[-1] // heads

    @pl.when(pl.program_id(1) == 0)
    def _():
        state_ref[...] = jnp.zeros_like(state_ref)

    row = lax.broadcasted_iota(jnp.int32, (chunk, chunk), 0)
    col = lax.broadcasted_iota(jnp.int32, (chunk, chunk), 1)
    incl = col <= row
    states = [state_ref[h] for h in range(heads)]
    for c in range(tb // chunk):
        rows = pl.ds(c * chunk, chunk)
        alpha_lo = al_ref[0, rows, :]
        outs = []
        for h in range(heads):
            cols = pl.ds(h * dv, dv)
            x = _dot(alpha_lo, aw_ref[h], HI) + ab_ref[h]
            log_a = (jnp.minimum(x, 0.0) - jnp.log(1.0 + jnp.exp(-jnp.abs(x)))) * (1.0 / GLA_TAU)
            b = _cumsum_rows(log_a)
            b_last = b[chunk - 1:chunk]
            q = q_ref[0, h, rows, :]
            k = k_ref[0, h, rows, :]
            v = v_ref[0, rows, cols]
            qe = q * (dk ** -0.5) * jnp.exp(b)
            attn = jnp.where(incl, _dot_nt(qe, k * jnp.exp(-b)), 0.0)
            o = _dot(attn, v) + _dot_nt(qe, states[h])
            states[h] = states[h] * jnp.exp(b_last) + _dot_tn(v, k * jnp.exp(b_last - b))
            outs.append(o * lax.rsqrt(jnp.mean(o * o, axis=-1, keepdims=True) + NORM_EPS))
        gate = r_ref[0, rows, :]
        o_ref[0, rows, :] = jnp.concatenate(outs, axis=1) * ng_ref[...] * (gate * jax.nn.sigmoid(gate))
    for h in range(heads):
        state_ref[h] = states[h]


def _gla(q, k, v, r, alpha_lo, aw, ab, ng, tb=256):
    B, H, S, dk = q.shape
    width = v.shape[-1]
    lora = alpha_lo.shape[-1]
    chunk = min(GLA_CHUNK, S)
    tb = _tile(S, tb)
    qk_spec = pl.BlockSpec((1, H, tb, dk), lambda b, t: (b, 0, t, 0))
    x_spec = pl.BlockSpec((1, tb, width), lambda b, t: (b, t, 0))
    return pl.pallas_call(
        functools.partial(_gla_kernel, heads=H, chunk=chunk),
        grid=(B, S // tb),
        in_specs=[
            qk_spec, qk_spec, x_spec, x_spec,
            pl.BlockSpec((1, tb, lora), lambda b, t: (b, t, 0)),
            pl.BlockSpec(aw.shape, lambda b, t: (0, 0, 0)),
            pl.BlockSpec(ab.shape, lambda b, t: (0, 0, 0)),
            pl.BlockSpec(ng.shape, lambda b, t: (0, 0)),
        ],
        out_specs=x_spec,
        out_shape=jax.ShapeDtypeStruct((B, S, width), F32),
        scratch_shapes=[pltpu.VMEM((H, width // H, dk), F32)],
        compiler_params=_params("parallel", "arbitrary"),
        name="gla",
    )(q, k, v, r, alpha_lo, aw, ab, ng)


def _rope_tables(S, d):
    inv = ROPE_THETA ** (-jnp.arange(0, d, 2, dtype=F32) / d)
    ang = jnp.arange(S).astype(F32)[:, None] * inv[None, :]
    cos, sin = jnp.cos(ang), jnp.sin(ang)
    return jnp.concatenate([cos, cos], axis=-1), jnp.concatenate([sin, sin], axis=-1)


def _rot_cols(w, d):
    K = w.shape[0]
    w = w.reshape(K, -1, 2, d // 2)
    return jnp.concatenate([-w[:, :, 1], w[:, :, 0]], axis=-1).reshape(K, -1)


def _nsa_constants(S):
    n_pieces = S // CMP_STRIDE
    n_cmp = (S - CMP_BLOCK) // CMP_STRIDE + 1
    n_sel = S // SLC_BLOCK
    start = np.arange(n_cmp)[:, None] * CMP_STRIDE
    end = start + CMP_BLOCK - 1
    blk = np.arange(n_sel)[None, :]
    overlap = np.zeros((n_pieces, n_sel), np.float32)
    overlap[:n_cmp] = ((start < (blk + 1) * SLC_BLOCK) & (end >= blk * SLC_BLOCK)).astype(np.float32)
    return jnp.asarray(overlap.T)


def _token_mixing(u, B, S, w_in, w_branch, cmp_pos, cmp_w1, cmp_w2, rwkv_mu, rwkv_w0, rwkv_w_w2, rwkv_a0,
                  rwkv_a_w2, rwkv_g_w2, rwkv_k_k, rwkv_k_a, rwkv_r_k, rwkv_ln_w, rwkv_ln_b, gla_alpha_w2,
                  gla_alpha_b, gla_norm_g, mla_q_norm, mla_w_uq, mla_kv_norm, mla_w_ukv):
    M, D = u.shape
    d = HEAD_DIM
    width = D // 4
    sizes = [width + 6 * NSA_KV_HEADS * d + 3 * NSA_HEADS, 3 * width + 256, 2 * (width // 2) + 2 * width + 16,
             384 + 128 + MLA_ROPE, 4 * D]
    offs = np.cumsum([0] + sizes)
    assert offs[-1] == w_in.shape[1]
    w_nsa, w_rwkv, w_gla, w_mla, w_gate = [w_in[:, offs[i]:offs[i + 1]] for i in range(5)]
    zeros = lambda n: jnp.zeros((D, n), F32)
    kvw = NSA_KV_HEADS * d

    wq = w_nsa[:, :width]
    wkc, wvc, wks, wvs, wkw, wvw = [w_nsa[:, width + i * kvw:width + (i + 1) * kvw] for i in range(6)]
    wgl = w_nsa[:, width + 6 * kvw:]
    wkr = w_mla[:, 512:512 + MLA_ROPE]
    n_rope = width + 3 * kvw + MLA_ROPE
    pad = (-n_rope) % 512
    w_a = jnp.concatenate([wq, wkc, wks, wkw, wkr, zeros(pad)], axis=1)
    w_b = jnp.concatenate([_rot_cols(wq, d), _rot_cols(wkc, d), _rot_cols(wks, d), _rot_cols(wkw, d),
                           _rot_cols(wkr, MLA_ROPE), zeros(pad)], axis=1)
    cos_d, sin_d = _rope_tables(S, d)
    cos_r, sin_r = _rope_tables(S, MLA_ROPE)
    reps = (width + 3 * kvw) // d
    cos_t = jnp.concatenate([jnp.tile(cos_d, (1, reps)), cos_r, jnp.ones((S, pad), F32)], axis=1)
    sin_t = jnp.concatenate([jnp.tile(sin_d, (1, reps)), sin_r, jnp.zeros((S, pad), F32)], axis=1)
    roped = _matmul(u, w_a, S=S, w_rot=w_b, cos=cos_t, sin=sin_t, name="proj_rope").reshape(B, S, -1)
    n_plain = 3 * kvw + 3 * NSA_HEADS
    plain = _matmul(u, jnp.concatenate([wvc, wvs, wvw, wgl, zeros((-n_plain) % 128)], axis=1), S=S,
                    name="proj_nsa_v")

    Hk, G = NSA_KV_HEADS, NSA_GROUP
    t_att = _tile(S, 256)
    nq = S // t_att
    cols_layout = lambda x, f: x.reshape(B, nq, t_att, Hk, G, f).transpose(0, 3, 1, 5, 4, 2).reshape(
        B, Hk, nq, f, G * t_att)
    kv_heads = lambda x: x.reshape(B, S, Hk, d).transpose(0, 2, 1, 3)
    v_tiles = lambda x: x.astype(BF16).reshape(B, nq, t_att, Hk, d).transpose(0, 3, 1, 4, 2)
    q_att = cols_layout((roped[..., :width] * d ** -0.5).astype(BF16), d)
    kc_in = kv_heads(roped[..., width:width + kvw])
    ks = kv_heads(roped[..., width + kvw:width + 2 * kvw].astype(BF16))
    kw = kv_heads(roped[..., width + 2 * kvw:width + 3 * kvw].astype(BF16))
    k_pe = roped[..., width + 3 * kvw:width + 3 * kvw + MLA_ROPE]
    plain3 = plain.reshape(B, S, -1)
    vc_in = kv_heads(plain3[..., :kvw])
    vs, vw = v_tiles(plain3[..., kvw:2 * kvw]), v_tiles(plain3[..., 2 * kvw:3 * kvw])
    gate_logits = cols_layout(plain3[..., 3 * kvw:3 * kvw + 3 * NSA_HEADS], 3)

    overlap_t = _nsa_constants(S)
    n_pieces = S // CMP_STRIDE
    kv_flat = jnp.stack([kc_in, vc_in]).reshape(2, B, Hk, n_pieces, CMP_STRIDE * d)
    cmp = _nsa_compress(kv_flat, cmp_pos.reshape(2, 1, CMP_BLOCK * d), cmp_w1, cmp_w2)
    o_cmp, sel = _nsa_cmp(q_att, cmp[0], cmp[1].transpose(0, 1, 3, 2), overlap_t)
    o_slc = _flash(q_att, ks, vs, mode="select", sel=sel)
    o_win = _flash(q_att, kw, vw, mode="window")
    y_nsa = _nsa_combine(gate_logits, o_cmp, o_slc, o_win)
    y_nsa = y_nsa.reshape(B, Hk, nq, d, G, t_att).transpose(0, 2, 5, 1, 4, 3).reshape(M, width)

    regroup = lambda t: jnp.concatenate(
        [t[..., 0:width], t[..., width + 64:2 * width + 64], t[..., 2 * width + 64:3 * width + 64],
         t[..., width:width + 64], t[..., 3 * width + 64:3 * width + 128], t[..., 3 * width + 128:]], axis=-1)
    p_rwkv = _matmul(u, regroup(w_rwkv), S=S, tn=256, name="proj_rwkv").reshape(B, S, -1)
    row = lambda t: t.reshape(1, -1)
    seg = jnp.asarray(np.kron(np.eye(RWKV_HEADS), np.ones((d, d))).astype(np.float32))
    lora_pad = jnp.zeros((64, width), F32)
    consts = [row(regroup(rwkv_mu)), row(rwkv_w0), jnp.concatenate([rwkv_w_w2, lora_pad], axis=0), row(rwkv_a0),
              jnp.concatenate([lora_pad, rwkv_a_w2], axis=0), rwkv_g_w2, row(rwkv_k_k), row(rwkv_k_a), seg]
    r, lw, k2, v, kk, bb, g = _rwkv_prep(p_rwkv, consts)
    y_rwkv = _rwkv_scan(r, lw, k2, v, kk, bb, g, row(rwkv_ln_w), row(rwkv_ln_b), row(rwkv_r_k), seg)
    y_rwkv = y_rwkv.reshape(M, width)

    dk_all = width // 2
    w_gla_p = jnp.concatenate([w_gla[:, :2 * dk_all + width], w_gla[:, 2 * dk_all + width + 16:],
                               w_gla[:, 2 * dk_all + width:2 * dk_all + width + 16], zeros(240)], axis=1)
    p_gla = _matmul(u, w_gla_p, S=S, tn=256, name="proj_gla").reshape(B, S, -1)
    dkh = dk_all // GLA_HEADS
    gla_heads = lambda t: t.reshape(B, S, GLA_HEADS, dkh).transpose(0, 2, 1, 3)
    y_gla = _gla(gla_heads(p_gla[..., :dk_all]), gla_heads(p_gla[..., dk_all:2 * dk_all]),
                 p_gla[..., 2 * dk_all:2 * dk_all + width], p_gla[..., 2 * dk_all + width:2 * dk_all + 2 * width],
                 p_gla[..., 2 * dk_all + 2 * width:2 * dk_all + 2 * width + 16],
                 gla_alpha_w2.reshape(16, GLA_HEADS, dkh).transpose(1, 0, 2), gla_alpha_b.reshape(GLA_HEADS, 1, dkh),
                 row(gla_norm_g)).reshape(M, width)

    c_q = _matmul(u, w_mla[:, :384], S=S, name="proj_mla_q")
    c_kv = _matmul(u, w_mla[:, 384:512], S=S, name="proj_mla_kv")
    H = MLA_HEADS
    wq3 = mla_w_uq.reshape(384, H, MLA_NOPE + MLA_ROPE)
    rp = wq3[:, :, MLA_NOPE:]
    half = MLA_ROPE // 2
    z = lambda n: jnp.zeros((384, H, n), F32)
    wq_a = jnp.concatenate([wq3, z(128 - MLA_NOPE - MLA_ROPE)], axis=-1).reshape(384, H * 128)
    wq_b = jnp.concatenate([z(MLA_NOPE), -rp[..., half:], rp[..., :half], z(128 - MLA_NOPE - MLA_ROPE)],
                           axis=-1).reshape(384, H * 128)
    tail = 128 - MLA_NOPE - MLA_ROPE
    cos_h = jnp.concatenate([jnp.ones((S, MLA_NOPE), F32), cos_r, jnp.ones((S, tail), F32)], axis=1)
    sin_h = jnp.concatenate([jnp.zeros((S, MLA_NOPE), F32), sin_r, jnp.zeros((S, tail), F32)], axis=1)
    q_m = _matmul(c_q, wq_a, S=S, g=mla_q_norm, w_rot=wq_b, cos=jnp.tile(cos_h, (1, H)), sin=jnp.tile(sin_h, (1, H)),
                  name="mla_q")
    t_mla = _tile(S, 512)
    nq_m = S // t_mla
    tiles_m = lambda x, f: x.reshape(B, nq_m, t_mla, H, f).transpose(0, 3, 1, 4, 2)
    q_m = tiles_m((q_m * (MLA_NOPE + MLA_ROPE) ** -0.5).astype(BF16), 128)
    kv_m = _matmul(c_kv, mla_w_ukv, S=S, g=mla_kv_norm, name="mla_kv").reshape(B, S, H, MLA_NOPE + width // H)
    k_m = jnp.concatenate([kv_m[..., :MLA_NOPE], jnp.broadcast_to(k_pe[:, :, None, :], (B, S, H, MLA_ROPE)),
                           jnp.zeros((B, S, H, tail), F32)], axis=-1).astype(BF16).transpose(0, 2, 1, 3)
    v_m = tiles_m(kv_m[..., MLA_NOPE:].astype(BF16), width // H)
    o_mla = _flash(q_m, k_m, v_m, mode="causal")
    y_mla = o_mla.transpose(0, 2, 4, 1, 3).reshape(M, width)

    w_gate4 = w_gate.reshape(D, 4, D).transpose(1, 0, 2)
    return _merge(u, w_gate4, (y_nsa, y_rwkv, y_gla, y_mla), w_branch)


def kernel(x, c, ada_w, ada_b, pre_g, post_g, ffn_wg, ffn_wu, ffn_wd, mix_w_in, mix_w_branch, mix_w_out, nsa_cmp_pos, nsa_cmp_w1, nsa_cmp_w2, rwkv_mu, rwkv_w0, rwkv_w_w2, rwkv_a0, rwkv_a_w2, rwkv_g_w2, rwkv_k_k, rwkv_k_a, rwkv_r_k, rwkv_ln_w, rwkv_ln_b, gla_alpha_w2, gla_alpha_b, gla_norm_g, mla_q_norm, mla_w_uq, mla_kv_norm, mla_w_ukv):
    B, S, D = x.shape
    M = B * S
    depth = ada_w.shape[0]
    mod = _modulation(c, ada_w, ada_b)
    x2 = x.reshape(M, D)
    for l in range(depth):
        for i in range(3):
            m = mod[3 * l + i]
            shift, scale, gate = m[:, :D], m[:, D:2 * D], m[:, 2 * D:]
            u = _prenorm(x2, pre_g[l, i], scale, shift, S)
            if i == 1:
                y = _token_mixing(
                    u, B, S, mix_w_in[l], mix_w_branch[l], nsa_cmp_pos[l], nsa_cmp_w1[l], nsa_cmp_w2[l], rwkv_mu[l],
                    rwkv_w0[l], rwkv_w_w2[l], rwkv_a0[l], rwkv_a_w2[l], rwkv_g_w2[l], rwkv_k_k[l], rwkv_k_a[l],
                    rwkv_r_k[l], rwkv_ln_w[l], rwkv_ln_b[l], gla_alpha_w2[l], gla_alpha_b[l], gla_norm_g[l],
                    mla_q_norm[l], mla_w_uq[l], mla_kv_norm[l], mla_w_ukv[l])
                x2 = _down_post(y, mix_w_out[l], x2, post_g[l, i], gate, 1.0, S)
            else:
                j = 0 if i == 0 else 1
                h = _ffn_up(u, ffn_wg[l, j], ffn_wu[l, j])
                x2 = _down_post(h, ffn_wd[l, j], x2, post_g[l, i], gate, 0.5, S)
    return x2.reshape(B, S, D)
```

```python
import functools
import math

import jax
import jax.numpy as jnp
import numpy as np
from jax import lax
from jax.experimental import pallas as pl
from jax.experimental.pallas import tpu as pltpu

F32 = jnp.float32
BF16 = jnp.bfloat16
HI = lax.Precision.HIGHEST

NORM_EPS = 1e-6
NEG_INF = -1e30
FORCE_SCORE = 1e30
ROPE_THETA = 10000.0
HEAD_DIM = 64

NSA_HEADS = 8
NSA_KV_HEADS = 2
NSA_GROUP = NSA_HEADS // NSA_KV_HEADS
CMP_BLOCK = 32
CMP_STRIDE = 16
SLC_BLOCK = 64
SLC_SHIFT = 6
SLC_TOPK = 16
WIN_SIZE = 512
RWKV_HEADS = 8
RWKV_GN_EPS = 64e-5
RWKV_CHUNK = 64
GLA_HEADS = 4
GLA_TAU = 16.0
GLA_CHUNK = 64
MLA_HEADS = 8
MLA_NOPE = 64
MLA_ROPE = 32

VMEM_LIMIT_BYTES = 56 * 1024 * 1024


def _params(*sem):
    return pltpu.CompilerParams(dimension_semantics=sem, vmem_limit_bytes=VMEM_LIMIT_BYTES)


def _dot(a, b, precision=None):
    return jnp.dot(a, b, preferred_element_type=F32, precision=precision)


def _dot_nt(a, b, precision=None):
    return lax.dot_general(a, b, (((1,), (1,)), ((), ())), preferred_element_type=F32, precision=precision)


def _dot_tn(a, b, precision=None):
    return lax.dot_general(a, b, (((0,), (0,)), ((), ())), preferred_element_type=F32, precision=precision)


def _rms(x):
    return x * lax.rsqrt(jnp.mean(x * x, axis=-1, keepdims=True) + NORM_EPS)


def _tile(n, t):
    t = min(n, t)
    assert n % t == 0, (n, t)
    return t


def _mod_kernel(c_ref, w_ref, b_ref, o_ref):
    c = c_ref[...]
    o_ref[0] = _dot(c * jax.nn.sigmoid(c), w_ref[0], HI) + b_ref[0]


def _modulation(c, ada_w, ada_b):
    B, D = c.shape
    n_sub = ada_w.shape[0] * ada_w.shape[1]
    w = ada_w.reshape(n_sub, D, 3 * D)
    b = ada_b.reshape(n_sub, 1, 3 * D)
    rows = 8
    cp = jnp.zeros((rows, D), F32).at[:B].set(c)
    tn = _tile(3 * D, 512)
    out = pl.pallas_call(
        _mod_kernel,
        grid=(n_sub, 3 * D // tn),
        in_specs=[
            pl.BlockSpec((rows, D), lambda s, j: (0, 0)),
            pl.BlockSpec((1, D, tn), lambda s, j: (s, 0, j)),
            pl.BlockSpec((1, 1, tn), lambda s, j: (s, 0, j)),
        ],
        out_specs=pl.BlockSpec((1, rows, tn), lambda s, j: (s, 0, j)),
        out_shape=jax.ShapeDtypeStruct((n_sub, rows, 3 * D), F32),
        compiler_params=_params("parallel", "parallel"),
        name="modulation",
    )(cp, w, b)
    return out[:, :B]


def _prenorm_kernel(x_ref, g_ref, sc_ref, sh_ref, o_ref):
    y = _rms(x_ref[...]) * g_ref[...]
    o_ref[...] = (y * (1.0 + sc_ref[0]) + sh_ref[0]).astype(o_ref.dtype)


def _prenorm(x2, g, scale, shift, S):
    M, D = x2.shape
    B = M // S
    tm = _tile(S, 512)
    per_b = S // tm
    return pl.pallas_call(
        _prenorm_kernel,
        grid=(M // tm,),
        in_specs=[
            pl.BlockSpec((tm, D), lambda i: (i, 0)),
            pl.BlockSpec((1, D), lambda i: (0, 0)),
            pl.BlockSpec((1, 1, D), lambda i: (i // per_b, 0, 0)),
            pl.BlockSpec((1, 1, D), lambda i: (i // per_b, 0, 0)),
        ],
        out_specs=pl.BlockSpec((tm, D), lambda i: (i, 0)),
        out_shape=jax.ShapeDtypeStruct((M, D), BF16),
        compiler_params=_params("parallel"),
        name="prenorm",
    )(x2, g.reshape(1, D), scale.reshape(B, 1, D), shift.reshape(B, 1, D))


def _mm_kernel(*refs, norm, rope):
    it = iter(refs)
    a_ref = next(it)
    g_ref = next(it) if norm else None
    w_ref = next(it)
    if rope:
        w2_ref, cos_ref, sin_ref = next(it), next(it), next(it)
    o_ref = next(it)
    a = a_ref[...]
    if norm:
        a = _rms(a.astype(F32)) * g_ref[...]
    a = a.astype(BF16)
    out = _dot(a, w_ref[...])
    if rope:
        out = out * cos_ref[...] + _dot(a, w2_ref[...]) * sin_ref[...]
    o_ref[...] = out.astype(o_ref.dtype)


def _matmul(a, w, *, S, g=None, w_rot=None, cos=None, sin=None, out_dtype=F32, tm=1024, tn=512, name="matmul"):
    M, K = a.shape
    N = w.shape[1]
    tm = _tile(S, tm)
    tn = _tile(N, tn)
    per_b = S // tm
    norm, rope = g is not None, w_rot is not None
    in_specs = [pl.BlockSpec((tm, K), lambda i, j: (i, 0))]
    args = [a]
    if norm:
        in_specs.append(pl.BlockSpec((1, K), lambda i, j: (0, 0)))
        args.append(g.reshape(1, K))
    in_specs.append(pl.BlockSpec((K, tn), lambda i, j: (0, j)))
    args.append(w.astype(BF16))
    if rope:
        in_specs += [
            pl.BlockSpec((K, tn), lambda i, j: (0, j)),
            pl.BlockSpec((tm, tn), lambda i, j: (i % per_b, j)),
            pl.BlockSpec((tm, tn), lambda i, j: (i % per_b, j)),
        ]
        args += [w_rot.astype(BF16), cos, sin]
    return pl.pallas_call(
        functools.partial(_mm_kernel, norm=norm, rope=rope),
        grid=(M // tm, N // tn),
        in_specs=in_specs,
        out_specs=pl.BlockSpec((tm, tn), lambda i, j: (i, j)),
        out_shape=jax.ShapeDtypeStruct((M, N), out_dtype),
        compiler_params=_params("parallel", "parallel"),
        name=name,
    )(*args)


def _ffn_up_kernel(u_ref, wg_ref, wu_ref, o_ref, wg_bf, wu_bf):
    @pl.when(pl.program_id(1) == 0)
    def _():
        wg_bf[...] = wg_ref[...].astype(BF16)
        wu_bf[...] = wu_ref[...].astype(BF16)

    u = u_ref[...]
    a = _dot(u, wg_bf[...])
    b = _dot(u, wu_bf[...])
    o_ref[...] = (a * jax.nn.sigmoid(a) * b).astype(o_ref.dtype)


def _ffn_up(u, wg, wu, tm=1024, tn=512):
    M, K = u.shape
    N = wg.shape[1]
    tm, tn = _tile(M, tm), _tile(N, tn)
    return pl.pallas_call(
        _ffn_up_kernel,
        grid=(N // tn, M // tm),
        in_specs=[
            pl.BlockSpec((tm, K), lambda j, i: (i, 0)),
            pl.BlockSpec((K, tn), lambda j, i: (0, j)),
            pl.BlockSpec((K, tn), lambda j, i: (0, j)),
        ],
        out_specs=pl.BlockSpec((tm, tn), lambda j, i: (i, j)),
        out_shape=jax.ShapeDtypeStruct((M, N), BF16),
        scratch_shapes=[pltpu.VMEM((K, tn), BF16), pltpu.VMEM((K, tn), BF16)],
        compiler_params=_params("parallel", "arbitrary"),
        name="ffn_up",
    )(u, wg, wu)


def _down_post_kernel(h_ref, w_ref, x_ref, g_ref, gate_ref, o_ref, *, res_w):
    y = _rms(_dot(h_ref[...], w_ref[...])) * g_ref[...]
    o_ref[...] = x_ref[...] + res_w * gate_ref[0] * y


def _down_post(h, wd, x2, g, gate, res_w, S, tm=256):
    M, K = h.shape
    D = wd.shape[1]
    B = M // S
    tm = _tile(S, tm)
    per_b = S // tm
    return pl.pallas_call(
        functools.partial(_down_post_kernel, res_w=res_w),
        grid=(M // tm,),
        in_specs=[
            pl.BlockSpec((tm, K), lambda i: (i, 0)),
            pl.BlockSpec((K, D), lambda i: (0, 0), pipeline_mode=pl.Buffered(1)),
            pl.BlockSpec((tm, D), lambda i: (i, 0)),
            pl.BlockSpec((1, D), lambda i: (0, 0)),
            pl.BlockSpec((1, 1, D), lambda i: (i // per_b, 0, 0)),
        ],
        out_specs=pl.BlockSpec((tm, D), lambda i: (i, 0)),
        out_shape=jax.ShapeDtypeStruct((M, D), F32),
        compiler_params=_params("parallel"),
        name="down_post",
    )(h, wd.astype(BF16), x2, g.reshape(1, D), gate.reshape(B, 1, D))


def _merge_kernel(u_ref, g0_ref, g1_ref, g2_ref, g3_ref, y0_ref, y1_ref, y2_ref, y3_ref, wb_ref, o_ref,
                  wg_bf, wb_bf):
    @pl.when(pl.program_id(1) == 0)
    def _():
        for i, g_ref in enumerate((g0_ref, g1_ref, g2_ref, g3_ref)):
            wg_bf[i] = g_ref[...].astype(BF16)
        wb_bf[...] = wb_ref[...].astype(BF16)

    u = u_ref[...]
    acc = None
    for i, y_ref in enumerate((y0_ref, y1_ref, y2_ref, y3_ref)):
        gate = jax.nn.sigmoid(_dot(u, wg_bf[i]))
        term = gate * _dot(y_ref[...], wb_bf[i])
        acc = term if acc is None else acc + term
    o_ref[...] = acc.astype(o_ref.dtype)


def _merge(u, w_gate, ys, w_branch, tm=512, tn=256):
    M, D = u.shape
    W = w_branch.shape[1]
    tm, tn = _tile(M, tm), _tile(D, tn)
    per_branch = D // tn
    y_spec = pl.BlockSpec((tm, W), lambda j, i: (i, 0))
    gate_specs = [pl.BlockSpec((D, tn), lambda j, i, b=b: (0, b * per_branch + j)) for b in range(4)]
    return pl.pallas_call(
        _merge_kernel,
        grid=(D // tn, M // tm),
        in_specs=[pl.BlockSpec((tm, D), lambda j, i: (i, 0))] + gate_specs + [y_spec] * 4 + [
            pl.BlockSpec((4, W, tn), lambda j, i: (0, 0, j))],
        out_specs=pl.BlockSpec((tm, tn), lambda j, i: (i, j)),
        out_shape=jax.ShapeDtypeStruct((M, D), BF16),
        scratch_shapes=[pltpu.VMEM((4, D, tn), BF16), pltpu.VMEM((4, W, tn), BF16)],
        compiler_params=_params("parallel", "arbitrary"),
        name="merge",
    )(u, w_gate, w_gate, w_gate, w_gate, *ys, w_branch)


def _flash_kernel(*refs, mode, t, G):
    if mode == "select":
        q_ref, k_ref, v_ref, sel_ref, o_ref, m_ref, l_ref, acc_ref = refs
    else:
        q_ref, k_ref, v_ref, o_ref, m_ref, l_ref, acc_ref = refs
    i = pl.program_id(2)
    cols = G * t
    q = q_ref[0, 0, 0]
    k_off = lax.broadcasted_iota(jnp.int32, (t, cols), 0)
    q_off = lax.broadcasted_iota(jnp.int32, (t, cols), 1) & (t - 1)
    blocks_per_tile = t // SLC_BLOCK

    def scores(j, mask):
        start = pl.multiple_of(j * t, t)
        s = _dot(k_ref[0, 0, pl.ds(start, t), :], q)
        if mode == "select":
            parts = []
            for kb in range(blocks_per_tile):
                picked = sel_ref[0, 0, pl.ds(j * blocks_per_tile + kb, 1), :]
                picked = jnp.concatenate([picked] * G, axis=1) if G > 1 else picked
                parts.append(jnp.where(picked > 0.5, s[kb * SLC_BLOCK:(kb + 1) * SLC_BLOCK], NEG_INF))
            s = jnp.concatenate(parts, axis=0)
        if mask is not None:
            s = jnp.where(mask, s, NEG_INF)
        return s, v_ref[0, 0, j]

    s, v = scores(i, k_off <= q_off)
    m = jnp.max(s, axis=0, keepdims=True)
    p = jnp.exp(s - m)
    m_ref[...] = m
    l_ref[...] = jnp.sum(p, axis=0, keepdims=True)
    acc_ref[...] = _dot(v, p.astype(BF16))

    def accumulate(j, mask=None):
        s, v = scores(j, mask)
        m_prev = m_ref[...]
        m_new = jnp.maximum(m_prev, jnp.max(s, axis=0, keepdims=True))
        p = jnp.exp(s - m_new)
        alpha = jnp.exp(m_prev - m_new)
        l_ref[...] = alpha * l_ref[...] + jnp.sum(p, axis=0, keepdims=True)
        acc_ref[...] = alpha * acc_ref[...] + _dot(v, p.astype(BF16))
        m_ref[...] = m_new

    if mode == "window":
        n_back = WIN_SIZE // t
        for back in range(1, n_back + 1):
            mask = (k_off > q_off) if back == n_back else None

            @pl.when(i >= back)
            def _(back=back, mask=mask):
                accumulate(i - back, mask)
    else:
        def body(j, carry):
            accumulate(j)
            return carry

        lax.fori_loop(0, i, body, 0)

    o_ref[0, 0, 0] = (acc_ref[...] * (1.0 / l_ref[...])).astype(o_ref.dtype)


def _flash(q, k, v, *, mode, sel=None, out_dtype=F32):
    B, Hk, nq, Dq, cols = q.shape
    S = k.shape[2]
    Dv, t = v.shape[3], v.shape[4]
    G = cols // t
    assert t & (t - 1) == 0 and t % SLC_BLOCK == 0 and nq * t == S
    if mode == "window":
        assert WIN_SIZE % t == 0
    in_specs = [
        pl.BlockSpec((1, 1, 1, Dq, cols), lambda b, h, i: (b, h, i, 0, 0)),
        pl.BlockSpec((1, 1, S, Dq), lambda b, h, i: (b, h, 0, 0)),
        pl.BlockSpec((1, 1, nq, Dv, t), lambda b, h, i: (b, h, 0, 0, 0)),
    ]
    args = [q, k, v]
    if mode == "select":
        in_specs.append(pl.BlockSpec((1, 1, sel.shape[2], t), lambda b, h, i: (b, h, 0, i)))
        args.append(sel)
    return pl.pallas_call(
        functools.partial(_flash_kernel, mode=mode, t=t, G=G),
        grid=(B, Hk, nq),
        in_specs=in_specs,
        out_specs=pl.BlockSpec((1, 1, 1, Dv, cols), lambda b, h, i: (b, h, i, 0, 0)),
        out_shape=jax.ShapeDtypeStruct((B, Hk, nq, Dv, cols), out_dtype),
        scratch_shapes=[pltpu.VMEM((1, cols), F32), pltpu.VMEM((1, cols), F32), pltpu.VMEM((Dv, cols), F32)],
        compiler_params=_params("parallel", "parallel", "arbitrary"),
        name="attn_" + mode,
    )(*args)


def _nsa_compress_kernel(t_ref, pos_ref, w1_ref, w2_ref, o_ref):
    n_pieces, half = t_ref.shape[-2], t_ref.shape[-1]
    for which in range(2):
        t = t_ref[which, 0, 0]
        w1 = w1_ref[which]
        first = _dot(t, w1[:half], HI)
        second = _dot(t, w1[half:], HI)
        pos_bias = _dot(pos_ref[which], w1, HI)
        pre = first + pltpu.roll(second, n_pieces - 1, 0) + pos_bias
        o_ref[which, 0, 0] = _dot(jax.nn.gelu(pre), w2_ref[which], HI)


def _nsa_compress(kv_flat, pos, w1, w2):
    _, B, Hk, n_pieces, half = kv_flat.shape
    hid, d = w2.shape[1], w2.shape[2]
    return pl.pallas_call(
        _nsa_compress_kernel,
        grid=(B, Hk),
        in_specs=[
            pl.BlockSpec((2, 1, 1, n_pieces, half), lambda b, h: (0, b, h, 0, 0)),
            pl.BlockSpec((2, 1, 2 * half), lambda b, h: (0, 0, 0)),
            pl.BlockSpec((2, 2 * half, hid), lambda b, h: (0, 0, 0)),
            pl.BlockSpec((2, hid, d), lambda b, h: (0, 0, 0)),
        ],
        out_specs=pl.BlockSpec((2, 1, 1, n_pieces, d), lambda b, h: (0, b, h, 0, 0)),
        out_shape=jax.ShapeDtypeStruct((2, B, Hk, n_pieces, d), F32),
        compiler_params=_params("parallel", "parallel"),
        name="nsa_compress",
    )(kv_flat, pos, w1, w2)


def _nsa_cmp_kernel(q_ref, kc_ref, vct_ref, ovt_ref, o_ref, sel_ref, *, t, G):
    i = pl.program_id(2)
    cols = G * t
    n_cmp = kc_ref.shape[-2]
    n_sel = ovt_ref.shape[0]
    s = _dot(kc_ref[0, 0], q_ref[0, 0, 0].astype(F32), HI)
    t_pos = i * t + (lax.broadcasted_iota(jnp.int32, (n_cmp, cols), 1) & (t - 1))
    blk_end = lax.broadcasted_iota(jnp.int32, (n_cmp, cols), 0) * CMP_STRIDE + (CMP_BLOCK - 1)
    mask = blk_end <= t_pos
    s = jnp.where(mask, s, NEG_INF)
    e = jnp.exp(s - jnp.max(s, axis=0, keepdims=True))
    p = jnp.where(mask, e * (1.0 / jnp.sum(e, axis=0, keepdims=True)), 0.0)
    o_ref[0, 0, 0] = _dot(vct_ref[0, 0], p, HI)

    p_group = p[:, 0:t]
    for g in range(1, G):
        p_group = p_group + p[:, g * t:(g + 1) * t]
    imp = _dot(ovt_ref[...], p_group, HI)
    blk = lax.broadcasted_iota(jnp.int32, (n_sel, t), 0)
    cur = jnp.right_shift(i * t + lax.broadcasted_iota(jnp.int32, (n_sel, t), 1), SLC_SHIFT)
    imp = jnp.where(blk <= cur, jnp.where(blk == 0, FORCE_SCORE, jnp.where(blk >= cur - 1, FORCE_SCORE, imp)),
                    NEG_INF)
    rank = jnp.zeros((n_sel, t), F32)
    for c in range(n_sel):
        other = imp[c:c + 1, :]
        later = jnp.where(blk > c, 1.0, 0.0)
        rank = rank + jnp.where(other > imp, 1.0, 0.0) + jnp.where(other == imp, later, 0.0)
    sel_ref[0, 0] = jnp.where(rank < float(min(SLC_TOPK, n_sel)), 1.0, 0.0)


def _nsa_cmp(q, kc, vct, overlap_t):
    B, Hk, nq, d, cols = q.shape
    n_cmp = kc.shape[2]
    n_sel = overlap_t.shape[0]
    t = n_sel * SLC_BLOCK // nq
    G = cols // t
    o_spec = pl.BlockSpec((1, 1, 1, d, cols), lambda b, h, i: (b, h, i, 0, 0))
    return pl.pallas_call(
        functools.partial(_nsa_cmp_kernel, t=t, G=G),
        grid=(B, Hk, nq),
        in_specs=[
            o_spec,
            pl.BlockSpec((1, 1, n_cmp, d), lambda b, h, i: (b, h, 0, 0)),
            pl.BlockSpec((1, 1, d, n_cmp), lambda b, h, i: (b, h, 0, 0)),
            pl.BlockSpec((n_sel, n_cmp), lambda b, h, i: (0, 0)),
        ],
        out_specs=[o_spec, pl.BlockSpec((1, 1, n_sel, t), lambda b, h, i: (b, h, 0, i))],
        out_shape=[
            jax.ShapeDtypeStruct((B, Hk, nq, d, cols), F32),
            jax.ShapeDtypeStruct((B, Hk, n_sel, nq * t), F32),
        ],
        compiler_params=_params("parallel", "parallel", "parallel"),
        name="nsa_cmp",
    )(q, kc, vct, overlap_t)


def _nsa_combine_kernel(gl_ref, oc_ref, os_ref, ow_ref, o_ref):
    g = jax.nn.sigmoid(gl_ref[0, 0, 0])
    o_ref[0, 0, 0] = (g[0:1] * oc_ref[0, 0, 0] + g[1:2] * os_ref[0, 0, 0]
                      + g[2:3] * ow_ref[0, 0, 0]).astype(o_ref.dtype)


def _nsa_combine(gl, o_cmp, o_slc, o_win):
    B, Hk, nq, d, cols = o_cmp.shape
    o_spec = pl.BlockSpec((1, 1, 1, d, cols), lambda b, h, i: (b, h, i, 0, 0))
    return pl.pallas_call(
        _nsa_combine_kernel,
        grid=(B, Hk, nq),
        in_specs=[pl.BlockSpec((1, 1, 1, 3, cols), lambda b, h, i: (b, h, i, 0, 0)), o_spec, o_spec, o_spec],
        out_specs=o_spec,
        out_shape=jax.ShapeDtypeStruct(o_cmp.shape, BF16),
        compiler_params=_params("parallel", "parallel", "parallel"),
        name="nsa_combine",
    )(gl, o_cmp, o_slc, o_win)


def _rwkv_prep_kernel(p_ref, mu_ref, w0_ref, ww2_ref, a0_ref, aw2_ref, gw2_ref, kk_ref, ka_ref, seg_ref,
                      r_o, lw_o, k_o, v_o, kk_o, b_o, g_o, last_ref):
    tb = p_ref.shape[1]
    width = r_o.shape[-1]

    @pl.when(pl.program_id(1) == 0)
    def _():
        last_ref[...] = jnp.zeros_like(last_ref)

    p = p_ref[0]
    row = lax.broadcasted_iota(jnp.int32, p.shape, 0)
    prev = jnp.where(row == 0, last_ref[...], pltpu.roll(p, 1, 0))
    last_ref[...] = p[tb - 1:tb]
    xs = p + (prev - p) * mu_ref[...]
    r = xs[:, 0:width]
    k = xs[:, width:2 * width]
    v = xs[:, 2 * width:3 * width]
    lora = xs[:, 3 * width:3 * width + 128]
    g_lo = xs[:, 3 * width + 128:]
    lw = -math.exp(-0.5) * jax.nn.sigmoid(w0_ref[...] + _dot(jnp.tanh(lora), ww2_ref[...], HI))
    a = jax.nn.sigmoid(a0_ref[...] + _dot(lora, aw2_ref[...], HI))
    g = _dot(jax.nn.sigmoid(g_lo), gw2_ref[...], HI)
    kk = k * kk_ref[...]
    norm = jnp.sqrt(_dot(kk * kk, seg_ref[...], HI))
    kk = kk / jnp.maximum(norm, 1e-12)
    r_o[0] = r
    lw_o[0] = lw
    k_o[0] = k * (1.0 + (a - 1.0) * ka_ref[...])
    v_o[0] = v
    kk_o[0] = kk
    b_o[0] = kk * a
    g_o[0] = g


def _rwkv_prep(p, consts, tb=256):
    B, S, P = p.shape
    width = consts[1].shape[-1]
    tb = _tile(S, tb)
    const_specs = [pl.BlockSpec(c.shape, lambda b, t: (0, 0)) for c in consts]
    o_spec = pl.BlockSpec((1, tb, width), lambda b, t: (b, t, 0))
    return pl.pallas_call(
        _rwkv_prep_kernel,
        grid=(B, S // tb),
        in_specs=[pl.BlockSpec((1, tb, P), lambda b, t: (b, t, 0))] + const_specs,
        out_specs=[o_spec] * 7,
        out_shape=[jax.ShapeDtypeStruct((B, S, width), F32)] * 7,
        scratch_shapes=[pltpu.VMEM((1, P), F32)],
        compiler_params=_params("parallel", "arbitrary"),
        name="rwkv_prep",
    )(p, *consts)


def _cumsum_rows(x):
    n = x.shape[0]
    row = lax.broadcasted_iota(jnp.int32, x.shape, 0)
    step = 1
    while step < n:
        x = x + jnp.where(row >= step, pltpu.roll(x, step, 0), 0.0)
        step *= 2
    return x


def _seg_sum(x, seg):
    rows = x.shape[0]
    hi = x.astype(BF16)
    lo = (x - hi.astype(F32)).astype(BF16)
    both = _dot(jnp.concatenate([hi, lo], axis=0), seg)
    return both[:rows] + both[rows:]


def _rwkv_group_chunk(rg, kkg, bd, kd, kc, bc, v, gamma, state, lane_masks, strict, incl, eye, chunk):
    stack = lambda x: jnp.concatenate([x * m for m in lane_masks], axis=0)
    kks, rs, vs = stack(kkg), stack(rg), stack(v)
    n, w = kks.shape
    big = _dot_nt(jnp.concatenate([kks, rs], axis=0), jnp.concatenate([stack(bd), stack(kd)], axis=0))
    a_mat = jnp.where(strict, big[:n, :n], 0.0)
    bk = jnp.where(strict, big[:n, n:], 0.0)
    ay = jnp.where(incl, big[n:, :n], 0.0)
    by = jnp.where(incl, big[n:, n:], 0.0)
    power = -a_mat
    inv = eye + power
    power = _dot(power, power)
    for _ in range(int(math.log2(chunk)) - 2):
        both = _dot(jnp.concatenate([power, inv], axis=0), power)
        power, inv = both[:n], inv + both[n:]
    inv = inv + _dot(inv, power)
    bv = _dot(jnp.concatenate([bk, by], axis=0), vs)
    tx = _dot(inv, jnp.concatenate([kks, bv[:n]], axis=1))
    ayx = _dot(ay, tx)
    q_hat = rs - ayx[:, :w]
    y_intra = bv[n:] - ayx[:, w:]
    qt = _dot_nt(jnp.concatenate([q_hat, tx[:, :w]], axis=0), state)
    ys = qt[:n] + y_intra
    us = qt[n:] + tx[:, w:]
    new_state = state * gamma + _dot_tn(jnp.concatenate([vs, us], axis=0),
                                        jnp.concatenate([stack(kc), -stack(bc)], axis=0))
    y = ys[:chunk]
    for h in range(1, len(lane_masks)):
        y = y + ys[h * chunk:(h + 1) * chunk]
    return y, new_state


def _rwkv_scan_kernel(r_ref, lw_ref, k_ref, v_ref, kk_ref, b_ref, g_ref, lnw_ref, lnb_ref, rk_ref, seg_ref,
                      o_ref, state_ref, *, heads, chunk, group):
    tb, width = r_ref.shape[1], r_ref.shape[2]
    n = width // heads
    groups, gw, gn = heads // group, group * n, group * chunk

    @pl.when(pl.program_id(1) == 0)
    def _():
        state_ref[...] = jnp.zeros_like(state_ref)

    lane = lax.broadcasted_iota(jnp.int32, (1, gw), 1)
    lane_masks = [jnp.where(jnp.logical_and(lane >= h * n, lane < (h + 1) * n), 1.0, 0.0) for h in range(group)]
    row_id = lax.broadcasted_iota(jnp.int32, (gn, gn), 0)
    col_id = lax.broadcasted_iota(jnp.int32, (gn, gn), 1)
    row, col = row_id & (chunk - 1), col_id & (chunk - 1)
    strict, incl = col < row, col <= row
    eye = jnp.where(row_id == col_id, 1.0, 0.0)
    states = [state_ref[p] for p in range(groups)]
    y_chunks = []
    for c in range(tb // chunk):
        rows = pl.ds(c * chunk, chunk)
        r, lw, k, v = r_ref[0, rows, :], lw_ref[0, rows, :], k_ref[0, rows, :], v_ref[0, rows, :]
        kk, bb = kk_ref[0, rows, :], b_ref[0, rows, :]
        g = _cumsum_rows(lw)
        g_last = g[chunk - 1:chunk]
        decay_out = jnp.exp(-g)
        decay_end = jnp.exp(g_last - g)
        rg, kkg = r * jnp.exp(g), kk * jnp.exp(g - lw)
        bd, kd = bb * decay_out, k * decay_out
        kc, bc = k * decay_end, bb * decay_end
        gamma = jnp.exp(g_last)
        ys = []
        for p in range(groups):
            sl = slice(p * gw, (p + 1) * gw)
            y, states[p] = _rwkv_group_chunk(rg[:, sl], kkg[:, sl], bd[:, sl], kd[:, sl], kc[:, sl], bc[:, sl],
                                             v[:, sl], gamma[:, sl], states[p], lane_masks, strict, incl, eye, chunk)
            ys.append(y)
        y_chunks.append(jnp.concatenate(ys, axis=1))
    for p in range(groups):
        state_ref[p] = states[p]
    y = jnp.concatenate(y_chunks, axis=0)
    seg = seg_ref[...]
    r, k, v = r_ref[0], k_ref[0], v_ref[0]
    sums = _seg_sum(jnp.concatenate([y, r * k * rk_ref[...]], axis=0), seg)
    dev = y - sums[:tb] * (1.0 / n)
    var = _seg_sum(dev * dev, seg) * (1.0 / n)
    yn = dev * lax.rsqrt(var + RWKV_GN_EPS) * lnw_ref[...] + lnb_ref[...]
    o_ref[0] = ((yn + sums[tb:] * v) * g_ref[0]).astype(o_ref.dtype)


def _rwkv_scan(r, lw, k, v, kk, bb, g, ln_w, ln_b, r_k, seg, tb=256, group=4):
    B, S, width = r.shape
    heads = RWKV_HEADS
    n = width // heads
    chunk = min(RWKV_CHUNK, S)
    assert chunk == n and heads % group == 0
    tb = _tile(S, tb)
    x_spec = pl.BlockSpec((1, tb, width), lambda b, t: (b, t, 0))
    vec_spec = pl.BlockSpec((1, width), lambda b, t: (0, 0))
    return pl.pallas_call(
        functools.partial(_rwkv_scan_kernel, heads=heads, chunk=chunk, group=group),
        grid=(B, S // tb),
        in_specs=[x_spec] * 7 + [vec_spec] * 3 + [pl.BlockSpec((width, width), lambda b, t: (0, 0))],
        out_specs=x_spec,
        out_shape=jax.ShapeDtypeStruct((B, S, width), BF16),
        scratch_shapes=[pltpu.VMEM((heads // group, group * n, group * n), F32)],
        compiler_params=_params("parallel", "arbitrary"),
        name="rwkv_scan",
    )(r, lw, k, v, kk, bb, g, ln_w, ln_b, r_k, seg.astype(BF16))


def _gla_kernel(q_ref, k_ref, v_ref, r_ref, al_ref, aw_ref, ab_ref, ng_ref, o_ref, state_ref,
                *, heads, chunk):
    tb = v_ref.shape[1]
    dk = q_ref.shape[-1]
    dv = v_ref.shape[-1] // heads

    @pl.when(pl.program_id(1) == 0)
    def _():
        state_ref[...] = jnp.zeros_like(state_ref)

    row = lax.broadcasted_iota(jnp.int32, (chunk, chunk), 0)
    col = lax.broadcasted_iota(jnp.int32, (chunk, chunk), 1)
    incl = col <= row
    states = [state_ref[h] for h in range(heads)]
    for c in range(tb // chunk):
        rows = pl.ds(c * chunk, chunk)
        alpha_lo = al_ref[0, rows, :]
        outs = []
        for h in range(heads):
            cols = pl.ds(h * dv, dv)
            x = _dot(alpha_lo, aw_ref[h], HI) + ab_ref[h]
            log_a = (jnp.minimum(x, 0.0) - jnp.log(1.0 + jnp.exp(-jnp.abs(x)))) * (1.0 / GLA_TAU)
            b = _cumsum_rows(log_a)
            b_last = b[chunk - 1:chunk]
            q = q_ref[0, h, rows, :]
            k = k_ref[0, h, rows, :]
            v = v_ref[0, rows, cols]
            qe = q * (dk ** -0.5) * jnp.exp(b)
            attn = jnp.where(incl, _dot_nt(qe, k * jnp.exp(-b)), 0.0)
            o = _dot(attn, v) + _dot_nt(qe, states[h])
            states[h] = states[h] * jnp.exp(b_last) + _dot_tn(v, k * jnp.exp(b_last - b))
            outs.append(o * lax.rsqrt(jnp.mean(o * o, axis=-1, keepdims=True) + NORM_EPS))
        gate = r_ref[0, rows, :]
        o_ref[0, rows, :] = (jnp.concatenate(outs, axis=1) * ng_ref[...]
                             * (gate * jax.nn.sigmoid(gate))).astype(o_ref.dtype)
    for h in range(heads):
        state_ref[h] = states[h]


def _gla(q, k, v, r, alpha_lo, aw, ab, ng, tb=256):
    B, H, S, dk = q.shape
    width = v.shape[-1]
    lora = alpha_lo.shape[-1]
    chunk = min(GLA_CHUNK, S)
    tb = _tile(S, tb)
    qk_spec = pl.BlockSpec((1, H, tb, dk), lambda b, t: (b, 0, t, 0))
    x_spec = pl.BlockSpec((1, tb, width), lambda b, t: (b, t, 0))
    return pl.pallas_call(
        functools.partial(_gla_kernel, heads=H, chunk=chunk),
        grid=(B, S // tb),
        in_specs=[
            qk_spec, qk_spec, x_spec, x_spec,
            pl.BlockSpec((1, tb, lora), lambda b, t: (b, t, 0)),
            pl.BlockSpec(aw.shape, lambda b, t: (0, 0, 0)),
            pl.BlockSpec(ab.shape, lambda b, t: (0, 0, 0)),
            pl.BlockSpec(ng.shape, lambda b, t: (0, 0)),
        ],
        out_specs=x_spec,
        out_shape=jax.ShapeDtypeStruct((B, S, width), BF16),
        scratch_shapes=[pltpu.VMEM((H, width // H, dk), F32)],
        compiler_params=_params("parallel", "arbitrary"),
        name="gla",
    )(q, k, v, r, alpha_lo, aw, ab, ng)


def _rope_tables(S, d):
    inv = ROPE_THETA ** (-jnp.arange(0, d, 2, dtype=F32) / d)
    ang = jnp.arange(S).astype(F32)[:, None] * inv[None, :]
    cos, sin = jnp.cos(ang), jnp.sin(ang)
    return jnp.concatenate([cos, cos], axis=-1), jnp.concatenate([sin, sin], axis=-1)


def _rot_cols(w, d):
    K = w.shape[0]
    w = w.reshape(K, -1, 2, d // 2)
    return jnp.concatenate([-w[:, :, 1], w[:, :, 0]], axis=-1).reshape(K, -1)


def _nsa_constants(S):
    n_pieces = S // CMP_STRIDE
    n_cmp = (S - CMP_BLOCK) // CMP_STRIDE + 1
    n_sel = S // SLC_BLOCK
    start = np.arange(n_cmp)[:, None] * CMP_STRIDE
    end = start + CMP_BLOCK - 1
    blk = np.arange(n_sel)[None, :]
    overlap = np.zeros((n_pieces, n_sel), np.float32)
    overlap[:n_cmp] = ((start < (blk + 1) * SLC_BLOCK) & (end >= blk * SLC_BLOCK)).astype(np.float32)
    return jnp.asarray(overlap.T)


def _token_mixing(u, B, S, w_in, w_branch, cmp_pos, cmp_w1, cmp_w2, rwkv_mu, rwkv_w0, rwkv_w_w2, rwkv_a0,
                  rwkv_a_w2, rwkv_g_w2, rwkv_k_k, rwkv_k_a, rwkv_r_k, rwkv_ln_w, rwkv_ln_b, gla_alpha_w2,
                  gla_alpha_b, gla_norm_g, mla_q_norm, mla_w_uq, mla_kv_norm, mla_w_ukv):
    M, D = u.shape
    d = HEAD_DIM
    width = D // 4
    sizes = [width + 6 * NSA_KV_HEADS * d + 3 * NSA_HEADS, 3 * width + 256, 2 * (width // 2) + 2 * width + 16,
             384 + 128 + MLA_ROPE, 4 * D]
    offs = np.cumsum([0] + sizes)
    assert offs[-1] == w_in.shape[1]
    w_nsa, w_rwkv, w_gla, w_mla, w_gate = [w_in[:, offs[i]:offs[i + 1]] for i in range(5)]
    zeros = lambda n: jnp.zeros((D, n), F32)
    kvw = NSA_KV_HEADS * d

    wq = w_nsa[:, :width]
    wkc, wvc, wks, wvs, wkw, wvw = [w_nsa[:, width + i * kvw:width + (i + 1) * kvw] for i in range(6)]
    wgl = w_nsa[:, width + 6 * kvw:]
    wkr = w_mla[:, 512:512 + MLA_ROPE]
    n_rope = width + 3 * kvw + MLA_ROPE
    pad = (-n_rope) % 512
    w_a = jnp.concatenate([wq, wkc, wks, wkw, wkr, zeros(pad)], axis=1)
    w_b = jnp.concatenate([_rot_cols(wq, d), _rot_cols(wkc, d), _rot_cols(wks, d), _rot_cols(wkw, d),
                           _rot_cols(wkr, MLA_ROPE), zeros(pad)], axis=1)
    cos_d, sin_d = _rope_tables(S, d)
    cos_r, sin_r = _rope_tables(S, MLA_ROPE)
    reps = (width + 3 * kvw) // d
    cos_t = jnp.concatenate([jnp.tile(cos_d, (1, reps)), cos_r, jnp.ones((S, pad), F32)], axis=1)
    sin_t = jnp.concatenate([jnp.tile(sin_d, (1, reps)), sin_r, jnp.zeros((S, pad), F32)], axis=1)
    roped = _matmul(u, w_a, S=S, w_rot=w_b, cos=cos_t, sin=sin_t, name="proj_rope").reshape(B, S, -1)
    n_plain = 3 * kvw + 3 * NSA_HEADS
    plain = _matmul(u, jnp.concatenate([wvc, wvs, wvw, wgl, zeros((-n_plain) % 128)], axis=1), S=S,
                    name="proj_nsa_v")

    Hk, G = NSA_KV_HEADS, NSA_GROUP
    t_att = _tile(S, 256)
    nq = S // t_att
    cols_layout = lambda x, f: x.reshape(B, nq, t_att, Hk, G, f).transpose(0, 3, 1, 5, 4, 2).reshape(
        B, Hk, nq, f, G * t_att)
    kv_heads = lambda x: x.reshape(B, S, Hk, d).transpose(0, 2, 1, 3)
    v_tiles = lambda x: x.astype(BF16).reshape(B, nq, t_att, Hk, d).transpose(0, 3, 1, 4, 2)
    q_att = cols_layout((roped[..., :width] * d ** -0.5).astype(BF16), d)
    kc_in = kv_heads(roped[..., width:width + kvw])
    ks = kv_heads(roped[..., width + kvw:width + 2 * kvw].astype(BF16))
    kw = kv_heads(roped[..., width + 2 * kvw:width + 3 * kvw].astype(BF16))
    k_pe = roped[..., width + 3 * kvw:width + 3 * kvw + MLA_ROPE]
    plain3 = plain.reshape(B, S, -1)
    vc_in = kv_heads(plain3[..., :kvw])
    vs, vw = v_tiles(plain3[..., kvw:2 * kvw]), v_tiles(plain3[..., 2 * kvw:3 * kvw])
    gate_logits = cols_layout(plain3[..., 3 * kvw:3 * kvw + 3 * NSA_HEADS], 3)

    overlap_t = _nsa_constants(S)
    n_pieces = S // CMP_STRIDE
    kv_flat = jnp.stack([kc_in, vc_in]).reshape(2, B, Hk, n_pieces, CMP_STRIDE * d)
    cmp = _nsa_compress(kv_flat, cmp_pos.reshape(2, 1, CMP_BLOCK * d), cmp_w1, cmp_w2)
    o_cmp, sel = _nsa_cmp(q_att, cmp[0], cmp[1].transpose(0, 1, 3, 2), overlap_t)
    o_slc = _flash(q_att, ks, vs, mode="select", sel=sel)
    o_win = _flash(q_att, kw, vw, mode="window")
    y_nsa = _nsa_combine(gate_logits, o_cmp, o_slc, o_win)
    y_nsa = y_nsa.reshape(B, Hk, nq, d, G, t_att).transpose(0, 2, 5, 1, 4, 3).reshape(M, width)

    regroup = lambda t: jnp.concatenate(
        [t[..., 0:width], t[..., width + 64:2 * width + 64], t[..., 2 * width + 64:3 * width + 64],
         t[..., width:width + 64], t[..., 3 * width + 64:3 * width + 128], t[..., 3 * width + 128:]], axis=-1)
    p_rwkv = _matmul(u, regroup(w_rwkv), S=S, tn=896, name="proj_rwkv").reshape(B, S, -1)
    row = lambda t: t.reshape(1, -1)
    seg = jnp.asarray(np.kron(np.eye(RWKV_HEADS), np.ones((d, d))).astype(np.float32))
    lora_pad = jnp.zeros((64, width), F32)
    consts = [row(regroup(rwkv_mu)), row(rwkv_w0), jnp.concatenate([rwkv_w_w2, lora_pad], axis=0), row(rwkv_a0),
              jnp.concatenate([lora_pad, rwkv_a_w2], axis=0), rwkv_g_w2, row(rwkv_k_k), row(rwkv_k_a), seg]
    r, lw, k2, v, kk, bb, g = _rwkv_prep(p_rwkv, consts)
    y_rwkv = _rwkv_scan(r, lw, k2, v, kk, bb, g, row(rwkv_ln_w), row(rwkv_ln_b), row(rwkv_r_k), seg)
    y_rwkv = y_rwkv.reshape(M, width)

    dk_all = width // 2
    w_gla_p = jnp.concatenate([w_gla[:, :2 * dk_all + width], w_gla[:, 2 * dk_all + width + 16:],
                               w_gla[:, 2 * dk_all + width:2 * dk_all + width + 16], zeros(240)], axis=1)
    p_gla = _matmul(u, w_gla_p, S=S, tn=896, name="proj_gla").reshape(B, S, -1)
    dkh = dk_all // GLA_HEADS
    gla_heads = lambda t: t.reshape(B, S, GLA_HEADS, dkh).transpose(0, 2, 1, 3)
    y_gla = _gla(gla_heads(p_gla[..., :dk_all]), gla_heads(p_gla[..., dk_all:2 * dk_all]),
                 p_gla[..., 2 * dk_all:2 * dk_all + width], p_gla[..., 2 * dk_all + width:2 * dk_all + 2 * width],
                 p_gla[..., 2 * dk_all + 2 * width:2 * dk_all + 2 * width + 16],
                 gla_alpha_w2.reshape(16, GLA_HEADS, dkh).transpose(1, 0, 2), gla_alpha_b.reshape(GLA_HEADS, 1, dkh),
                 row(gla_norm_g)).reshape(M, width)

    c_q = _matmul(u, w_mla[:, :384], S=S, name="proj_mla_q")
    c_kv = _matmul(u, w_mla[:, 384:512], S=S, name="proj_mla_kv")
    H = MLA_HEADS
    wq3 = mla_w_uq.reshape(384, H, MLA_NOPE + MLA_ROPE)
    rp = wq3[:, :, MLA_NOPE:]
    half = MLA_ROPE // 2
    z = lambda n: jnp.zeros((384, H, n), F32)
    wq_a = jnp.concatenate([wq3, z(128 - MLA_NOPE - MLA_ROPE)], axis=-1).reshape(384, H * 128)
    wq_b = jnp.concatenate([z(MLA_NOPE), -rp[..., half:], rp[..., :half], z(128 - MLA_NOPE - MLA_ROPE)],
                           axis=-1).reshape(384, H * 128)
    tail = 128 - MLA_NOPE - MLA_ROPE
    cos_h = jnp.concatenate([jnp.ones((S, MLA_NOPE), F32), cos_r, jnp.ones((S, tail), F32)], axis=1)
    sin_h = jnp.concatenate([jnp.zeros((S, MLA_NOPE), F32), sin_r, jnp.zeros((S, tail), F32)], axis=1)
    q_m = _matmul(c_q, wq_a, S=S, g=mla_q_norm, w_rot=wq_b, cos=jnp.tile(cos_h, (1, H)), sin=jnp.tile(sin_h, (1, H)),
                  name="mla_q")
    t_mla = _tile(S, 512)
    nq_m = S // t_mla
    tiles_m = lambda x, f: x.reshape(B, nq_m, t_mla, H, f).transpose(0, 3, 1, 4, 2)
    q_m = tiles_m((q_m * (MLA_NOPE + MLA_ROPE) ** -0.5).astype(BF16), 128)
    kv_m = _matmul(c_kv, mla_w_ukv, S=S, g=mla_kv_norm, name="mla_kv").reshape(B, S, H, MLA_NOPE + width // H)
    k_m = jnp.concatenate([kv_m[..., :MLA_NOPE], jnp.broadcast_to(k_pe[:, :, None, :], (B, S, H, MLA_ROPE)),
                           jnp.zeros((B, S, H, tail), F32)], axis=-1).astype(BF16).transpose(0, 2, 1, 3)
    v_m = tiles_m(kv_m[..., MLA_NOPE:].astype(BF16), width // H)
    o_mla = _flash(q_m, k_m, v_m, mode="causal", out_dtype=BF16)
    y_mla = o_mla.transpose(0, 2, 4, 1, 3).reshape(M, width)

    return _merge(u, w_gate, (y_nsa, y_rwkv, y_gla, y_mla), w_branch)


def kernel(x, c, ada_w, ada_b, pre_g, post_g, ffn_wg, ffn_wu, ffn_wd, mix_w_in, mix_w_branch, mix_w_out, nsa_cmp_pos, nsa_cmp_w1, nsa_cmp_w2, rwkv_mu, rwkv_w0, rwkv_w_w2, rwkv_a0, rwkv_a_w2, rwkv_g_w2, rwkv_k_k, rwkv_k_a, rwkv_r_k, rwkv_ln_w, rwkv_ln_b, gla_alpha_w2, gla_alpha_b, gla_norm_g, mla_q_norm, mla_w_uq, mla_kv_norm, mla_w_ukv):
    B, S, D = x.shape
    M = B * S
    depth = ada_w.shape[0]
    mod = _modulation(c, ada_w, ada_b)
    x2 = x.reshape(M, D)
    for l in range(depth):
        for i in range(3):
            m = mod[3 * l + i]
            shift, scale, gate = m[:, :D], m[:, D:2 * D], m[:, 2 * D:]
            u = _prenorm(x2, pre_g[l, i], scale, shift, S)
            if i == 1:
                y = _token_mixing(
                    u, B, S, mix_w_in[l], mix_w_branch[l], nsa_cmp_pos[l], nsa_cmp_w1[l], nsa_cmp_w2[l], rwkv_mu[l],
                    rwkv_w0[l], rwkv_w_w2[l], rwkv_a0[l], rwkv_a_w2[l], rwkv_g_w2[l], rwkv_k_k[l], rwkv_k_a[l],
                    rwkv_r_k[l], rwkv_ln_w[l], rwkv_ln_b[l], gla_alpha_w2[l], gla_alpha_b[l], gla_norm_g[l],
                    mla_q_norm[l], mla_w_uq[l], mla_kv_norm[l], mla_w_ukv[l])
                x2 = _down_post(y, mix_w_out[l], x2, post_g[l, i], gate, 1.0, S)
            else:
                j = 0 if i == 0 else 1
                h = _ffn_up(u, ffn_wg[l, j], ffn_wu[l, j])
                x2 = _down_post(h, ffn_wd[l, j], x2, post_g[l, i], gate, 0.5, S)
    return x2.reshape(B, S, D)
```

```python
import functools
import math

import jax
import jax.numpy as jnp
import numpy as np
from jax import lax
from jax.experimental import pallas as pl
from jax.experimental.pallas import tpu as pltpu

F32 = jnp.float32
BF16 = jnp.bfloat16
HI = lax.Precision.HIGHEST

NORM_EPS = 1e-6
NEG_INF = -1e30
FORCE_SCORE = 1e30
ROPE_THETA = 10000.0
HEAD_DIM = 64

NSA_HEADS = 8
NSA_KV_HEADS = 2
NSA_GROUP = NSA_HEADS // NSA_KV_HEADS
CMP_BLOCK = 32
CMP_STRIDE = 16
SLC_BLOCK = 64
SLC_SHIFT = 6
SLC_TOPK = 16
WIN_SIZE = 512
RWKV_HEADS = 8
RWKV_GN_EPS = 64e-5
RWKV_CHUNK = 64
GLA_HEADS = 4
GLA_TAU = 16.0
GLA_CHUNK = 64
MLA_HEADS = 8
MLA_NOPE = 64
MLA_ROPE = 32

VMEM_LIMIT_BYTES = 56 * 1024 * 1024


def _params(*sem):
    return pltpu.CompilerParams(dimension_semantics=sem, vmem_limit_bytes=VMEM_LIMIT_BYTES)


def _dot(a, b, precision=None):
    return jnp.dot(a, b, preferred_element_type=F32, precision=precision)


def _dot_nt(a, b, precision=None):
    return lax.dot_general(a, b, (((1,), (1,)), ((), ())), preferred_element_type=F32, precision=precision)


def _dot_tn(a, b, precision=None):
    return lax.dot_general(a, b, (((0,), (0,)), ((), ())), preferred_element_type=F32, precision=precision)


def _rms(x):
    return x * lax.rsqrt(jnp.mean(x * x, axis=-1, keepdims=True) + NORM_EPS)


def _tile(n, t):
    t = min(n, t)
    assert n % t == 0, (n, t)
    return t


def _mod_kernel(c_ref, w_ref, b_ref, o_ref):
    c = c_ref[...]
    o_ref[0] = _dot(c * jax.nn.sigmoid(c), w_ref[0], HI) + b_ref[0]


def _modulation(c, ada_w, ada_b):
    B, D = c.shape
    n_sub = ada_w.shape[0] * ada_w.shape[1]
    w = ada_w.reshape(n_sub, D, 3 * D)
    b = ada_b.reshape(n_sub, 1, 3 * D)
    rows = 8
    cp = jnp.zeros((rows, D), F32).at[:B].set(c)
    tn = _tile(3 * D, 512)
    out = pl.pallas_call(
        _mod_kernel,
        grid=(n_sub, 3 * D // tn),
        in_specs=[
            pl.BlockSpec((rows, D), lambda s, j: (0, 0)),
            pl.BlockSpec((1, D, tn), lambda s, j: (s, 0, j)),
            pl.BlockSpec((1, 1, tn), lambda s, j: (s, 0, j)),
        ],
        out_specs=pl.BlockSpec((1, rows, tn), lambda s, j: (s, 0, j)),
        out_shape=jax.ShapeDtypeStruct((n_sub, rows, 3 * D), F32),
        compiler_params=_params("parallel", "parallel"),
        name="modulation",
    )(cp, w, b)
    return out[:, :B]


def _prenorm_kernel(x_ref, g_ref, sc_ref, sh_ref, o_ref):
    y = _rms(x_ref[...]) * g_ref[...]
    o_ref[...] = (y * (1.0 + sc_ref[0]) + sh_ref[0]).astype(o_ref.dtype)


def _prenorm(x2, g, scale, shift, S):
    M, D = x2.shape
    B = M // S
    tm = _tile(S, 512)
    per_b = S // tm
    return pl.pallas_call(
        _prenorm_kernel,
        grid=(M // tm,),
        in_specs=[
            pl.BlockSpec((tm, D), lambda i: (i, 0)),
            pl.BlockSpec((1, D), lambda i: (0, 0)),
            pl.BlockSpec((1, 1, D), lambda i: (i // per_b, 0, 0)),
            pl.BlockSpec((1, 1, D), lambda i: (i // per_b, 0, 0)),
        ],
        out_specs=pl.BlockSpec((tm, D), lambda i: (i, 0)),
        out_shape=jax.ShapeDtypeStruct((M, D), BF16),
        compiler_params=_params("parallel"),
        name="prenorm",
    )(x2, g.reshape(1, D), scale.reshape(B, 1, D), shift.reshape(B, 1, D))


def _mm_kernel(*refs, norm, rope):
    it = iter(refs)
    a_ref = next(it)
    g_ref = next(it) if norm else None
    w_ref = next(it)
    if rope:
        w2_ref, cos_ref, sin_ref = next(it), next(it), next(it)
    o_ref = next(it)
    a = a_ref[...]
    if norm:
        a = _rms(a.astype(F32)) * g_ref[...]
    a = a.astype(BF16)
    out = _dot(a, w_ref[...])
    if rope:
        out = out * cos_ref[...] + _dot(a, w2_ref[...]) * sin_ref[...]
    o_ref[...] = out.astype(o_ref.dtype)


def _matmul(a, w, *, S, g=None, w_rot=None, cos=None, sin=None, out_dtype=F32, tm=1024, tn=512, name="matmul"):
    M, K = a.shape
    N = w.shape[1]
    tm = _tile(S, tm)
    tn = _tile(N, tn)
    per_b = S // tm
    norm, rope = g is not None, w_rot is not None
    in_specs = [pl.BlockSpec((tm, K), lambda i, j: (i, 0))]
    args = [a]
    if norm:
        in_specs.append(pl.BlockSpec((1, K), lambda i, j: (0, 0)))
        args.append(g.reshape(1, K))
    in_specs.append(pl.BlockSpec((K, tn), lambda i, j: (0, j)))
    args.append(w.astype(BF16))
    if rope:
        in_specs += [
            pl.BlockSpec((K, tn), lambda i, j: (0, j)),
            pl.BlockSpec((tm, tn), lambda i, j: (i % per_b, j)),
            pl.BlockSpec((tm, tn), lambda i, j: (i % per_b, j)),
        ]
        args += [w_rot.astype(BF16), cos, sin]
    return pl.pallas_call(
        functools.partial(_mm_kernel, norm=norm, rope=rope),
        grid=(M // tm, N // tn),
        in_specs=in_specs,
        out_specs=pl.BlockSpec((tm, tn), lambda i, j: (i, j)),
        out_shape=jax.ShapeDtypeStruct((M, N), out_dtype),
        compiler_params=_params("parallel", "parallel"),
        name=name,
    )(*args)


def _ffn_up_kernel(u_ref, wg_ref, wu_ref, o_ref, wg_bf, wu_bf):
    @pl.when(pl.program_id(1) == 0)
    def _():
        wg_bf[...] = wg_ref[...].astype(BF16)
        wu_bf[...] = wu_ref[...].astype(BF16)

    u = u_ref[...]
    a = _dot(u, wg_bf[...])
    b = _dot(u, wu_bf[...])
    o_ref[...] = (a * jax.nn.sigmoid(a) * b).astype(o_ref.dtype)


def _ffn_up(u, wg, wu, tm=1024, tn=512):
    M, K = u.shape
    N = wg.shape[1]
    tm, tn = _tile(M, tm), _tile(N, tn)
    return pl.pallas_call(
        _ffn_up_kernel,
        grid=(N // tn, M // tm),
        in_specs=[
            pl.BlockSpec((tm, K), lambda j, i: (i, 0)),
            pl.BlockSpec((K, tn), lambda j, i: (0, j)),
            pl.BlockSpec((K, tn), lambda j, i: (0, j)),
        ],
        out_specs=pl.BlockSpec((tm, tn), lambda j, i: (i, j)),
        out_shape=jax.ShapeDtypeStruct((M, N), BF16),
        scratch_shapes=[pltpu.VMEM((K, tn), BF16), pltpu.VMEM((K, tn), BF16)],
        compiler_params=_params("parallel", "arbitrary"),
        name="ffn_up",
    )(u, wg, wu)


def _down_post_kernel(h_ref, w_ref, x_ref, g_ref, gate_ref, o_ref, *, res_w):
    y = _rms(_dot(h_ref[...], w_ref[...])) * g_ref[...]
    o_ref[...] = x_ref[...] + res_w * gate_ref[0] * y


def _down_post(h, wd, x2, g, gate, res_w, S, tm=256):
    M, K = h.shape
    D = wd.shape[1]
    B = M // S
    tm = _tile(S, tm)
    per_b = S // tm
    return pl.pallas_call(
        functools.partial(_down_post_kernel, res_w=res_w),
        grid=(M // tm,),
        in_specs=[
            pl.BlockSpec((tm, K), lambda i: (i, 0)),
            pl.BlockSpec((K, D), lambda i: (0, 0), pipeline_mode=pl.Buffered(1)),
            pl.BlockSpec((tm, D), lambda i: (i, 0)),
            pl.BlockSpec((1, D), lambda i: (0, 0)),
            pl.BlockSpec((1, 1, D), lambda i: (i // per_b, 0, 0)),
        ],
        out_specs=pl.BlockSpec((tm, D), lambda i: (i, 0)),
        out_shape=jax.ShapeDtypeStruct((M, D), F32),
        compiler_params=_params("parallel"),
        name="down_post",
    )(h, wd.astype(BF16), x2, g.reshape(1, D), gate.reshape(B, 1, D))


def _merge_kernel(u_ref, g0_ref, g1_ref, g2_ref, g3_ref, y0_ref, y1_ref, y2_ref, y3_ref, wb_ref, o_ref,
                  wg_bf, wb_bf):
    @pl.when(pl.program_id(1) == 0)
    def _():
        for i, g_ref in enumerate((g0_ref, g1_ref, g2_ref, g3_ref)):
            wg_bf[i] = g_ref[...].astype(BF16)
        wb_bf[...] = wb_ref[...].astype(BF16)

    u = u_ref[...]
    acc = None
    for i, y_ref in enumerate((y0_ref, y1_ref, y2_ref, y3_ref)):
        gate = jax.nn.sigmoid(_dot(u, wg_bf[i]))
        term = gate * _dot(y_ref[...], wb_bf[i])
        acc = term if acc is None else acc + term
    o_ref[...] = acc.astype(o_ref.dtype)


def _merge(u, w_gate, ys, w_branch, tm=512, tn=256):
    M, D = u.shape
    W = w_branch.shape[1]
    tm, tn = _tile(M, tm), _tile(D, tn)
    per_branch = D // tn
    y_spec = pl.BlockSpec((tm, W), lambda j, i: (i, 0))
    gate_specs = [pl.BlockSpec((D, tn), lambda j, i, b=b: (0, b * per_branch + j)) for b in range(4)]
    return pl.pallas_call(
        _merge_kernel,
        grid=(D // tn, M // tm),
        in_specs=[pl.BlockSpec((tm, D), lambda j, i: (i, 0))] + gate_specs + [y_spec] * 4 + [
            pl.BlockSpec((4, W, tn), lambda j, i: (0, 0, j))],
        out_specs=pl.BlockSpec((tm, tn), lambda j, i: (i, j)),
        out_shape=jax.ShapeDtypeStruct((M, D), BF16),
        scratch_shapes=[pltpu.VMEM((4, D, tn), BF16), pltpu.VMEM((4, W, tn), BF16)],
        compiler_params=_params("parallel", "arbitrary"),
        name="merge",
    )(u, w_gate, w_gate, w_gate, w_gate, *ys, w_branch)


def _flash_kernel(*refs, mode, t, G):
    if mode == "select":
        q_ref, k_ref, v_ref, sel_ref, o_ref, m_ref, l_ref, acc_ref = refs
    else:
        q_ref, k_ref, v_ref, o_ref, m_ref, l_ref, acc_ref = refs
    i = pl.program_id(2)
    cols = G * t
    q = q_ref[0, 0, 0]
    k_off = lax.broadcasted_iota(jnp.int32, (t, cols), 0)
    q_off = lax.broadcasted_iota(jnp.int32, (t, cols), 1) & (t - 1)
    blocks_per_tile = t // SLC_BLOCK

    def scores(j, mask):
        start = pl.multiple_of(j * t, t)
        s = _dot(k_ref[0, 0, pl.ds(start, t), :], q)
        if mode == "select":
            parts = []
            for kb in range(blocks_per_tile):
                picked = sel_ref[0, 0, pl.ds(j * blocks_per_tile + kb, 1), :]
                picked = jnp.concatenate([picked] * G, axis=1) if G > 1 else picked
                parts.append(jnp.where(picked > 0.5, s[kb * SLC_BLOCK:(kb + 1) * SLC_BLOCK], NEG_INF))
            s = jnp.concatenate(parts, axis=0)
        if mask is not None:
            s = jnp.where(mask, s, NEG_INF)
        return s, v_ref[0, 0, j]

    s, v = scores(i, k_off <= q_off)
    m = jnp.max(s, axis=0, keepdims=True)
    p = jnp.exp(s - m)
    m_ref[...] = m
    l_ref[...] = jnp.sum(p, axis=0, keepdims=True)
    acc_ref[...] = _dot(v, p.astype(BF16))

    def accumulate(tiles, mask=None):
        parts = [scores(j, mask) for j in tiles]
        s = jnp.concatenate([part[0] for part in parts], axis=0) if len(parts) > 1 else parts[0][0]
        v = jnp.concatenate([part[1] for part in parts], axis=1) if len(parts) > 1 else parts[0][1]
        m_prev = m_ref[...]
        m_new = jnp.maximum(m_prev, jnp.max(s, axis=0, keepdims=True))
        p = jnp.exp(s - m_new)
        alpha = jnp.exp(m_prev - m_new)
        l_ref[...] = alpha * l_ref[...] + jnp.sum(p, axis=0, keepdims=True)
        acc_ref[...] = alpha * acc_ref[...] + _dot(v, p.astype(BF16))
        m_ref[...] = m_new

    if mode == "window":
        n_back = WIN_SIZE // t
        for back in range(1, n_back + 1):
            mask = (k_off > q_off) if back == n_back else None

            @pl.when(i >= back)
            def _(back=back, mask=mask):
                accumulate([i - back], mask)
    else:
        def body(pair, carry):
            accumulate([2 * pair, 2 * pair + 1])
            return carry

        lax.fori_loop(0, jnp.right_shift(i, 1), body, 0)

        @pl.when((i & 1) == 1)
        def _():
            accumulate([i - 1])

    o_ref[0, 0, 0] = (acc_ref[...] * (1.0 / l_ref[...])).astype(o_ref.dtype)


def _flash(q, k, v, *, mode, sel=None, out_dtype=F32):
    B, Hk, nq, Dq, cols = q.shape
    S = k.shape[2]
    Dv, t = v.shape[3], v.shape[4]
    G = cols // t
    assert t & (t - 1) == 0 and t % SLC_BLOCK == 0 and nq * t == S
    if mode == "window":
        assert WIN_SIZE % t == 0
    in_specs = [
        pl.BlockSpec((1, 1, 1, Dq, cols), lambda b, h, i: (b, h, i, 0, 0)),
        pl.BlockSpec((1, 1, S, Dq), lambda b, h, i: (b, h, 0, 0)),
        pl.BlockSpec((1, 1, nq, Dv, t), lambda b, h, i: (b, h, 0, 0, 0)),
    ]
    args = [q, k, v]
    if mode == "select":
        in_specs.append(pl.BlockSpec((1, 1, sel.shape[2], t), lambda b, h, i: (b, h, 0, i)))
        args.append(sel)
    return pl.pallas_call(
        functools.partial(_flash_kernel, mode=mode, t=t, G=G),
        grid=(B, Hk, nq),
        in_specs=in_specs,
        out_specs=pl.BlockSpec((1, 1, 1, Dv, cols), lambda b, h, i: (b, h, i, 0, 0)),
        out_shape=jax.ShapeDtypeStruct((B, Hk, nq, Dv, cols), out_dtype),
        scratch_shapes=[pltpu.VMEM((1, cols), F32), pltpu.VMEM((1, cols), F32), pltpu.VMEM((Dv, cols), F32)],
        compiler_params=_params("parallel", "parallel", "arbitrary"),
        name="attn_" + mode,
    )(*args)


def _nsa_compress_kernel(t_ref, pos_ref, w1_ref, w2_ref, o_ref):
    n_pieces, half = t_ref.shape[-2], t_ref.shape[-1]
    for which in range(2):
        t = t_ref[which, 0, 0]
        w1 = w1_ref[which]
        first = _dot(t, w1[:half], HI)
        second = _dot(t, w1[half:], HI)
        pos_bias = _dot(pos_ref[which], w1, HI)
        pre = first + pltpu.roll(second, n_pieces - 1, 0) + pos_bias
        o_ref[which, 0, 0] = _dot(jax.nn.gelu(pre), w2_ref[which], HI)


def _nsa_compress(kv_flat, pos, w1, w2):
    _, B, Hk, n_pieces, half = kv_flat.shape
    hid, d = w2.shape[1], w2.shape[2]
    return pl.pallas_call(
        _nsa_compress_kernel,
        grid=(B, Hk),
        in_specs=[
            pl.BlockSpec((2, 1, 1, n_pieces, half), lambda b, h: (0, b, h, 0, 0)),
            pl.BlockSpec((2, 1, 2 * half), lambda b, h: (0, 0, 0)),
            pl.BlockSpec((2, 2 * half, hid), lambda b, h: (0, 0, 0)),
            pl.BlockSpec((2, hid, d), lambda b, h: (0, 0, 0)),
        ],
        out_specs=pl.BlockSpec((2, 1, 1, n_pieces, d), lambda b, h: (0, b, h, 0, 0)),
        out_shape=jax.ShapeDtypeStruct((2, B, Hk, n_pieces, d), F32),
        compiler_params=_params("parallel", "parallel"),
        name="nsa_compress",
    )(kv_flat, pos, w1, w2)


def _nsa_cmp_kernel(q_ref, kc_ref, vct_ref, ovt_ref, o_ref, sel_ref, *, t, G):
    i = pl.program_id(2)
    cols = G * t
    n_cmp = kc_ref.shape[-2]
    n_sel = ovt_ref.shape[0]
    s = _dot(kc_ref[0, 0], q_ref[0, 0, 0].astype(F32), HI)
    t_pos = i * t + (lax.broadcasted_iota(jnp.int32, (n_cmp, cols), 1) & (t - 1))
    blk_end = lax.broadcasted_iota(jnp.int32, (n_cmp, cols), 0) * CMP_STRIDE + (CMP_BLOCK - 1)
    mask = blk_end <= t_pos
    s = jnp.where(mask, s, NEG_INF)
    e = jnp.exp(s - jnp.max(s, axis=0, keepdims=True))
    p = jnp.where(mask, e * (1.0 / jnp.sum(e, axis=0, keepdims=True)), 0.0)
    o_ref[0, 0, 0] = _dot(vct_ref[0, 0], p, HI)

    p_group = p[:, 0:t]
    for g in range(1, G):
        p_group = p_group + p[:, g * t:(g + 1) * t]
    imp = _dot(ovt_ref[...], p_group, HI)
    blk = lax.broadcasted_iota(jnp.int32, (n_sel, t), 0)
    cur = jnp.right_shift(i * t + lax.broadcasted_iota(jnp.int32, (n_sel, t), 1), SLC_SHIFT)
    imp = jnp.where(blk <= cur, jnp.where(blk == 0, FORCE_SCORE, jnp.where(blk >= cur - 1, FORCE_SCORE, imp)),
                    NEG_INF)
    rank = jnp.zeros((n_sel, t), F32)
    for c in range(n_sel):
        other = imp[c:c + 1, :]
        later = jnp.where(blk > c, 1.0, 0.0)
        rank = rank + jnp.where(other > imp, 1.0, 0.0) + jnp.where(other == imp, later, 0.0)
    sel_ref[0, 0] = jnp.where(rank < float(min(SLC_TOPK, n_sel)), 1.0, 0.0)


def _nsa_cmp(q, kc, vct, overlap_t):
    B, Hk, nq, d, cols = q.shape
    n_cmp = kc.shape[2]
    n_sel = overlap_t.shape[0]
    t = n_sel * SLC_BLOCK // nq
    G = cols // t
    o_spec = pl.BlockSpec((1, 1, 1, d, cols), lambda b, h, i: (b, h, i, 0, 0))
    return pl.pallas_call(
        functools.partial(_nsa_cmp_kernel, t=t, G=G),
        grid=(B, Hk, nq),
        in_specs=[
            o_spec,
            pl.BlockSpec((1, 1, n_cmp, d), lambda b, h, i: (b, h, 0, 0)),
            pl.BlockSpec((1, 1, d, n_cmp), lambda b, h, i: (b, h, 0, 0)),
            pl.BlockSpec((n_sel, n_cmp), lambda b, h, i: (0, 0)),
        ],
        out_specs=[o_spec, pl.BlockSpec((1, 1, n_sel, t), lambda b, h, i: (b, h, 0, i))],
        out_shape=[
            jax.ShapeDtypeStruct((B, Hk, nq, d, cols), F32),
            jax.ShapeDtypeStruct((B, Hk, n_sel, nq * t), F32),
        ],
        compiler_params=_params("parallel", "parallel", "parallel"),
        name="nsa_cmp",
    )(q, kc, vct, overlap_t)


def _nsa_combine_kernel(gl_ref, oc_ref, os_ref, ow_ref, o_ref):
    g = jax.nn.sigmoid(gl_ref[0, 0, 0])
    o_ref[0, 0, 0] = (g[0:1] * oc_ref[0, 0, 0] + g[1:2] * os_ref[0, 0, 0]
                      + g[2:3] * ow_ref[0, 0, 0]).astype(o_ref.dtype)


def _nsa_combine(gl, o_cmp, o_slc, o_win):
    B, Hk, nq, d, cols = o_cmp.shape
    o_spec = pl.BlockSpec((1, 1, 1, d, cols), lambda b, h, i: (b, h, i, 0, 0))
    return pl.pallas_call(
        _nsa_combine_kernel,
        grid=(B, Hk, nq),
        in_specs=[pl.BlockSpec((1, 1, 1, 3, cols), lambda b, h, i: (b, h, i, 0, 0)), o_spec, o_spec, o_spec],
        out_specs=o_spec,
        out_shape=jax.ShapeDtypeStruct(o_cmp.shape, BF16),
        compiler_params=_params("parallel", "parallel", "parallel"),
        name="nsa_combine",
    )(gl, o_cmp, o_slc, o_win)


def _rwkv_prep_kernel(p_ref, mu_ref, w0_ref, ww2_ref, a0_ref, aw2_ref, gw2_ref, kk_ref, ka_ref, seg_ref,
                      r_o, lw_o, k_o, v_o, kk_o, b_o, g_o, last_ref):
    tb = p_ref.shape[1]
    width = r_o.shape[-1]

    @pl.when(pl.program_id(1) == 0)
    def _():
        last_ref[...] = jnp.zeros_like(last_ref)

    p = p_ref[0]
    row = lax.broadcasted_iota(jnp.int32, p.shape, 0)
    prev = jnp.where(row == 0, last_ref[...], pltpu.roll(p, 1, 0))
    last_ref[...] = p[tb - 1:tb]
    xs = p + (prev - p) * mu_ref[...]
    r = xs[:, 0:width]
    k = xs[:, width:2 * width]
    v = xs[:, 2 * width:3 * width]
    lora = xs[:, 3 * width:3 * width + 128]
    g_lo = xs[:, 3 * width + 128:]
    lw = -math.exp(-0.5) * jax.nn.sigmoid(w0_ref[...] + _dot(jnp.tanh(lora), ww2_ref[...], HI))
    a = jax.nn.sigmoid(a0_ref[...] + _dot(lora, aw2_ref[...], HI))
    g = _dot(jax.nn.sigmoid(g_lo), gw2_ref[...], HI)
    kk = k * kk_ref[...]
    norm = jnp.sqrt(_dot(kk * kk, seg_ref[...], HI))
    kk = kk / jnp.maximum(norm, 1e-12)
    r_o[0] = r
    lw_o[0] = lw
    k_o[0] = k * (1.0 + (a - 1.0) * ka_ref[...])
    v_o[0] = v
    kk_o[0] = kk
    b_o[0] = kk * a
    g_o[0] = g


def _rwkv_prep(p, consts, tb=256):
    B, S, P = p.shape
    width = consts[1].shape[-1]
    tb = _tile(S, tb)
    const_specs = [pl.BlockSpec(c.shape, lambda b, t: (0, 0)) for c in consts]
    o_spec = pl.BlockSpec((1, tb, width), lambda b, t: (b, t, 0))
    return pl.pallas_call(
        _rwkv_prep_kernel,
        grid=(B, S // tb),
        in_specs=[pl.BlockSpec((1, tb, P), lambda b, t: (b, t, 0))] + const_specs,
        out_specs=[o_spec] * 7,
        out_shape=[jax.ShapeDtypeStruct((B, S, width), F32)] * 7,
        scratch_shapes=[pltpu.VMEM((1, P), F32)],
        compiler_params=_params("parallel", "arbitrary"),
        name="rwkv_prep",
    )(p, *consts)


def _cumsum_rows(x):
    n = x.shape[0]
    row = lax.broadcasted_iota(jnp.int32, x.shape, 0)
    step = 1
    while step < n:
        x = x + jnp.where(row >= step, pltpu.roll(x, step, 0), 0.0)
        step *= 2
    return x


def _seg_sum(x, seg):
    rows = x.shape[0]
    hi = x.astype(BF16)
    lo = (x - hi.astype(F32)).astype(BF16)
    both = _dot(jnp.concatenate([hi, lo], axis=0), seg)
    return both[:rows] + both[rows:]


def _rwkv_group_chunk(rg, kkg, bd, kd, kc, bc, v, gamma, state, lane_masks, strict, incl, eye, chunk):
    stack = lambda x: jnp.concatenate([x * m for m in lane_masks], axis=0)
    kks, rs, vs = stack(kkg), stack(rg), stack(v)
    n, w = kks.shape
    big = _dot_nt(jnp.concatenate([kks, rs], axis=0), jnp.concatenate([stack(bd), stack(kd)], axis=0))
    a_mat = jnp.where(strict, big[:n, :n], 0.0)
    bk = jnp.where(strict, big[:n, n:], 0.0)
    ay = jnp.where(incl, big[n:, :n], 0.0)
    by = jnp.where(incl, big[n:, n:], 0.0)
    power = -a_mat
    inv = eye + power
    power = _dot(power, power)
    for _ in range(int(math.log2(chunk)) - 2):
        both = _dot(jnp.concatenate([power, inv], axis=0), power)
        power, inv = both[:n], inv + both[n:]
    inv = inv + _dot(inv, power)
    bv = _dot(jnp.concatenate([bk, by], axis=0), vs)
    tx = _dot(inv, jnp.concatenate([kks, bv[:n]], axis=1))
    ayx = _dot(ay, tx)
    q_hat = rs - ayx[:, :w]
    y_intra = bv[n:] - ayx[:, w:]
    qt = _dot_nt(jnp.concatenate([q_hat, tx[:, :w]], axis=0), state)
    ys = qt[:n] + y_intra
    us = qt[n:] + tx[:, w:]
    new_state = state * gamma + _dot_tn(jnp.concatenate([vs, us], axis=0),
                                        jnp.concatenate([stack(kc), -stack(bc)], axis=0))
    y = ys[:chunk]
    for h in range(1, len(lane_masks)):
        y = y + ys[h * chunk:(h + 1) * chunk]
    return y, new_state


def _rwkv_scan_kernel(r_ref, lw_ref, k_ref, v_ref, kk_ref, b_ref, g_ref, lnw_ref, lnb_ref, rk_ref, seg_ref,
                      o_ref, state_ref, *, heads, chunk, group):
    tb, width = r_ref.shape[1], r_ref.shape[2]
    n = width // heads
    groups, gw, gn = heads // group, group * n, group * chunk

    @pl.when(pl.program_id(1) == 0)
    def _():
        state_ref[...] = jnp.zeros_like(state_ref)

    lane = lax.broadcasted_iota(jnp.int32, (1, gw), 1)
    lane_masks = [jnp.where(jnp.logical_and(lane >= h * n, lane < (h + 1) * n), 1.0, 0.0) for h in range(group)]
    row_id = lax.broadcasted_iota(jnp.int32, (gn, gn), 0)
    col_id = lax.broadcasted_iota(jnp.int32, (gn, gn), 1)
    row, col = row_id & (chunk - 1), col_id & (chunk - 1)
    strict, incl = col < row, col <= row
    eye = jnp.where(row_id == col_id, 1.0, 0.0)
    states = [state_ref[p] for p in range(groups)]
    y_chunks = []
    for c in range(tb // chunk):
        rows = pl.ds(c * chunk, chunk)
        r, lw, k, v = r_ref[0, rows, :], lw_ref[0, rows, :], k_ref[0, rows, :], v_ref[0, rows, :]
        kk, bb = kk_ref[0, rows, :], b_ref[0, rows, :]
        g = _cumsum_rows(lw)
        g_last = g[chunk - 1:chunk]
        decay_out = jnp.exp(-g)
        decay_end = jnp.exp(g_last - g)
        rg, kkg = r * jnp.exp(g), kk * jnp.exp(g - lw)
        bd, kd = bb * decay_out, k * decay_out
        kc, bc = k * decay_end, bb * decay_end
        gamma = jnp.exp(g_last)
        ys = []
        for p in range(groups):
            sl = slice(p * gw, (p + 1) * gw)
            y, states[p] = _rwkv_group_chunk(rg[:, sl], kkg[:, sl], bd[:, sl], kd[:, sl], kc[:, sl], bc[:, sl],
                                             v[:, sl], gamma[:, sl], states[p], lane_masks, strict, incl, eye, chunk)
            ys.append(y)
        y_chunks.append(jnp.concatenate(ys, axis=1))
    for p in range(groups):
        state_ref[p] = states[p]
    y = jnp.concatenate(y_chunks, axis=0)
    seg = seg_ref[...]
    r, k, v = r_ref[0], k_ref[0], v_ref[0]
    sums = _seg_sum(jnp.concatenate([y, r * k * rk_ref[...]], axis=0), seg)
    dev = y - sums[:tb] * (1.0 / n)
    var = _seg_sum(dev * dev, seg) * (1.0 / n)
    yn = dev * lax.rsqrt(var + RWKV_GN_EPS) * lnw_ref[...] + lnb_ref[...]
    o_ref[0] = ((yn + sums[tb:] * v) * g_ref[0]).astype(o_ref.dtype)


def _rwkv_scan(r, lw, k, v, kk, bb, g, ln_w, ln_b, r_k, seg, tb=256, group=4):
    B, S, width = r.shape
    heads = RWKV_HEADS
    n = width // heads
    chunk = min(RWKV_CHUNK, S)
    assert chunk == n and heads % group == 0
    tb = _tile(S, tb)
    x_spec = pl.BlockSpec((1, tb, width), lambda b, t: (b, t, 0))
    vec_spec = pl.BlockSpec((1, width), lambda b, t: (0, 0))
    return pl.pallas_call(
        functools.partial(_rwkv_scan_kernel, heads=heads, chunk=chunk, group=group),
        grid=(B, S // tb),
        in_specs=[x_spec] * 7 + [vec_spec] * 3 + [pl.BlockSpec((width, width), lambda b, t: (0, 0))],
        out_specs=x_spec,
        out_shape=jax.ShapeDtypeStruct((B, S, width), BF16),
        scratch_shapes=[pltpu.VMEM((heads // group, group * n, group * n), F32)],
        compiler_params=_params("parallel", "arbitrary"),
        name="rwkv_scan",
    )(r, lw, k, v, kk, bb, g, ln_w, ln_b, r_k, seg.astype(BF16))


def _gla_kernel(q_ref, k_ref, v_ref, r_ref, al_ref, aw_ref, ab_ref, ng_ref, o_ref, state_ref,
                *, heads, chunk):
    tb = v_ref.shape[1]
    dk = q_ref.shape[-1]
    dv = v_ref.shape[-1] // heads

    @pl.when(pl.program_id(1) == 0)
    def _():
        state_ref[...] = jnp.zeros_like(state_ref)

    row = lax.broadcasted_iota(jnp.int32, (chunk, chunk), 0)
    col = lax.broadcasted_iota(jnp.int32, (chunk, chunk), 1)
    incl = col <= row
    states = [state_ref[h] for h in range(heads)]
    for c in range(tb // chunk):
        rows = pl.ds(c * chunk, chunk)
        x = _dot(al_ref[0, rows, :], aw_ref[...], HI) + ab_ref[...]
        log_a = (jnp.minimum(x, 0.0) - jnp.log(1.0 + jnp.exp(-jnp.abs(x)))) * (1.0 / GLA_TAU)
        b_all = _cumsum_rows(log_a)
        outs = []
        for h in range(heads):
            cols = pl.ds(h * dv, dv)
            b = b_all[:, h * dk:(h + 1) * dk]
            b_last = b[chunk - 1:chunk]
            q = q_ref[0, h, rows, :]
            k = k_ref[0, h, rows, :]
            v = v_ref[0, rows, cols]
            qe = q * (dk ** -0.5) * jnp.exp(b)
            attn = jnp.where(incl, _dot_nt(qe, k * jnp.exp(-b)), 0.0)
            o = _dot(attn, v) + _dot_nt(qe, states[h])
            states[h] = states[h] * jnp.exp(b_last) + _dot_tn(v, k * jnp.exp(b_last - b))
            outs.append(o * lax.rsqrt(jnp.mean(o * o, axis=-1, keepdims=True) + NORM_EPS))
        gate = r_ref[0, rows, :]
        o_ref[0, rows, :] = (jnp.concatenate(outs, axis=1) * ng_ref[...]
                             * (gate * jax.nn.sigmoid(gate))).astype(o_ref.dtype)
    for h in range(heads):
        state_ref[h] = states[h]


def _gla(q, k, v, r, alpha_lo, aw, ab, ng, tb=256):
    B, H, S, dk = q.shape
    width = v.shape[-1]
    lora = alpha_lo.shape[-1]
    chunk = min(GLA_CHUNK, S)
    tb = _tile(S, tb)
    qk_spec = pl.BlockSpec((1, H, tb, dk), lambda b, t: (b, 0, t, 0))
    x_spec = pl.BlockSpec((1, tb, width), lambda b, t: (b, t, 0))
    return pl.pallas_call(
        functools.partial(_gla_kernel, heads=H, chunk=chunk),
        grid=(B, S // tb),
        in_specs=[
            qk_spec, qk_spec, x_spec, x_spec,
            pl.BlockSpec((1, tb, lora), lambda b, t: (b, t, 0)),
            pl.BlockSpec(aw.shape, lambda b, t: (0, 0)),
            pl.BlockSpec(ab.shape, lambda b, t: (0, 0)),
            pl.BlockSpec(ng.shape, lambda b, t: (0, 0)),
        ],
        out_specs=x_spec,
        out_shape=jax.ShapeDtypeStruct((B, S, width), BF16),
        scratch_shapes=[pltpu.VMEM((H, width // H, dk), F32)],
        compiler_params=_params("parallel", "arbitrary"),
        name="gla",
    )(q, k, v, r, alpha_lo, aw, ab, ng)


def _rope_tables(S, d):
    inv = ROPE_THETA ** (-jnp.arange(0, d, 2, dtype=F32) / d)
    ang = jnp.arange(S).astype(F32)[:, None] * inv[None, :]
    cos, sin = jnp.cos(ang), jnp.sin(ang)
    return jnp.concatenate([cos, cos], axis=-1), jnp.concatenate([sin, sin], axis=-1)


def _rot_cols(w, d):
    K = w.shape[0]
    w = w.reshape(K, -1, 2, d // 2)
    return jnp.concatenate([-w[:, :, 1], w[:, :, 0]], axis=-1).reshape(K, -1)


def _nsa_constants(S):
    n_pieces = S // CMP_STRIDE
    n_cmp = (S - CMP_BLOCK) // CMP_STRIDE + 1
    n_sel = S // SLC_BLOCK
    start = np.arange(n_cmp)[:, None] * CMP_STRIDE
    end = start + CMP_BLOCK - 1
    blk = np.arange(n_sel)[None, :]
    overlap = np.zeros((n_pieces, n_sel), np.float32)
    overlap[:n_cmp] = ((start < (blk + 1) * SLC_BLOCK) & (end >= blk * SLC_BLOCK)).astype(np.float32)
    return jnp.asarray(overlap.T)


def _token_mixing(u, B, S, w_in, w_branch, cmp_pos, cmp_w1, cmp_w2, rwkv_mu, rwkv_w0, rwkv_w_w2, rwkv_a0,
                  rwkv_a_w2, rwkv_g_w2, rwkv_k_k, rwkv_k_a, rwkv_r_k, rwkv_ln_w, rwkv_ln_b, gla_alpha_w2,
                  gla_alpha_b, gla_norm_g, mla_q_norm, mla_w_uq, mla_kv_norm, mla_w_ukv):
    M, D = u.shape
    d = HEAD_DIM
    width = D // 4
    sizes = [width + 6 * NSA_KV_HEADS * d + 3 * NSA_HEADS, 3 * width + 256, 2 * (width // 2) + 2 * width + 16,
             384 + 128 + MLA_ROPE, 4 * D]
    offs = np.cumsum([0] + sizes)
    assert offs[-1] == w_in.shape[1]
    w_nsa, w_rwkv, w_gla, w_mla, w_gate = [w_in[:, offs[i]:offs[i + 1]] for i in range(5)]
    zeros = lambda n: jnp.zeros((D, n), F32)
    kvw = NSA_KV_HEADS * d

    wq = w_nsa[:, :width]
    wkc, wvc, wks, wvs, wkw, wvw = [w_nsa[:, width + i * kvw:width + (i + 1) * kvw] for i in range(6)]
    wgl = w_nsa[:, width + 6 * kvw:]
    wkr = w_mla[:, 512:512 + MLA_ROPE]
    n_rope = width + 3 * kvw + MLA_ROPE
    pad = (-n_rope) % 512
    w_a = jnp.concatenate([wq, wkc, wks, wkw, wkr, zeros(pad)], axis=1)
    w_b = jnp.concatenate([_rot_cols(wq, d), _rot_cols(wkc, d), _rot_cols(wks, d), _rot_cols(wkw, d),
                           _rot_cols(wkr, MLA_ROPE), zeros(pad)], axis=1)
    cos_d, sin_d = _rope_tables(S, d)
    cos_r, sin_r = _rope_tables(S, MLA_ROPE)
    reps = (width + 3 * kvw) // d
    cos_t = jnp.concatenate([jnp.tile(cos_d, (1, reps)), cos_r, jnp.ones((S, pad), F32)], axis=1)
    sin_t = jnp.concatenate([jnp.tile(sin_d, (1, reps)), sin_r, jnp.zeros((S, pad), F32)], axis=1)
    roped = _matmul(u, w_a, S=S, w_rot=w_b, cos=cos_t, sin=sin_t, name="proj_rope").reshape(B, S, -1)
    n_plain = 3 * kvw + 3 * NSA_HEADS
    plain = _matmul(u, jnp.concatenate([wvc, wvs, wvw, wgl, zeros((-n_plain) % 128)], axis=1), S=S,
                    name="proj_nsa_v")

    Hk, G = NSA_KV_HEADS, NSA_GROUP
    t_att = _tile(S, 256)
    nq = S // t_att
    cols_layout = lambda x, f: x.reshape(B, nq, t_att, Hk, G, f).transpose(0, 3, 1, 5, 4, 2).reshape(
        B, Hk, nq, f, G * t_att)
    kv_heads = lambda x: x.reshape(B, S, Hk, d).transpose(0, 2, 1, 3)
    v_tiles = lambda x: x.astype(BF16).reshape(B, nq, t_att, Hk, d).transpose(0, 3, 1, 4, 2)
    q_att = cols_layout((roped[..., :width] * d ** -0.5).astype(BF16), d)
    kc_in = kv_heads(roped[..., width:width + kvw])
    ks = kv_heads(roped[..., width + kvw:width + 2 * kvw].astype(BF16))
    kw = kv_heads(roped[..., width + 2 * kvw:width + 3 * kvw].astype(BF16))
    k_pe = roped[..., width + 3 * kvw:width + 3 * kvw + MLA_ROPE]
    plain3 = plain.reshape(B, S, -1)
    vc_in = kv_heads(plain3[..., :kvw])
    vs, vw = v_tiles(plain3[..., kvw:2 * kvw]), v_tiles(plain3[..., 2 * kvw:3 * kvw])
    gate_logits = cols_layout(plain3[..., 3 * kvw:3 * kvw + 3 * NSA_HEADS], 3)

    overlap_t = _nsa_constants(S)
    n_pieces = S // CMP_STRIDE
    kv_flat = jnp.stack([kc_in, vc_in]).reshape(2, B, Hk, n_pieces, CMP_STRIDE * d)
    cmp = _nsa_compress(kv_flat, cmp_pos.reshape(2, 1, CMP_BLOCK * d), cmp_w1, cmp_w2)
    o_cmp, sel = _nsa_cmp(q_att, cmp[0], cmp[1].transpose(0, 1, 3, 2), overlap_t)
    o_slc = _flash(q_att, ks, vs, mode="select", sel=sel)
    o_win = _flash(q_att, kw, vw, mode="window")
    y_nsa = _nsa_combine(gate_logits, o_cmp, o_slc, o_win)
    y_nsa = y_nsa.reshape(B, Hk, nq, d, G, t_att).transpose(0, 2, 5, 1, 4, 3).reshape(M, width)

    regroup = lambda t: jnp.concatenate(
        [t[..., 0:width], t[..., width + 64:2 * width + 64], t[..., 2 * width + 64:3 * width + 64],
         t[..., width:width + 64], t[..., 3 * width + 64:3 * width + 128], t[..., 3 * width + 128:]], axis=-1)
    p_rwkv = _matmul(u, regroup(w_rwkv), S=S, tn=896, name="proj_rwkv").reshape(B, S, -1)
    row = lambda t: t.reshape(1, -1)
    seg = jnp.asarray(np.kron(np.eye(RWKV_HEADS), np.ones((d, d))).astype(np.float32))
    lora_pad = jnp.zeros((64, width), F32)
    consts = [row(regroup(rwkv_mu)), row(rwkv_w0), jnp.concatenate([rwkv_w_w2, lora_pad], axis=0), row(rwkv_a0),
              jnp.concatenate([lora_pad, rwkv_a_w2], axis=0), rwkv_g_w2, row(rwkv_k_k), row(rwkv_k_a), seg]
    r, lw, k2, v, kk, bb, g = _rwkv_prep(p_rwkv, consts)
    y_rwkv = _rwkv_scan(r, lw, k2, v, kk, bb, g, row(rwkv_ln_w), row(rwkv_ln_b), row(rwkv_r_k), seg)
    y_rwkv = y_rwkv.reshape(M, width)

    dk_all = width // 2
    w_gla_p = jnp.concatenate([w_gla[:, :2 * dk_all + width], w_gla[:, 2 * dk_all + width + 16:],
                               w_gla[:, 2 * dk_all + width:2 * dk_all + width + 16], zeros(240)], axis=1)
    p_gla = _matmul(u, w_gla_p, S=S, tn=896, name="proj_gla").reshape(B, S, -1)
    dkh = dk_all // GLA_HEADS
    gla_heads = lambda t: t.reshape(B, S, GLA_HEADS, dkh).transpose(0, 2, 1, 3)
    y_gla = _gla(gla_heads(p_gla[..., :dk_all]), gla_heads(p_gla[..., dk_all:2 * dk_all]),
                 p_gla[..., 2 * dk_all:2 * dk_all + width], p_gla[..., 2 * dk_all + width:2 * dk_all + 2 * width],
                 p_gla[..., 2 * dk_all + 2 * width:2 * dk_all + 2 * width + 16],
                 gla_alpha_w2, row(gla_alpha_b), row(gla_norm_g)).reshape(M, width)

    c_q = _matmul(u, w_mla[:, :384], S=S, name="proj_mla_q")
    c_kv = _matmul(u, w_mla[:, 384:512], S=S, name="proj_mla_kv")
    H = MLA_HEADS
    wq3 = mla_w_uq.reshape(384, H, MLA_NOPE + MLA_ROPE)
    rp = wq3[:, :, MLA_NOPE:]
    half = MLA_ROPE // 2
    z = lambda n: jnp.zeros((384, H, n), F32)
    wq_a = jnp.concatenate([wq3, z(128 - MLA_NOPE - MLA_ROPE)], axis=-1).reshape(384, H * 128)
    wq_b = jnp.concatenate([z(MLA_NOPE), -rp[..., half:], rp[..., :half], z(128 - MLA_NOPE - MLA_ROPE)],
                           axis=-1).reshape(384, H * 128)
    tail = 128 - MLA_NOPE - MLA_ROPE
    cos_h = jnp.concatenate([jnp.ones((S, MLA_NOPE), F32), cos_r, jnp.ones((S, tail), F32)], axis=1)
    sin_h = jnp.concatenate([jnp.zeros((S, MLA_NOPE), F32), sin_r, jnp.zeros((S, tail), F32)], axis=1)
    q_m = _matmul(c_q, wq_a, S=S, g=mla_q_norm, w_rot=wq_b, cos=jnp.tile(cos_h, (1, H)), sin=jnp.tile(sin_h, (1, H)),
                  name="mla_q")
    t_mla = _tile(S, 512)
    nq_m = S // t_mla
    tiles_m = lambda x, f: x.reshape(B, nq_m, t_mla, H, f).transpose(0, 3, 1, 4, 2)
    q_m = tiles_m((q_m * (MLA_NOPE + MLA_ROPE) ** -0.5).astype(BF16), 128)
    kv_m = _matmul(c_kv, mla_w_ukv, S=S, g=mla_kv_norm, name="mla_kv").reshape(B, S, H, MLA_NOPE + width // H)
    k_m = jnp.concatenate([kv_m[..., :MLA_NOPE], jnp.broadcast_to(k_pe[:, :, None, :], (B, S, H, MLA_ROPE)),
                           jnp.zeros((B, S, H, tail), F32)], axis=-1).astype(BF16).transpose(0, 2, 1, 3)
    v_m = tiles_m(kv_m[..., MLA_NOPE:].astype(BF16), width // H)
    o_mla = _flash(q_m, k_m, v_m, mode="causal", out_dtype=BF16)
    y_mla = o_mla.transpose(0, 2, 4, 1, 3).reshape(M, width)

    return _merge(u, w_gate, (y_nsa, y_rwkv, y_gla, y_mla), w_branch)


def kernel(x, c, ada_w, ada_b, pre_g, post_g, ffn_wg, ffn_wu, ffn_wd, mix_w_in, mix_w_branch, mix_w_out, nsa_cmp_pos, nsa_cmp_w1, nsa_cmp_w2, rwkv_mu, rwkv_w0, rwkv_w_w2, rwkv_a0, rwkv_a_w2, rwkv_g_w2, rwkv_k_k, rwkv_k_a, rwkv_r_k, rwkv_ln_w, rwkv_ln_b, gla_alpha_w2, gla_alpha_b, gla_norm_g, mla_q_norm, mla_w_uq, mla_kv_norm, mla_w_ukv):
    B, S, D = x.shape
    M = B * S
    depth = ada_w.shape[0]
    mod = _modulation(c, ada_w, ada_b)
    x2 = x.reshape(M, D)
    for l in range(depth):
        for i in range(3):
            m = mod[3 * l + i]
            shift, scale, gate = m[:, :D], m[:, D:2 * D], m[:, 2 * D:]
            u = _prenorm(x2, pre_g[l, i], scale, shift, S)
            if i == 1:
                y = _token_mixing(
                    u, B, S, mix_w_in[l], mix_w_branch[l], nsa_cmp_pos[l], nsa_cmp_w1[l], nsa_cmp_w2[l], rwkv_mu[l],
                    rwkv_w0[l], rwkv_w_w2[l], rwkv_a0[l], rwkv_a_w2[l], rwkv_g_w2[l], rwkv_k_k[l], rwkv_k_a[l],
                    rwkv_r_k[l], rwkv_ln_w[l], rwkv_ln_b[l], gla_alpha_w2[l], gla_alpha_b[l], gla_norm_g[l],
                    mla_q_norm[l], mla_w_uq[l], mla_kv_norm[l], mla_w_ukv[l])
                x2 = _down_post(y, mix_w_out[l], x2, post_g[l, i], gate, 1.0, S)
            else:
                j = 0 if i == 0 else 1
                h = _ffn_up(u, ffn_wg[l, j], ffn_wu[l, j])
                x2 = _down_post(h, ffn_wd[l, j], x2, post_g[l, i], gate, 0.5, S)
    return x2.reshape(B, S, D)
```

```python
import functools
import math

import jax
import jax.numpy as jnp
import numpy as np
from jax import lax
from jax.experimental import pallas as pl
from jax.experimental.pallas import tpu as pltpu

F32 = jnp.float32
BF16 = jnp.bfloat16
HI = lax.Precision.HIGHEST

NORM_EPS = 1e-6
NEG_INF = -1e30
FORCE_SCORE = 1e30
ROPE_THETA = 10000.0
HEAD_DIM = 64

NSA_HEADS = 8
NSA_KV_HEADS = 2
NSA_GROUP = NSA_HEADS // NSA_KV_HEADS
CMP_BLOCK = 32
CMP_STRIDE = 16
SLC_BLOCK = 64
SLC_SHIFT = 6
SLC_TOPK = 16
WIN_SIZE = 512
RWKV_HEADS = 8
RWKV_GN_EPS = 64e-5
RWKV_CHUNK = 64
GLA_HEADS = 4
GLA_TAU = 16.0
GLA_CHUNK = 64
MLA_HEADS = 8
MLA_NOPE = 64
MLA_ROPE = 32

VMEM_LIMIT_BYTES = 56 * 1024 * 1024


def _params(*sem):
    return pltpu.CompilerParams(dimension_semantics=sem, vmem_limit_bytes=VMEM_LIMIT_BYTES)


def _dot(a, b, precision=None):
    return jnp.dot(a, b, preferred_element_type=F32, precision=precision)


def _dot_nt(a, b, precision=None):
    return lax.dot_general(a, b, (((1,), (1,)), ((), ())), preferred_element_type=F32, precision=precision)


def _dot_tn(a, b, precision=None):
    return lax.dot_general(a, b, (((0,), (0,)), ((), ())), preferred_element_type=F32, precision=precision)


def _rms(x):
    return x * lax.rsqrt(jnp.mean(x * x, axis=-1, keepdims=True) + NORM_EPS)


def _tile(n, t):
    t = min(n, t)
    assert n % t == 0, (n, t)
    return t


def _mod_kernel(c_ref, w_ref, b_ref, o_ref):
    c = c_ref[...]
    act = c * jax.nn.sigmoid(c)
    hi = act.astype(BF16)
    lo = (act - hi.astype(F32)).astype(BF16)
    w = w_ref[0].astype(BF16)
    o_ref[0] = _dot(hi, w) + _dot(lo, w) + b_ref[0]


def _modulation(c, ada_w, ada_b):
    B, D = c.shape
    n_sub = ada_w.shape[0] * ada_w.shape[1]
    w = ada_w.reshape(n_sub, D, 3 * D)
    b = ada_b.reshape(n_sub, 1, 3 * D)
    rows = 8
    cp = jnp.zeros((rows, D), F32).at[:B].set(c)
    tn = _tile(3 * D, 512)
    out = pl.pallas_call(
        _mod_kernel,
        grid=(n_sub, 3 * D // tn),
        in_specs=[
            pl.BlockSpec((rows, D), lambda s, j: (0, 0)),
            pl.BlockSpec((1, D, tn), lambda s, j: (s, 0, j)),
            pl.BlockSpec((1, 1, tn), lambda s, j: (s, 0, j)),
        ],
        out_specs=pl.BlockSpec((1, rows, tn), lambda s, j: (s, 0, j)),
        out_shape=jax.ShapeDtypeStruct((n_sub, rows, 3 * D), F32),
        compiler_params=_params("parallel", "parallel"),
        name="modulation",
    )(cp, w, b)
    return out[:, :B]


def _prenorm_kernel(x_ref, g_ref, sc_ref, sh_ref, o_ref):
    y = _rms(x_ref[...]) * g_ref[...]
    o_ref[...] = (y * (1.0 + sc_ref[0]) + sh_ref[0]).astype(o_ref.dtype)


def _prenorm(x2, g, scale, shift, S):
    M, D = x2.shape
    B = M // S
    tm = _tile(S, 512)
    per_b = S // tm
    return pl.pallas_call(
        _prenorm_kernel,
        grid=(M // tm,),
        in_specs=[
            pl.BlockSpec((tm, D), lambda i: (i, 0)),
            pl.BlockSpec((1, D), lambda i: (0, 0)),
            pl.BlockSpec((1, 1, D), lambda i: (i // per_b, 0, 0)),
            pl.BlockSpec((1, 1, D), lambda i: (i // per_b, 0, 0)),
        ],
        out_specs=pl.BlockSpec((tm, D), lambda i: (i, 0)),
        out_shape=jax.ShapeDtypeStruct((M, D), BF16),
        compiler_params=_params("parallel"),
        name="prenorm",
    )(x2, g.reshape(1, D), scale.reshape(B, 1, D), shift.reshape(B, 1, D))


def _mm_kernel(*refs, norm, rope):
    it = iter(refs)
    a_ref = next(it)
    g_ref = next(it) if norm else None
    w_ref = next(it)
    if rope:
        w2_ref, cos_ref, sin_ref = next(it), next(it), next(it)
    o_ref = next(it)
    a = a_ref[...]
    if norm:
        a = _rms(a.astype(F32)) * g_ref[...]
    a = a.astype(BF16)
    out = _dot(a, w_ref[...])
    if rope:
        out = out * cos_ref[...] + _dot(a, w2_ref[...]) * sin_ref[...]
    o_ref[...] = out.astype(o_ref.dtype)


def _matmul(a, w, *, S, g=None, w_rot=None, cos=None, sin=None, out_dtype=F32, tm=1024, tn=512, name="matmul"):
    M, K = a.shape
    N = w.shape[1]
    tm = _tile(S, tm)
    tn = _tile(N, tn)
    per_b = S // tm
    norm, rope = g is not None, w_rot is not None
    in_specs = [pl.BlockSpec((tm, K), lambda i, j: (i, 0))]
    args = [a]
    if norm:
        in_specs.append(pl.BlockSpec((1, K), lambda i, j: (0, 0)))
        args.append(g.reshape(1, K))
    in_specs.append(pl.BlockSpec((K, tn), lambda i, j: (0, j)))
    args.append(w.astype(BF16))
    if rope:
        in_specs += [
            pl.BlockSpec((K, tn), lambda i, j: (0, j)),
            pl.BlockSpec((tm, tn), lambda i, j: (i % per_b, j)),
            pl.BlockSpec((tm, tn), lambda i, j: (i % per_b, j)),
        ]
        args += [w_rot.astype(BF16), cos, sin]
    return pl.pallas_call(
        functools.partial(_mm_kernel, norm=norm, rope=rope),
        grid=(M // tm, N // tn),
        in_specs=in_specs,
        out_specs=pl.BlockSpec((tm, tn), lambda i, j: (i, j)),
        out_shape=jax.ShapeDtypeStruct((M, N), out_dtype),
        compiler_params=_params("parallel", "parallel"),
        name=name,
    )(*args)


def _ffn_up_kernel(u_ref, wg_ref, wu_ref, o_ref, wg_bf, wu_bf):
    @pl.when(pl.program_id(1) == 0)
    def _():
        wg_bf[...] = wg_ref[...].astype(BF16)
        wu_bf[...] = wu_ref[...].astype(BF16)

    u = u_ref[...]
    a = _dot(u, wg_bf[...])
    b = _dot(u, wu_bf[...])
    o_ref[...] = (a * jax.nn.sigmoid(a) * b).astype(o_ref.dtype)


def _ffn_up(u, wg, wu, tm=1024, tn=512):
    M, K = u.shape
    N = wg.shape[1]
    tm, tn = _tile(M, tm), _tile(N, tn)
    return pl.pallas_call(
        _ffn_up_kernel,
        grid=(N // tn, M // tm),
        in_specs=[
            pl.BlockSpec((tm, K), lambda j, i: (i, 0)),
            pl.BlockSpec((K, tn), lambda j, i: (0, j)),
            pl.BlockSpec((K, tn), lambda j, i: (0, j)),
        ],
        out_specs=pl.BlockSpec((tm, tn), lambda j, i: (i, j)),
        out_shape=jax.ShapeDtypeStruct((M, N), BF16),
        scratch_shapes=[pltpu.VMEM((K, tn), BF16), pltpu.VMEM((K, tn), BF16)],
        compiler_params=_params("parallel", "arbitrary"),
        name="ffn_up",
    )(u, wg, wu)


def _cast_kernel(x_ref, o_ref):
    o_ref[...] = x_ref[...].astype(o_ref.dtype)


def _to_bf16(w, rows=512):
    R, C = w.shape
    rows = _tile(R, rows)
    return pl.pallas_call(
        _cast_kernel,
        grid=(R // rows,),
        in_specs=[pl.BlockSpec((rows, C), lambda i: (i, 0))],
        out_specs=pl.BlockSpec((rows, C), lambda i: (i, 0)),
        out_shape=jax.ShapeDtypeStruct((R, C), BF16),
        compiler_params=_params("parallel"),
        name="cast_bf16",
    )(w)


def _down_post_kernel(h_ref, w_ref, x_ref, g_ref, gate_ref, *rest, res_w, with_next):
    y = _rms(_dot(h_ref[...], w_ref[...])) * g_ref[...]
    x_new = x_ref[...] + res_w * gate_ref[0] * y
    if with_next:
        gn_ref, sc_ref, sh_ref, o_ref, u_ref = rest
        u_ref[...] = (_rms(x_new) * gn_ref[...] * (1.0 + sc_ref[0]) + sh_ref[0]).astype(u_ref.dtype)
    else:
        (o_ref,) = rest
    o_ref[...] = x_new


def _down_post(h, wd, x2, g, gate, res_w, S, next_norm=None, tm=256):
    M, K = h.shape
    D = wd.shape[1]
    B = M // S
    tm = _tile(S, tm)
    per_b = S // tm
    row_spec = pl.BlockSpec((tm, D), lambda i: (i, 0))
    vec_spec = pl.BlockSpec((1, D), lambda i: (0, 0))
    batch_spec = pl.BlockSpec((1, 1, D), lambda i: (i // per_b, 0, 0))
    in_specs = [
        pl.BlockSpec((tm, K), lambda i: (i, 0)),
        pl.BlockSpec((K, D), lambda i: (0, 0), pipeline_mode=pl.Buffered(1)),
        row_spec, vec_spec, batch_spec,
    ]
    args = [h, _to_bf16(wd), x2, g.reshape(1, D), gate.reshape(B, 1, D)]
    out_specs, out_shape = row_spec, jax.ShapeDtypeStruct((M, D), F32)
    if next_norm is not None:
        g_next, scale_next, shift_next = next_norm
        in_specs += [vec_spec, batch_spec, batch_spec]
        args += [g_next.reshape(1, D), scale_next.reshape(B, 1, D), shift_next.reshape(B, 1, D)]
        out_specs, out_shape = [row_spec, row_spec], [out_shape, jax.ShapeDtypeStruct((M, D), BF16)]
    return pl.pallas_call(
        functools.partial(_down_post_kernel, res_w=res_w, with_next=next_norm is not None),
        grid=(M // tm,),
        in_specs=in_specs,
        out_specs=out_specs,
        out_shape=out_shape,
        compiler_params=_params("parallel"),
        name="down_post",
    )(*args)


def _merge_kernel(u_ref, g0_ref, g1_ref, g2_ref, g3_ref, y0_ref, y1_ref, y2_ref, y3_ref, wb_ref, o_ref,
                  wg_bf, wb_bf):
    @pl.when(pl.program_id(1) == 0)
    def _():
        for i, g_ref in enumerate((g0_ref, g1_ref, g2_ref, g3_ref)):
            wg_bf[i] = g_ref[...].astype(BF16)
        wb_bf[...] = wb_ref[...].astype(BF16)

    u = u_ref[...]
    acc = None
    for i, y_ref in enumerate((y0_ref, y1_ref, y2_ref, y3_ref)):
        gate = jax.nn.sigmoid(_dot(u, wg_bf[i]))
        term = gate * _dot(y_ref[...], wb_bf[i])
        acc = term if acc is None else acc + term
    o_ref[...] = acc.astype(o_ref.dtype)


def _merge(u, w_gate, ys, w_branch, tm=512, tn=256):
    M, D = u.shape
    W = w_branch.shape[1]
    tm, tn = _tile(M, tm), _tile(D, tn)
    per_branch = D // tn
    y_spec = pl.BlockSpec((tm, W), lambda j, i: (i, 0))
    gate_specs = [pl.BlockSpec((D, tn), lambda j, i, b=b: (0, b * per_branch + j)) for b in range(4)]
    return pl.pallas_call(
        _merge_kernel,
        grid=(D // tn, M // tm),
        in_specs=[pl.BlockSpec((tm, D), lambda j, i: (i, 0))] + gate_specs + [y_spec] * 4 + [
            pl.BlockSpec((4, W, tn), lambda j, i: (0, 0, j))],
        out_specs=pl.BlockSpec((tm, tn), lambda j, i: (i, j)),
        out_shape=jax.ShapeDtypeStruct((M, D), BF16),
        scratch_shapes=[pltpu.VMEM((4, D, tn), BF16), pltpu.VMEM((4, W, tn), BF16)],
        compiler_params=_params("parallel", "arbitrary"),
        name="merge",
    )(u, w_gate, w_gate, w_gate, w_gate, *ys, w_branch)


def _flash_kernel(*refs, mode, t, G):
    if mode == "select":
        q_ref, k_ref, v_ref, sel_ref, o_ref, m_ref, l_ref, acc_ref = refs
    else:
        q_ref, k_ref, v_ref, o_ref, m_ref, l_ref, acc_ref = refs
    i = pl.program_id(2)
    cols = G * t
    q = q_ref[0, 0, 0]
    k_off = lax.broadcasted_iota(jnp.int32, (t, cols), 0)
    q_off = lax.broadcasted_iota(jnp.int32, (t, cols), 1) & (t - 1)
    blocks_per_tile = t // SLC_BLOCK

    def scores(j, mask):
        start = pl.multiple_of(j * t, t)
        s = _dot(k_ref[0, 0, pl.ds(start, t), :], q)
        if mode == "select":
            parts = []
            for kb in range(blocks_per_tile):
                picked = sel_ref[0, 0, pl.ds(j * blocks_per_tile + kb, 1), :]
                picked = jnp.concatenate([picked] * G, axis=1) if G > 1 else picked
                parts.append(jnp.where(picked > 0.5, s[kb * SLC_BLOCK:(kb + 1) * SLC_BLOCK], NEG_INF))
            s = jnp.concatenate(parts, axis=0)
        if mask is not None:
            s = jnp.where(mask, s, NEG_INF)
        return s, v_ref[0, 0, j]

    s, v = scores(i, k_off <= q_off)
    m = jnp.max(s, axis=0, keepdims=True)
    p = jnp.exp(s - m)
    m_ref[...] = m
    l_ref[...] = jnp.sum(p, axis=0, keepdims=True)
    acc_ref[...] = _dot(v, p.astype(BF16))

    def accumulate(tiles, mask=None):
        parts = [scores(j, mask) for j in tiles]
        s = jnp.concatenate([part[0] for part in parts], axis=0) if len(parts) > 1 else parts[0][0]
        v = jnp.concatenate([part[1] for part in parts], axis=1) if len(parts) > 1 else parts[0][1]
        m_prev = m_ref[...]
        m_new = jnp.maximum(m_prev, jnp.max(s, axis=0, keepdims=True))
        p = jnp.exp(s - m_new)
        alpha = jnp.exp(m_prev - m_new)
        l_ref[...] = alpha * l_ref[...] + jnp.sum(p, axis=0, keepdims=True)
        acc_ref[...] = alpha * acc_ref[...] + _dot(v, p.astype(BF16))
        m_ref[...] = m_new

    if mode == "window":
        n_back = WIN_SIZE // t
        for back in range(1, n_back + 1):
            mask = (k_off > q_off) if back == n_back else None

            @pl.when(i >= back)
            def _(back=back, mask=mask):
                accumulate([i - back], mask)
    else:
        def body(pair, carry):
            accumulate([2 * pair, 2 * pair + 1])
            return carry

        lax.fori_loop(0, jnp.right_shift(i, 1), body, 0)

        @pl.when((i & 1) == 1)
        def _():
            accumulate([i - 1])

    o_ref[0, 0, 0] = (acc_ref[...] * (1.0 / l_ref[...])).astype(o_ref.dtype)


def _flash(q, k, v, *, mode, sel=None, out_dtype=F32):
    B, Hk, nq, Dq, cols = q.shape
    S = k.shape[2]
    Dv, t = v.shape[3], v.shape[4]
    G = cols // t
    assert t & (t - 1) == 0 and t % SLC_BLOCK == 0 and nq * t == S
    if mode == "window":
        assert WIN_SIZE % t == 0
    in_specs = [
        pl.BlockSpec((1, 1, 1, Dq, cols), lambda b, h, i: (b, h, i, 0, 0)),
        pl.BlockSpec((1, 1, S, Dq), lambda b, h, i: (b, h, 0, 0)),
        pl.BlockSpec((1, 1, nq, Dv, t), lambda b, h, i: (b, h, 0, 0, 0)),
    ]
    args = [q, k, v]
    if mode == "select":
        in_specs.append(pl.BlockSpec((1, 1, sel.shape[2], t), lambda b, h, i: (b, h, 0, i)))
        args.append(sel)
    return pl.pallas_call(
        functools.partial(_flash_kernel, mode=mode, t=t, G=G),
        grid=(B, Hk, nq),
        in_specs=in_specs,
        out_specs=pl.BlockSpec((1, 1, 1, Dv, cols), lambda b, h, i: (b, h, i, 0, 0)),
        out_shape=jax.ShapeDtypeStruct((B, Hk, nq, Dv, cols), out_dtype),
        scratch_shapes=[pltpu.VMEM((1, cols), F32), pltpu.VMEM((1, cols), F32), pltpu.VMEM((Dv, cols), F32)],
        compiler_params=_params("parallel", "parallel", "arbitrary"),
        name="attn_" + mode,
    )(*args)


def _nsa_compress_kernel(t_ref, pos_ref, w1_ref, w2_ref, o_ref):
    n_pieces, half = t_ref.shape[-2], t_ref.shape[-1]
    for which in range(2):
        t = t_ref[which, 0, 0]
        w1 = w1_ref[which]
        first = _dot(t, w1[:half], HI)
        second = _dot(t, w1[half:], HI)
        pos_bias = _dot(pos_ref[which], w1, HI)
        pre = first + pltpu.roll(second, n_pieces - 1, 0) + pos_bias
        o_ref[which, 0, 0] = _dot(jax.nn.gelu(pre), w2_ref[which], HI)


def _nsa_compress(kv_flat, pos, w1, w2):
    _, B, Hk, n_pieces, half = kv_flat.shape
    hid, d = w2.shape[1], w2.shape[2]
    return pl.pallas_call(
        _nsa_compress_kernel,
        grid=(B, Hk),
        in_specs=[
            pl.BlockSpec((2, 1, 1, n_pieces, half), lambda b, h: (0, b, h, 0, 0)),
            pl.BlockSpec((2, 1, 2 * half), lambda b, h: (0, 0, 0)),
            pl.BlockSpec((2, 2 * half, hid), lambda b, h: (0, 0, 0)),
            pl.BlockSpec((2, hid, d), lambda b, h: (0, 0, 0)),
        ],
        out_specs=pl.BlockSpec((2, 1, 1, n_pieces, d), lambda b, h: (0, b, h, 0, 0)),
        out_shape=jax.ShapeDtypeStruct((2, B, Hk, n_pieces, d), F32),
        compiler_params=_params("parallel", "parallel"),
        name="nsa_compress",
    )(kv_flat, pos, w1, w2)


def _nsa_cmp_kernel(q_ref, kc_ref, vct_ref, ovt_ref, o_ref, sel_ref, *, t, G):
    i = pl.program_id(2)
    cols = G * t
    n_cmp = kc_ref.shape[-2]
    n_sel = ovt_ref.shape[0]
    kc = kc_ref[0, 0]
    kc_hi = kc.astype(BF16)
    kc_lo = (kc - kc_hi.astype(F32)).astype(BF16)
    q = q_ref[0, 0, 0]
    s = _dot(kc_hi, q) + _dot(kc_lo, q)
    t_pos = i * t + (lax.broadcasted_iota(jnp.int32, (n_cmp, cols), 1) & (t - 1))
    blk_end = lax.broadcasted_iota(jnp.int32, (n_cmp, cols), 0) * CMP_STRIDE + (CMP_BLOCK - 1)
    mask = blk_end <= t_pos
    s = jnp.where(mask, s, NEG_INF)
    e = jnp.exp(s - jnp.max(s, axis=0, keepdims=True))
    p = jnp.where(mask, e * (1.0 / jnp.sum(e, axis=0, keepdims=True)), 0.0)
    o_ref[0, 0, 0] = _dot(vct_ref[0, 0].astype(BF16), p.astype(BF16))

    p_group = p[:, 0:t]
    for g in range(1, G):
        p_group = p_group + p[:, g * t:(g + 1) * t]
    pg_hi = p_group.astype(BF16)
    pg_lo = (p_group - pg_hi.astype(F32)).astype(BF16)
    ovt = ovt_ref[...].astype(BF16)
    imp = _dot(ovt, pg_hi) + _dot(ovt, pg_lo)
    blk = lax.broadcasted_iota(jnp.int32, (n_sel, t), 0)
    cur = jnp.right_shift(i * t + lax.broadcasted_iota(jnp.int32, (n_sel, t), 1), SLC_SHIFT)
    imp = jnp.where(blk <= cur, jnp.where(blk == 0, FORCE_SCORE, jnp.where(blk >= cur - 1, FORCE_SCORE, imp)),
                    NEG_INF)
    rank = jnp.zeros((n_sel, t), F32)
    for c in range(n_sel):
        other = imp[c:c + 1, :]
        later = jnp.where(blk > c, 1.0, 0.0)
        rank = rank + jnp.where(other > imp, 1.0, 0.0) + jnp.where(other == imp, later, 0.0)
    sel_ref[0, 0] = jnp.where(rank < float(min(SLC_TOPK, n_sel)), 1.0, 0.0)


def _nsa_cmp(q, kc, vct, overlap_t):
    B, Hk, nq, d, cols = q.shape
    n_cmp = kc.shape[2]
    n_sel = overlap_t.shape[0]
    t = n_sel * SLC_BLOCK // nq
    G = cols // t
    o_spec = pl.BlockSpec((1, 1, 1, d, cols), lambda b, h, i: (b, h, i, 0, 0))
    return pl.pallas_call(
        functools.partial(_nsa_cmp_kernel, t=t, G=G),
        grid=(B, Hk, nq),
        in_specs=[
            o_spec,
            pl.BlockSpec((1, 1, n_cmp, d), lambda b, h, i: (b, h, 0, 0)),
            pl.BlockSpec((1, 1, d, n_cmp), lambda b, h, i: (b, h, 0, 0)),
            pl.BlockSpec((n_sel, n_cmp), lambda b, h, i: (0, 0)),
        ],
        out_specs=[o_spec, pl.BlockSpec((1, 1, n_sel, t), lambda b, h, i: (b, h, 0, i))],
        out_shape=[
            jax.ShapeDtypeStruct((B, Hk, nq, d, cols), F32),
            jax.ShapeDtypeStruct((B, Hk, n_sel, nq * t), F32),
        ],
        compiler_params=_params("parallel", "parallel", "parallel"),
        name="nsa_cmp",
    )(q, kc, vct, overlap_t)


def _nsa_combine_kernel(gl_ref, oc_ref, os_ref, ow_ref, o_ref):
    g = jax.nn.sigmoid(gl_ref[0, 0, 0])
    o_ref[0, 0, 0] = (g[0:1] * oc_ref[0, 0, 0] + g[1:2] * os_ref[0, 0, 0]
                      + g[2:3] * ow_ref[0, 0, 0]).astype(o_ref.dtype)


def _nsa_combine(gl, o_cmp, o_slc, o_win):
    B, Hk, nq, d, cols = o_cmp.shape
    o_spec = pl.BlockSpec((1, 1, 1, d, cols), lambda b, h, i: (b, h, i, 0, 0))
    return pl.pallas_call(
        _nsa_combine_kernel,
        grid=(B, Hk, nq),
        in_specs=[pl.BlockSpec((1, 1, 1, 3, cols), lambda b, h, i: (b, h, i, 0, 0)), o_spec, o_spec, o_spec],
        out_specs=o_spec,
        out_shape=jax.ShapeDtypeStruct(o_cmp.shape, BF16),
        compiler_params=_params("parallel", "parallel", "parallel"),
        name="nsa_combine",
    )(gl, o_cmp, o_slc, o_win)


def _rwkv_prep_kernel(p_ref, mu_ref, w0_ref, ww2_ref, a0_ref, aw2_ref, gw2_ref, kk_ref, ka_ref, seg_ref,
                      r_o, lw_o, k_o, v_o, kk_o, b_o, g_o, last_ref):
    tb = p_ref.shape[1]
    width = r_o.shape[-1]

    @pl.when(pl.program_id(1) == 0)
    def _():
        last_ref[...] = jnp.zeros_like(last_ref)

    p = p_ref[0]
    row = lax.broadcasted_iota(jnp.int32, p.shape, 0)
    prev = jnp.where(row == 0, last_ref[...], pltpu.roll(p, 1, 0))
    last_ref[...] = p[tb - 1:tb]
    xs = p + (prev - p) * mu_ref[...]
    r = xs[:, 0:width]
    k = xs[:, width:2 * width]
    v = xs[:, 2 * width:3 * width]
    lora = xs[:, 3 * width:3 * width + 128]
    g_lo = xs[:, 3 * width + 128:]
    lw = -math.exp(-0.5) * jax.nn.sigmoid(w0_ref[...] + _dot(jnp.tanh(lora), ww2_ref[...], HI))
    a = jax.nn.sigmoid(a0_ref[...] + _dot(lora, aw2_ref[...], HI))
    g = _dot(jax.nn.sigmoid(g_lo), gw2_ref[...], HI)
    kk = k * kk_ref[...]
    norm = jnp.sqrt(_dot(kk * kk, seg_ref[...], HI))
    kk = kk / jnp.maximum(norm, 1e-12)
    r_o[0] = r
    lw_o[0] = lw
    k_o[0] = k * (1.0 + (a - 1.0) * ka_ref[...])
    v_o[0] = v
    kk_o[0] = kk
    b_o[0] = kk * a
    g_o[0] = g


def _rwkv_prep(p, consts, tb=256):
    B, S, P = p.shape
    width = consts[1].shape[-1]
    tb = _tile(S, tb)
    const_specs = [pl.BlockSpec(c.shape, lambda b, t: (0, 0)) for c in consts]
    o_spec = pl.BlockSpec((1, tb, width), lambda b, t: (b, t, 0))
    return pl.pallas_call(
        _rwkv_prep_kernel,
        grid=(B, S // tb),
        in_specs=[pl.BlockSpec((1, tb, P), lambda b, t: (b, t, 0))] + const_specs,
        out_specs=[o_spec] * 7,
        out_shape=[jax.ShapeDtypeStruct((B, S, width), F32)] * 7,
        scratch_shapes=[pltpu.VMEM((1, P), F32)],
        compiler_params=_params("parallel", "arbitrary"),
        name="rwkv_prep",
    )(p, *consts)


def _cumsum_rows(x):
    n = x.shape[0]
    row = lax.broadcasted_iota(jnp.int32, x.shape, 0)
    step = 1
    while step < n:
        x = x + jnp.where(row >= step, pltpu.roll(x, step, 0), 0.0)
        step *= 2
    return x


def _seg_sum(x, seg):
    rows = x.shape[0]
    hi = x.astype(BF16)
    lo = (x - hi.astype(F32)).astype(BF16)
    both = _dot(jnp.concatenate([hi, lo], axis=0), seg)
    return both[:rows] + both[rows:]


def _rwkv_group_chunk(rg, kkg, bd, kd, kc, bc, v, gamma, state, lane_masks, strict, incl, eye, chunk):
    stack = lambda x: jnp.concatenate([x * m for m in lane_masks], axis=0)
    kks, rs, vs = stack(kkg), stack(rg), stack(v)
    n, w = kks.shape
    big = _dot_nt(jnp.concatenate([kks, rs], axis=0), jnp.concatenate([stack(bd), stack(kd)], axis=0))
    a_mat = jnp.where(strict, big[:n, :n], 0.0)
    bk = jnp.where(strict, big[:n, n:], 0.0)
    ay = jnp.where(incl, big[n:, :n], 0.0)
    by = jnp.where(incl, big[n:, n:], 0.0)
    power = -a_mat
    inv = eye + power
    power = _dot(power, power)
    for _ in range(int(math.log2(chunk)) - 2):
        both = _dot(jnp.concatenate([power, inv], axis=0), power)
        power, inv = both[:n], inv + both[n:]
    inv = inv + _dot(inv, power)
    bv = _dot(jnp.concatenate([bk, by], axis=0), vs)
    tx = _dot(inv, jnp.concatenate([kks, bv[:n]], axis=1))
    ayx = _dot(ay, tx)
    q_hat = rs - ayx[:, :w]
    y_intra = bv[n:] - ayx[:, w:]
    qt = _dot_nt(jnp.concatenate([q_hat, tx[:, :w]], axis=0), state)
    ys = qt[:n] + y_intra
    us = qt[n:] + tx[:, w:]
    new_state = state * gamma + _dot_tn(jnp.concatenate([vs, us], axis=0),
                                        jnp.concatenate([stack(kc), -stack(bc)], axis=0))
    y = ys[:chunk]
    for h in range(1, len(lane_masks)):
        y = y + ys[h * chunk:(h + 1) * chunk]
    return y, new_state


def _rwkv_scan_kernel(r_ref, lw_ref, k_ref, v_ref, kk_ref, b_ref, g_ref, lnw_ref, lnb_ref, rk_ref, seg_ref,
                      o_ref, state_ref, *, heads, chunk, group):
    tb, width = r_ref.shape[1], r_ref.shape[2]
    n = width // heads
    groups, gw, gn = heads // group, group * n, group * chunk

    @pl.when(pl.program_id(1) == 0)
    def _():
        state_ref[...] = jnp.zeros_like(state_ref)

    lane = lax.broadcasted_iota(jnp.int32, (1, gw), 1)
    lane_masks = [jnp.where(jnp.logical_and(lane >= h * n, lane < (h + 1) * n), 1.0, 0.0) for h in range(group)]
    row_id = lax.broadcasted_iota(jnp.int32, (gn, gn), 0)
    col_id = lax.broadcasted_iota(jnp.int32, (gn, gn), 1)
    row, col = row_id & (chunk - 1), col_id & (chunk - 1)
    strict, incl = col < row, col <= row
    eye = jnp.where(row_id == col_id, 1.0, 0.0)
    states = [state_ref[p] for p in range(groups)]
    y_chunks = []
    for c in range(tb // chunk):
        rows = pl.ds(c * chunk, chunk)
        r, lw, k, v = r_ref[0, rows, :], lw_ref[0, rows, :], k_ref[0, rows, :], v_ref[0, rows, :]
        kk, bb = kk_ref[0, rows, :], b_ref[0, rows, :]
        g = _cumsum_rows(lw)
        g_last = g[chunk - 1:chunk]
        decay_out = jnp.exp(-g)
        decay_end = jnp.exp(g_last - g)
        rg, kkg = r * jnp.exp(g), kk * jnp.exp(g - lw)
        bd, kd = bb * decay_out, k * decay_out
        kc, bc = k * decay_end, bb * decay_end
        gamma = jnp.exp(g_last)
        ys = []
        for p in range(groups):
            sl = slice(p * gw, (p + 1) * gw)
            y, states[p] = _rwkv_group_chunk(rg[:, sl], kkg[:, sl], bd[:, sl], kd[:, sl], kc[:, sl], bc[:, sl],
                                             v[:, sl], gamma[:, sl], states[p], lane_masks, strict, incl, eye, chunk)
            ys.append(y)
        y_chunks.append(jnp.concatenate(ys, axis=1))
    for p in range(groups):
        state_ref[p] = states[p]
    y = jnp.concatenate(y_chunks, axis=0)
    seg = seg_ref[...]
    r, k, v = r_ref[0], k_ref[0], v_ref[0]
    sums = _seg_sum(jnp.concatenate([y, r * k * rk_ref[...]], axis=0), seg)
    dev = y - sums[:tb] * (1.0 / n)
    var = _seg_sum(dev * dev, seg) * (1.0 / n)
    yn = dev * lax.rsqrt(var + RWKV_GN_EPS) * lnw_ref[...] + lnb_ref[...]
    o_ref[0] = ((yn + sums[tb:] * v) * g_ref[0]).astype(o_ref.dtype)


def _rwkv_scan(r, lw, k, v, kk, bb, g, ln_w, ln_b, r_k, seg, tb=256, group=4):
    B, S, width = r.shape
    heads = RWKV_HEADS
    n = width // heads
    chunk = min(RWKV_CHUNK, S)
    assert chunk == n and heads % group == 0
    tb = _tile(S, tb)
    x_spec = pl.BlockSpec((1, tb, width), lambda b, t: (b, t, 0))
    vec_spec = pl.BlockSpec((1, width), lambda b, t: (0, 0))
    return pl.pallas_call(
        functools.partial(_rwkv_scan_kernel, heads=heads, chunk=chunk, group=group),
        grid=(B, S // tb),
        in_specs=[x_spec] * 7 + [vec_spec] * 3 + [pl.BlockSpec((width, width), lambda b, t: (0, 0))],
        out_specs=x_spec,
        out_shape=jax.ShapeDtypeStruct((B, S, width), BF16),
        scratch_shapes=[pltpu.VMEM((heads // group, group * n, group * n), F32)],
        compiler_params=_params("parallel", "arbitrary"),
        name="rwkv_scan",
    )(r, lw, k, v, kk, bb, g, ln_w, ln_b, r_k, seg.astype(BF16))


def _gla_kernel(q_ref, k_ref, v_ref, r_ref, al_ref, aw_ref, ab_ref, ng_ref, o_ref, state_ref,
                *, heads, chunk):
    tb = v_ref.shape[1]
    dk = q_ref.shape[-1]
    dv = v_ref.shape[-1] // heads

    @pl.when(pl.program_id(1) == 0)
    def _():
        state_ref[...] = jnp.zeros_like(state_ref)

    row = lax.broadcasted_iota(jnp.int32, (chunk, chunk), 0)
    col = lax.broadcasted_iota(jnp.int32, (chunk, chunk), 1)
    incl = col <= row
    states = [state_ref[h] for h in range(heads)]
    for c in range(tb // chunk):
        rows = pl.ds(c * chunk, chunk)
        x = _dot(al_ref[0, rows, :], aw_ref[...], HI) + ab_ref[...]
        log_a = (jnp.minimum(x, 0.0) - jnp.log(1.0 + jnp.exp(-jnp.abs(x)))) * (1.0 / GLA_TAU)
        b_all = _cumsum_rows(log_a)
        outs = []
        for h in range(heads):
            cols = pl.ds(h * dv, dv)
            b = b_all[:, h * dk:(h + 1) * dk]
            b_last = b[chunk - 1:chunk]
            q = q_ref[0, h, rows, :]
            k = k_ref[0, h, rows, :]
            v = v_ref[0, rows, cols]
            qe = q * (dk ** -0.5) * jnp.exp(b)
            attn = jnp.where(incl, _dot_nt(qe, k * jnp.exp(-b)), 0.0)
            o = _dot(attn, v) + _dot_nt(qe, states[h])
            states[h] = states[h] * jnp.exp(b_last) + _dot_tn(v, k * jnp.exp(b_last - b))
            outs.append(o * lax.rsqrt(jnp.mean(o * o, axis=-1, keepdims=True) + NORM_EPS))
        gate = r_ref[0, rows, :]
        o_ref[0, rows, :] = (jnp.concatenate(outs, axis=1) * ng_ref[...]
                             * (gate * jax.nn.sigmoid(gate))).astype(o_ref.dtype)
    for h in range(heads):
        state_ref[h] = states[h]


def _gla(q, k, v, r, alpha_lo, aw, ab, ng, tb=256):
    B, H, S, dk = q.shape
    width = v.shape[-1]
    lora = alpha_lo.shape[-1]
    chunk = min(GLA_CHUNK, S)
    tb = _tile(S, tb)
    qk_spec = pl.BlockSpec((1, H, tb, dk), lambda b, t: (b, 0, t, 0))
    x_spec = pl.BlockSpec((1, tb, width), lambda b, t: (b, t, 0))
    return pl.pallas_call(
        functools.partial(_gla_kernel, heads=H, chunk=chunk),
        grid=(B, S // tb),
        in_specs=[
            qk_spec, qk_spec, x_spec, x_spec,
            pl.BlockSpec((1, tb, lora), lambda b, t: (b, t, 0)),
            pl.BlockSpec(aw.shape, lambda b, t: (0, 0)),
            pl.BlockSpec(ab.shape, lambda b, t: (0, 0)),
            pl.BlockSpec(ng.shape, lambda b, t: (0, 0)),
        ],
        out_specs=x_spec,
        out_shape=jax.ShapeDtypeStruct((B, S, width), BF16),
        scratch_shapes=[pltpu.VMEM((H, width // H, dk), F32)],
        compiler_params=_params("parallel", "arbitrary"),
        name="gla",
    )(q, k, v, r, alpha_lo, aw, ab, ng)


def _rope_tables(S, d):
    inv = ROPE_THETA ** (-jnp.arange(0, d, 2, dtype=F32) / d)
    ang = jnp.arange(S).astype(F32)[:, None] * inv[None, :]
    cos, sin = jnp.cos(ang), jnp.sin(ang)
    return jnp.concatenate([cos, cos], axis=-1), jnp.concatenate([sin, sin], axis=-1)


def _rot_cols(w, d):
    K = w.shape[0]
    w = w.reshape(K, -1, 2, d // 2)
    return jnp.concatenate([-w[:, :, 1], w[:, :, 0]], axis=-1).reshape(K, -1)


def _nsa_constants(S):
    n_pieces = S // CMP_STRIDE
    n_cmp = (S - CMP_BLOCK) // CMP_STRIDE + 1
    n_sel = S // SLC_BLOCK
    start = np.arange(n_cmp)[:, None] * CMP_STRIDE
    end = start + CMP_BLOCK - 1
    blk = np.arange(n_sel)[None, :]
    overlap = np.zeros((n_pieces, n_sel), np.float32)
    overlap[:n_cmp] = ((start < (blk + 1) * SLC_BLOCK) & (end >= blk * SLC_BLOCK)).astype(np.float32)
    return jnp.asarray(overlap.T)


def _token_mixing(u, B, S, w_in, w_branch, cmp_pos, cmp_w1, cmp_w2, rwkv_mu, rwkv_w0, rwkv_w_w2, rwkv_a0,
                  rwkv_a_w2, rwkv_g_w2, rwkv_k_k, rwkv_k_a, rwkv_r_k, rwkv_ln_w, rwkv_ln_b, gla_alpha_w2,
                  gla_alpha_b, gla_norm_g, mla_q_norm, mla_w_uq, mla_kv_norm, mla_w_ukv):
    M, D = u.shape
    d = HEAD_DIM
    width = D // 4
    sizes = [width + 6 * NSA_KV_HEADS * d + 3 * NSA_HEADS, 3 * width + 256, 2 * (width // 2) + 2 * width + 16,
             384 + 128 + MLA_ROPE, 4 * D]
    offs = np.cumsum([0] + sizes)
    assert offs[-1] == w_in.shape[1]
    w_nsa, w_rwkv, w_gla, w_mla, w_gate = [w_in[:, offs[i]:offs[i + 1]] for i in range(5)]
    zeros = lambda n: jnp.zeros((D, n), F32)
    kvw = NSA_KV_HEADS * d

    wq = w_nsa[:, :width]
    wkc, wvc, wks, wvs, wkw, wvw = [w_nsa[:, width + i * kvw:width + (i + 1) * kvw] for i in range(6)]
    wgl = w_nsa[:, width + 6 * kvw:]
    wkr = w_mla[:, 512:512 + MLA_ROPE]
    n_rope = width + 3 * kvw + MLA_ROPE
    pad = (-n_rope) % 512
    w_a = jnp.concatenate([wq, wkc, wks, wkw, wkr, zeros(pad)], axis=1)
    w_b = jnp.concatenate([_rot_cols(wq, d), _rot_cols(wkc, d), _rot_cols(wks, d), _rot_cols(wkw, d),
                           _rot_cols(wkr, MLA_ROPE), zeros(pad)], axis=1)
    cos_d, sin_d = _rope_tables(S, d)
    cos_r, sin_r = _rope_tables(S, MLA_ROPE)
    reps = (width + 3 * kvw) // d
    cos_t = jnp.concatenate([jnp.tile(cos_d, (1, reps)), cos_r, jnp.ones((S, pad), F32)], axis=1)
    sin_t = jnp.concatenate([jnp.tile(sin_d, (1, reps)), sin_r, jnp.zeros((S, pad), F32)], axis=1)
    roped = _matmul(u, w_a, S=S, w_rot=w_b, cos=cos_t, sin=sin_t, name="proj_rope").reshape(B, S, -1)
    n_plain = 3 * kvw + 3 * NSA_HEADS
    plain = _matmul(u, jnp.concatenate([wvc, wvs, wvw, wgl, zeros((-n_plain) % 128)], axis=1), S=S,
                    name="proj_nsa_v")

    Hk, G = NSA_KV_HEADS, NSA_GROUP
    t_att = _tile(S, 256)
    nq = S // t_att
    cols_layout = lambda x, f: x.reshape(B, nq, t_att, Hk, G, f).transpose(0, 3, 1, 5, 4, 2).reshape(
        B, Hk, nq, f, G * t_att)
    kv_heads = lambda x: x.reshape(B, S, Hk, d).transpose(0, 2, 1, 3)
    v_tiles = lambda x: x.astype(BF16).reshape(B, nq, t_att, Hk, d).transpose(0, 3, 1, 4, 2)
    q_att = cols_layout((roped[..., :width] * d ** -0.5).astype(BF16), d)
    kc_in = kv_heads(roped[..., width:width + kvw])
    ks = kv_heads(roped[..., width + kvw:width + 2 * kvw].astype(BF16))
    kw = kv_heads(roped[..., width + 2 * kvw:width + 3 * kvw].astype(BF16))
    k_pe = roped[..., width + 3 * kvw:width + 3 * kvw + MLA_ROPE]
    plain3 = plain.reshape(B, S, -1)
    vc_in = kv_heads(plain3[..., :kvw])
    vs, vw = v_tiles(plain3[..., kvw:2 * kvw]), v_tiles(plain3[..., 2 * kvw:3 * kvw])
    gate_logits = cols_layout(plain3[..., 3 * kvw:3 * kvw + 3 * NSA_HEADS], 3)

    overlap_t = _nsa_constants(S)
    n_pieces = S // CMP_STRIDE
    kv_flat = jnp.stack([kc_in, vc_in]).reshape(2, B, Hk, n_pieces, CMP_STRIDE * d)
    cmp = _nsa_compress(kv_flat, cmp_pos.reshape(2, 1, CMP_BLOCK * d), cmp_w1, cmp_w2)
    o_cmp, sel = _nsa_cmp(q_att, cmp[0], cmp[1].transpose(0, 1, 3, 2), overlap_t)
    o_slc = _flash(q_att, ks, vs, mode="select", sel=sel)
    o_win = _flash(q_att, kw, vw, mode="window")
    y_nsa = _nsa_combine(gate_logits, o_cmp, o_slc, o_win)
    y_nsa = y_nsa.reshape(B, Hk, nq, d, G, t_att).transpose(0, 2, 5, 1, 4, 3).reshape(M, width)

    regroup = lambda t: jnp.concatenate(
        [t[..., 0:width], t[..., width + 64:2 * width + 64], t[..., 2 * width + 64:3 * width + 64],
         t[..., width:width + 64], t[..., 3 * width + 64:3 * width + 128], t[..., 3 * width + 128:]], axis=-1)
    p_rwkv = _matmul(u, regroup(w_rwkv), S=S, tn=896, name="proj_rwkv").reshape(B, S, -1)
    row = lambda t: t.reshape(1, -1)
    seg = jnp.asarray(np.kron(np.eye(RWKV_HEADS), np.ones((d, d))).astype(np.float32))
    lora_pad = jnp.zeros((64, width), F32)
    consts = [row(regroup(rwkv_mu)), row(rwkv_w0), jnp.concatenate([rwkv_w_w2, lora_pad], axis=0), row(rwkv_a0),
              jnp.concatenate([lora_pad, rwkv_a_w2], axis=0), rwkv_g_w2, row(rwkv_k_k), row(rwkv_k_a), seg]
    r, lw, k2, v, kk, bb, g = _rwkv_prep(p_rwkv, consts)
    y_rwkv = _rwkv_scan(r, lw, k2, v, kk, bb, g, row(rwkv_ln_w), row(rwkv_ln_b), row(rwkv_r_k), seg)
    y_rwkv = y_rwkv.reshape(M, width)

    dk_all = width // 2
    w_gla_p = jnp.concatenate([w_gla[:, :2 * dk_all + width], w_gla[:, 2 * dk_all + width + 16:],
                               w_gla[:, 2 * dk_all + width:2 * dk_all + width + 16], zeros(240)], axis=1)
    p_gla = _matmul(u, w_gla_p, S=S, tn=896, name="proj_gla").reshape(B, S, -1)
    dkh = dk_all // GLA_HEADS
    gla_heads = lambda t: t.reshape(B, S, GLA_HEADS, dkh).transpose(0, 2, 1, 3)
    y_gla = _gla(gla_heads(p_gla[..., :dk_all]), gla_heads(p_gla[..., dk_all:2 * dk_all]),
                 p_gla[..., 2 * dk_all:2 * dk_all + width], p_gla[..., 2 * dk_all + width:2 * dk_all + 2 * width],
                 p_gla[..., 2 * dk_all + 2 * width:2 * dk_all + 2 * width + 16],
                 gla_alpha_w2, row(gla_alpha_b), row(gla_norm_g)).reshape(M, width)

    c_q = _matmul(u, w_mla[:, :384], S=S, name="proj_mla_q")
    c_kv = _matmul(u, w_mla[:, 384:512], S=S, name="proj_mla_kv")
    H = MLA_HEADS
    wq3 = mla_w_uq.reshape(384, H, MLA_NOPE + MLA_ROPE)
    rp = wq3[:, :, MLA_NOPE:]
    half = MLA_ROPE // 2
    z = lambda n: jnp.zeros((384, H, n), F32)
    wq_a = jnp.concatenate([wq3, z(128 - MLA_NOPE - MLA_ROPE)], axis=-1).reshape(384, H * 128)
    wq_b = jnp.concatenate([z(MLA_NOPE), -rp[..., half:], rp[..., :half], z(128 - MLA_NOPE - MLA_ROPE)],
                           axis=-1).reshape(384, H * 128)
    tail = 128 - MLA_NOPE - MLA_ROPE
    cos_h = jnp.concatenate([jnp.ones((S, MLA_NOPE), F32), cos_r, jnp.ones((S, tail), F32)], axis=1)
    sin_h = jnp.concatenate([jnp.zeros((S, MLA_NOPE), F32), sin_r, jnp.zeros((S, tail), F32)], axis=1)
    q_m = _matmul(c_q, wq_a, S=S, g=mla_q_norm, w_rot=wq_b, cos=jnp.tile(cos_h, (1, H)), sin=jnp.tile(sin_h, (1, H)),
                  name="mla_q")
    t_mla = _tile(S, 512)
    nq_m = S // t_mla
    tiles_m = lambda x, f: x.reshape(B, nq_m, t_mla, H, f).transpose(0, 3, 1, 4, 2)
    q_m = tiles_m((q_m * (MLA_NOPE + MLA_ROPE) ** -0.5).astype(BF16), 128)
    kv_m = _matmul(c_kv, mla_w_ukv, S=S, g=mla_kv_norm, name="mla_kv").reshape(B, S, H, MLA_NOPE + width // H)
    k_m = jnp.concatenate([kv_m[..., :MLA_NOPE], jnp.broadcast_to(k_pe[:, :, None, :], (B, S, H, MLA_ROPE)),
                           jnp.zeros((B, S, H, tail), F32)], axis=-1).astype(BF16).transpose(0, 2, 1, 3)
    v_m = tiles_m(kv_m[..., MLA_NOPE:].astype(BF16), width // H)
    o_mla = _flash(q_m, k_m, v_m, mode="causal", out_dtype=BF16)
    y_mla = o_mla.transpose(0, 2, 4, 1, 3).reshape(M, width)

    return _merge(u, w_gate, (y_nsa, y_rwkv, y_gla, y_mla), w_branch)


def kernel(x, c, ada_w, ada_b, pre_g, post_g, ffn_wg, ffn_wu, ffn_wd, mix_w_in, mix_w_branch, mix_w_out, nsa_cmp_pos, nsa_cmp_w1, nsa_cmp_w2, rwkv_mu, rwkv_w0, rwkv_w_w2, rwkv_a0, rwkv_a_w2, rwkv_g_w2, rwkv_k_k, rwkv_k_a, rwkv_r_k, rwkv_ln_w, rwkv_ln_b, gla_alpha_w2, gla_alpha_b, gla_norm_g, mla_q_norm, mla_w_uq, mla_kv_norm, mla_w_ukv):
    B, S, D = x.shape
    M = B * S
    depth = ada_w.shape[0]
    mod = _modulation(c, ada_w, ada_b)
    shifts, scales, gates = mod[:, :, :D], mod[:, :, D:2 * D], mod[:, :, 2 * D:]
    x2 = x.reshape(M, D)
    n_sub = 3 * depth
    u = _prenorm(x2, pre_g[0, 0], scales[0], shifts[0], S)
    for s in range(n_sub):
        l, i = divmod(s, 3)
        nxt = s + 1
        next_norm = (pre_g[nxt // 3, nxt % 3], scales[nxt], shifts[nxt]) if nxt < n_sub else None
        if i == 1:
            y = _token_mixing(
                u, B, S, mix_w_in[l], mix_w_branch[l], nsa_cmp_pos[l], nsa_cmp_w1[l], nsa_cmp_w2[l], rwkv_mu[l],
                rwkv_w0[l], rwkv_w_w2[l], rwkv_a0[l], rwkv_a_w2[l], rwkv_g_w2[l], rwkv_k_k[l], rwkv_k_a[l],
                rwkv_r_k[l], rwkv_ln_w[l], rwkv_ln_b[l], gla_alpha_w2[l], gla_alpha_b[l], gla_norm_g[l],
                mla_q_norm[l], mla_w_uq[l], mla_kv_norm[l], mla_w_ukv[l])
            out = _down_post(y, mix_w_out[l], x2, post_g[l, i], gates[s], 1.0, S, next_norm)
        else:
            j = 0 if i == 0 else 1
            h = _ffn_up(u, ffn_wg[l, j], ffn_wu[l, j])
            out = _down_post(h, ffn_wd[l, j], x2, post_g[l, i], gates[s], 0.5, S, next_norm)
        x2, u = out if next_norm is not None else (out, None)
    return x2.reshape(B, S, D)
```

```python
import functools
import math

import jax
import jax.numpy as jnp
import numpy as np
from jax import lax
from jax.experimental import pallas as pl
from jax.experimental.pallas import tpu as pltpu

F32 = jnp.float32
BF16 = jnp.bfloat16
HI = lax.Precision.HIGHEST

NORM_EPS = 1e-6
NEG_INF = -1e30
FORCE_SCORE = 1e30
ROPE_THETA = 10000.0
HEAD_DIM = 64

NSA_HEADS = 8
NSA_KV_HEADS = 2
NSA_GROUP = NSA_HEADS // NSA_KV_HEADS
CMP_BLOCK = 32
CMP_STRIDE = 16
SLC_BLOCK = 64
SLC_SHIFT = 6
SLC_TOPK = 16
WIN_SIZE = 512
RWKV_HEADS = 8
RWKV_GN_EPS = 64e-5
RWKV_CHUNK = 64
GLA_HEADS = 4
GLA_TAU = 16.0
GLA_CHUNK = 64
MLA_HEADS = 8
MLA_NOPE = 64
MLA_ROPE = 32

VMEM_LIMIT_BYTES = 56 * 1024 * 1024


def _params(*sem):
    return pltpu.CompilerParams(dimension_semantics=sem, vmem_limit_bytes=VMEM_LIMIT_BYTES)


def _dot(a, b, precision=None):
    return jnp.dot(a, b, preferred_element_type=F32, precision=precision)


def _dot_nt(a, b, precision=None):
    return lax.dot_general(a, b, (((1,), (1,)), ((), ())), preferred_element_type=F32, precision=precision)


def _dot_tn(a, b, precision=None):
    return lax.dot_general(a, b, (((0,), (0,)), ((), ())), preferred_element_type=F32, precision=precision)


def _rms(x):
    return x * lax.rsqrt(jnp.mean(x * x, axis=-1, keepdims=True) + NORM_EPS)


def _tile(n, t):
    t = min(n, t)
    assert n % t == 0, (n, t)
    return t


def _mod_kernel(c_ref, w_ref, b_ref, o_ref):
    c = c_ref[...]
    act = c * jax.nn.sigmoid(c)
    hi = act.astype(BF16)
    lo = (act - hi.astype(F32)).astype(BF16)
    w = w_ref[0].astype(BF16)
    o_ref[0] = _dot(hi, w) + _dot(lo, w) + b_ref[0]


def _modulation(c, ada_w, ada_b):
    B, D = c.shape
    n_sub = ada_w.shape[0] * ada_w.shape[1]
    w = ada_w.reshape(n_sub, D, 3 * D)
    b = ada_b.reshape(n_sub, 1, 3 * D)
    rows = 8
    cp = jnp.zeros((rows, D), F32).at[:B].set(c)
    tn = _tile(3 * D, 512)
    out = pl.pallas_call(
        _mod_kernel,
        grid=(n_sub, 3 * D // tn),
        in_specs=[
            pl.BlockSpec((rows, D), lambda s, j: (0, 0)),
            pl.BlockSpec((1, D, tn), lambda s, j: (s, 0, j)),
            pl.BlockSpec((1, 1, tn), lambda s, j: (s, 0, j)),
        ],
        out_specs=pl.BlockSpec((1, rows, tn), lambda s, j: (s, 0, j)),
        out_shape=jax.ShapeDtypeStruct((n_sub, rows, 3 * D), F32),
        compiler_params=_params("parallel", "parallel"),
        name="modulation",
    )(cp, w, b)
    return out[:, :B]


def _prenorm_kernel(x_ref, g_ref, sc_ref, sh_ref, o_ref):
    y = _rms(x_ref[...]) * g_ref[...]
    o_ref[...] = (y * (1.0 + sc_ref[0]) + sh_ref[0]).astype(o_ref.dtype)


def _prenorm(x2, g, scale, shift, S):
    M, D = x2.shape
    B = M // S
    tm = _tile(S, 512)
    per_b = S // tm
    return pl.pallas_call(
        _prenorm_kernel,
        grid=(M // tm,),
        in_specs=[
            pl.BlockSpec((tm, D), lambda i: (i, 0)),
            pl.BlockSpec((1, D), lambda i: (0, 0)),
            pl.BlockSpec((1, 1, D), lambda i: (i // per_b, 0, 0)),
            pl.BlockSpec((1, 1, D), lambda i: (i // per_b, 0, 0)),
        ],
        out_specs=pl.BlockSpec((tm, D), lambda i: (i, 0)),
        out_shape=jax.ShapeDtypeStruct((M, D), BF16),
        compiler_params=_params("parallel"),
        name="prenorm",
    )(x2, g.reshape(1, D), scale.reshape(B, 1, D), shift.reshape(B, 1, D))


def _mm_kernel(*refs, norm, rope):
    it = iter(refs)
    a_ref = next(it)
    g_ref = next(it) if norm else None
    w_ref = next(it)
    if rope:
        w2_ref, cos_ref, sin_ref = next(it), next(it), next(it)
    o_ref = next(it)
    a = a_ref[...]
    if norm:
        a = _rms(a.astype(F32)) * g_ref[...]
    a = a.astype(BF16)
    out = _dot(a, w_ref[...])
    if rope:
        out = out * cos_ref[...] + _dot(a, w2_ref[...]) * sin_ref[...]
    o_ref[...] = out.astype(o_ref.dtype)


def _matmul(a, w, *, S, g=None, w_rot=None, cos=None, sin=None, out_dtype=F32, tm=1024, tn=512, name="matmul"):
    M, K = a.shape
    N = w.shape[1]
    tm = _tile(S, tm)
    tn = _tile(N, tn)
    per_b = S // tm
    norm, rope = g is not None, w_rot is not None
    in_specs = [pl.BlockSpec((tm, K), lambda i, j: (i, 0))]
    args = [a]
    if norm:
        in_specs.append(pl.BlockSpec((1, K), lambda i, j: (0, 0)))
        args.append(g.reshape(1, K))
    in_specs.append(pl.BlockSpec((K, tn), lambda i, j: (0, j)))
    args.append(w.astype(BF16))
    if rope:
        in_specs += [
            pl.BlockSpec((K, tn), lambda i, j: (0, j)),
            pl.BlockSpec((tm, tn), lambda i, j: (i % per_b, j)),
            pl.BlockSpec((tm, tn), lambda i, j: (i % per_b, j)),
        ]
        args += [w_rot.astype(BF16), cos, sin]
    return pl.pallas_call(
        functools.partial(_mm_kernel, norm=norm, rope=rope),
        grid=(M // tm, N // tn),
        in_specs=in_specs,
        out_specs=pl.BlockSpec((tm, tn), lambda i, j: (i, j)),
        out_shape=jax.ShapeDtypeStruct((M, N), out_dtype),
        compiler_params=_params("parallel", "parallel"),
        name=name,
    )(*args)


def _ffn_up_kernel(u_ref, wg_ref, wu_ref, o_ref, wg_bf, wu_bf):
    @pl.when(pl.program_id(1) == 0)
    def _():
        wg_bf[...] = wg_ref[...].astype(BF16)
        wu_bf[...] = wu_ref[...].astype(BF16)

    u = u_ref[...]
    a = _dot(u, wg_bf[...])
    b = _dot(u, wu_bf[...])
    o_ref[...] = (a * jax.nn.sigmoid(a) * b).astype(o_ref.dtype)


def _ffn_up(u, wg, wu, tm=1024, tn=512):
    M, K = u.shape
    N = wg.shape[1]
    tm, tn = _tile(M, tm), _tile(N, tn)
    return pl.pallas_call(
        _ffn_up_kernel,
        grid=(N // tn, M // tm),
        in_specs=[
            pl.BlockSpec((tm, K), lambda j, i: (i, 0)),
            pl.BlockSpec((K, tn), lambda j, i: (0, j)),
            pl.BlockSpec((K, tn), lambda j, i: (0, j)),
        ],
        out_specs=pl.BlockSpec((tm, tn), lambda j, i: (i, j)),
        out_shape=jax.ShapeDtypeStruct((M, N), BF16),
        scratch_shapes=[pltpu.VMEM((K, tn), BF16), pltpu.VMEM((K, tn), BF16)],
        compiler_params=_params("parallel", "arbitrary"),
        name="ffn_up",
    )(u, wg, wu)


def _cast_kernel(x_ref, o_ref):
    o_ref[...] = x_ref[...].astype(o_ref.dtype)


def _to_bf16(w, rows=512):
    R, C = w.shape
    rows = _tile(R, rows)
    return pl.pallas_call(
        _cast_kernel,
        grid=(R // rows,),
        in_specs=[pl.BlockSpec((rows, C), lambda i: (i, 0))],
        out_specs=pl.BlockSpec((rows, C), lambda i: (i, 0)),
        out_shape=jax.ShapeDtypeStruct((R, C), BF16),
        compiler_params=_params("parallel"),
        name="cast_bf16",
    )(w)


def _down_post_kernel(h_ref, w_ref, x_ref, g_ref, gate_ref, *rest, res_w, with_next):
    y = _rms(_dot(h_ref[...], w_ref[...])) * g_ref[...]
    x_new = x_ref[...] + res_w * gate_ref[0] * y
    if with_next:
        gn_ref, sc_ref, sh_ref, o_ref, u_ref = rest
        u_ref[...] = (_rms(x_new) * gn_ref[...] * (1.0 + sc_ref[0]) + sh_ref[0]).astype(u_ref.dtype)
    else:
        (o_ref,) = rest
    o_ref[...] = x_new


def _down_post(h, wd, x2, g, gate, res_w, S, next_norm=None, tm=256):
    M, K = h.shape
    D = wd.shape[1]
    B = M // S
    tm = _tile(S, tm)
    per_b = S // tm
    row_spec = pl.BlockSpec((tm, D), lambda i: (i, 0))
    vec_spec = pl.BlockSpec((1, D), lambda i: (0, 0))
    batch_spec = pl.BlockSpec((1, 1, D), lambda i: (i // per_b, 0, 0))
    in_specs = [
        pl.BlockSpec((tm, K), lambda i: (i, 0)),
        pl.BlockSpec((K, D), lambda i: (0, 0), pipeline_mode=pl.Buffered(1)),
        row_spec, vec_spec, batch_spec,
    ]
    args = [h, _to_bf16(wd), x2, g.reshape(1, D), gate.reshape(B, 1, D)]
    out_specs, out_shape = row_spec, jax.ShapeDtypeStruct((M, D), F32)
    if next_norm is not None:
        g_next, scale_next, shift_next = next_norm
        in_specs += [vec_spec, batch_spec, batch_spec]
        args += [g_next.reshape(1, D), scale_next.reshape(B, 1, D), shift_next.reshape(B, 1, D)]
        out_specs, out_shape = [row_spec, row_spec], [out_shape, jax.ShapeDtypeStruct((M, D), BF16)]
    return pl.pallas_call(
        functools.partial(_down_post_kernel, res_w=res_w, with_next=next_norm is not None),
        grid=(M // tm,),
        in_specs=in_specs,
        out_specs=out_specs,
        out_shape=out_shape,
        compiler_params=_params("parallel"),
        name="down_post",
    )(*args)


def _merge_kernel(*refs, tn, lane_off):
    u_ref = refs[0]
    gate_refs = refs[1:9]
    y_refs = refs[9:13]
    wb_ref, o_ref, wg_bf, wb_bf = refs[13:]

    @pl.when(pl.program_id(1) == 0)
    def _():
        for b in range(4):
            both = jnp.concatenate([gate_refs[2 * b][...], gate_refs[2 * b + 1][...]], axis=1)
            wg_bf[b] = both[:, lane_off:lane_off + tn].astype(BF16)
        wb_bf[...] = wb_ref[...].astype(BF16)

    u = u_ref[...]
    acc = None
    for b, y_ref in enumerate(y_refs):
        gate = jax.nn.sigmoid(_dot(u, wg_bf[b]))
        term = gate * _dot(y_ref[...], wb_bf[b])
        acc = term if acc is None else acc + term
    o_ref[...] = acc.astype(o_ref.dtype)


def _merge(u, w_in, gate_col0, ys, w_branch, tm=512, tn=256):
    M, D = u.shape
    W = w_branch.shape[1]
    tm, tn = _tile(M, tm), _tile(D, tn)
    per_branch = D // tn
    first_blk, lane_off = divmod(gate_col0, tn)
    y_spec = pl.BlockSpec((tm, W), lambda j, i: (i, 0))
    gate_specs = [pl.BlockSpec((D, tn), lambda j, i, b=b, e=e: (0, first_blk + b * per_branch + j + e))
                  for b in range(4) for e in range(2)]
    return pl.pallas_call(
        functools.partial(_merge_kernel, tn=tn, lane_off=lane_off),
        grid=(D // tn, M // tm),
        in_specs=[pl.BlockSpec((tm, D), lambda j, i: (i, 0))] + gate_specs + [y_spec] * 4 + [
            pl.BlockSpec((4, W, tn), lambda j, i: (0, 0, j))],
        out_specs=pl.BlockSpec((tm, tn), lambda j, i: (i, j)),
        out_shape=jax.ShapeDtypeStruct((M, D), BF16),
        scratch_shapes=[pltpu.VMEM((4, D, tn), BF16), pltpu.VMEM((4, W, tn), BF16)],
        compiler_params=_params("parallel", "arbitrary"),
        name="merge",
    )(u, *([w_in] * 8), *ys, w_branch)


def _flash_kernel(*refs, mode, t, G):
    if mode == "select":
        q_ref, k_ref, v_ref, sel_ref, o_ref, m_ref, l_ref, acc_ref = refs
    else:
        q_ref, k_ref, v_ref, o_ref, m_ref, l_ref, acc_ref = refs
    i = pl.program_id(2)
    cols = G * t
    q = q_ref[0, 0, 0]
    k_off = lax.broadcasted_iota(jnp.int32, (t, cols), 0)
    q_off = lax.broadcasted_iota(jnp.int32, (t, cols), 1) & (t - 1)
    blocks_per_tile = t // SLC_BLOCK

    def scores(j, mask):
        start = pl.multiple_of(j * t, t)
        s = _dot(k_ref[0, 0, pl.ds(start, t), :], q)
        if mode == "select":
            parts = []
            for kb in range(blocks_per_tile):
                picked = sel_ref[0, 0, pl.ds(j * blocks_per_tile + kb, 1), :]
                picked = jnp.concatenate([picked] * G, axis=1) if G > 1 else picked
                parts.append(jnp.where(picked > 0.5, s[kb * SLC_BLOCK:(kb + 1) * SLC_BLOCK], NEG_INF))
            s = jnp.concatenate(parts, axis=0)
        if mask is not None:
            s = jnp.where(mask, s, NEG_INF)
        return s, v_ref[0, 0, j]

    s, v = scores(i, k_off <= q_off)
    m = jnp.max(s, axis=0, keepdims=True)
    p = jnp.exp(s - m)
    m_ref[...] = m
    l_ref[...] = jnp.sum(p, axis=0, keepdims=True)
    acc_ref[...] = _dot(v, p.astype(BF16))

    def accumulate(tiles, mask=None):
        parts = [scores(j, mask) for j in tiles]
        s = jnp.concatenate([part[0] for part in parts], axis=0) if len(parts) > 1 else parts[0][0]
        v = jnp.concatenate([part[1] for part in parts], axis=1) if len(parts) > 1 else parts[0][1]
        m_prev = m_ref[...]
        m_new = jnp.maximum(m_prev, jnp.max(s, axis=0, keepdims=True))
        p = jnp.exp(s - m_new)
        alpha = jnp.exp(m_prev - m_new)
        l_ref[...] = alpha * l_ref[...] + jnp.sum(p, axis=0, keepdims=True)
        acc_ref[...] = alpha * acc_ref[...] + _dot(v, p.astype(BF16))
        m_ref[...] = m_new

    if mode == "window":
        n_back = WIN_SIZE // t
        for back in range(1, n_back + 1):
            mask = (k_off > q_off) if back == n_back else None

            @pl.when(i >= back)
            def _(back=back, mask=mask):
                accumulate([i - back], mask)
    else:
        def body(pair, carry):
            accumulate([2 * pair, 2 * pair + 1])
            return carry

        lax.fori_loop(0, jnp.right_shift(i, 1), body, 0)

        @pl.when((i & 1) == 1)
        def _():
            accumulate([i - 1])

    o_ref[0, 0, 0] = (acc_ref[...] * (1.0 / l_ref[...])).astype(o_ref.dtype)


def _flash(q, k, v, *, mode, sel=None, out_dtype=F32):
    B, Hk, nq, Dq, cols = q.shape
    S = k.shape[2]
    Dv, t = v.shape[3], v.shape[4]
    G = cols // t
    assert t & (t - 1) == 0 and t % SLC_BLOCK == 0 and nq * t == S
    if mode == "window":
        assert WIN_SIZE % t == 0
    in_specs = [
        pl.BlockSpec((1, 1, 1, Dq, cols), lambda b, h, i: (b, h, i, 0, 0)),
        pl.BlockSpec((1, 1, S, Dq), lambda b, h, i: (b, h, 0, 0)),
        pl.BlockSpec((1, 1, nq, Dv, t), lambda b, h, i: (b, h, 0, 0, 0)),
    ]
    args = [q, k, v]
    if mode == "select":
        in_specs.append(pl.BlockSpec((1, 1, sel.shape[2], t), lambda b, h, i: (b, h, 0, i)))
        args.append(sel)
    return pl.pallas_call(
        functools.partial(_flash_kernel, mode=mode, t=t, G=G),
        grid=(B, Hk, nq),
        in_specs=in_specs,
        out_specs=pl.BlockSpec((1, 1, 1, Dv, cols), lambda b, h, i: (b, h, i, 0, 0)),
        out_shape=jax.ShapeDtypeStruct((B, Hk, nq, Dv, cols), out_dtype),
        scratch_shapes=[pltpu.VMEM((1, cols), F32), pltpu.VMEM((1, cols), F32), pltpu.VMEM((Dv, cols), F32)],
        compiler_params=_params("parallel", "parallel", "arbitrary"),
        name="attn_" + mode,
    )(*args)


def _nsa_compress_kernel(t_ref, pos_ref, w1_ref, w2_ref, o_ref):
    n_pieces, half = t_ref.shape[-2], t_ref.shape[-1]
    for which in range(2):
        t = t_ref[which, 0, 0]
        w1 = w1_ref[which]
        first = _dot(t, w1[:half], HI)
        second = _dot(t, w1[half:], HI)
        pos_bias = _dot(pos_ref[which], w1, HI)
        pre = first + pltpu.roll(second, n_pieces - 1, 0) + pos_bias
        o_ref[which, 0, 0] = _dot(jax.nn.gelu(pre), w2_ref[which], HI)


def _nsa_compress(kv_flat, pos, w1, w2):
    _, B, Hk, n_pieces, half = kv_flat.shape
    hid, d = w2.shape[1], w2.shape[2]
    return pl.pallas_call(
        _nsa_compress_kernel,
        grid=(B, Hk),
        in_specs=[
            pl.BlockSpec((2, 1, 1, n_pieces, half), lambda b, h: (0, b, h, 0, 0)),
            pl.BlockSpec((2, 1, 2 * half), lambda b, h: (0, 0, 0)),
            pl.BlockSpec((2, 2 * half, hid), lambda b, h: (0, 0, 0)),
            pl.BlockSpec((2, hid, d), lambda b, h: (0, 0, 0)),
        ],
        out_specs=pl.BlockSpec((2, 1, 1, n_pieces, d), lambda b, h: (0, b, h, 0, 0)),
        out_shape=jax.ShapeDtypeStruct((2, B, Hk, n_pieces, d), F32),
        compiler_params=_params("parallel", "parallel"),
        name="nsa_compress",
    )(kv_flat, pos, w1, w2)


def _nsa_cmp_kernel(q_ref, kc_ref, vct_ref, ovt_ref, o_ref, sel_ref, *, t, G):
    i = pl.program_id(2)
    cols = G * t
    n_cmp = kc_ref.shape[-2]
    n_sel = ovt_ref.shape[0]
    kc = kc_ref[0, 0]
    kc_hi = kc.astype(BF16)
    kc_lo = (kc - kc_hi.astype(F32)).astype(BF16)
    q = q_ref[0, 0, 0]
    s = _dot(kc_hi, q) + _dot(kc_lo, q)
    t_pos = i * t + (lax.broadcasted_iota(jnp.int32, (n_cmp, cols), 1) & (t - 1))
    blk_end = lax.broadcasted_iota(jnp.int32, (n_cmp, cols), 0) * CMP_STRIDE + (CMP_BLOCK - 1)
    mask = blk_end <= t_pos
    s = jnp.where(mask, s, NEG_INF)
    e = jnp.exp(s - jnp.max(s, axis=0, keepdims=True))
    p = jnp.where(mask, e * (1.0 / jnp.sum(e, axis=0, keepdims=True)), 0.0)
    o_ref[0, 0, 0] = _dot(vct_ref[0, 0].astype(BF16), p.astype(BF16))

    p_group = p[:, 0:t]
    for g in range(1, G):
        p_group = p_group + p[:, g * t:(g + 1) * t]
    pg_hi = p_group.astype(BF16)
    pg_lo = (p_group - pg_hi.astype(F32)).astype(BF16)
    ovt = ovt_ref[...].astype(BF16)
    imp = _dot(ovt, pg_hi) + _dot(ovt, pg_lo)
    blk = lax.broadcasted_iota(jnp.int32, (n_sel, t), 0)
    cur = jnp.right_shift(i * t + lax.broadcasted_iota(jnp.int32, (n_sel, t), 1), SLC_SHIFT)
    imp = jnp.where(blk <= cur, jnp.where(blk == 0, FORCE_SCORE, jnp.where(blk >= cur - 1, FORCE_SCORE, imp)),
                    NEG_INF)
    rank = jnp.zeros((n_sel, t), F32)
    for c in range(n_sel):
        other = imp[c:c + 1, :]
        later = jnp.where(blk > c, 1.0, 0.0)
        rank = rank + jnp.where(other > imp, 1.0, 0.0) + jnp.where(other == imp, later, 0.0)
    sel_ref[0, 0] = jnp.where(rank < float(min(SLC_TOPK, n_sel)), 1.0, 0.0)


def _nsa_cmp(q, kc, vct, overlap_t):
    B, Hk, nq, d, cols = q.shape
    n_cmp = kc.shape[2]
    n_sel = overlap_t.shape[0]
    t = n_sel * SLC_BLOCK // nq
    G = cols // t
    o_spec = pl.BlockSpec((1, 1, 1, d, cols), lambda b, h, i: (b, h, i, 0, 0))
    return pl.pallas_call(
        functools.partial(_nsa_cmp_kernel, t=t, G=G),
        grid=(B, Hk, nq),
        in_specs=[
            o_spec,
            pl.BlockSpec((1, 1, n_cmp, d), lambda b, h, i: (b, h, 0, 0)),
            pl.BlockSpec((1, 1, d, n_cmp), lambda b, h, i: (b, h, 0, 0)),
            pl.BlockSpec((n_sel, n_cmp), lambda b, h, i: (0, 0)),
        ],
        out_specs=[o_spec, pl.BlockSpec((1, 1, n_sel, t), lambda b, h, i: (b, h, 0, i))],
        out_shape=[
            jax.ShapeDtypeStruct((B, Hk, nq, d, cols), F32),
            jax.ShapeDtypeStruct((B, Hk, n_sel, nq * t), F32),
        ],
        compiler_params=_params("parallel", "parallel", "parallel"),
        name="nsa_cmp",
    )(q, kc, vct, overlap_t)


def _nsa_combine_kernel(gl_ref, oc_ref, os_ref, ow_ref, o_ref):
    g = jax.nn.sigmoid(gl_ref[0, 0, 0])
    o_ref[0, 0, 0] = (g[0:1] * oc_ref[0, 0, 0] + g[1:2] * os_ref[0, 0, 0]
                      + g[2:3] * ow_ref[0, 0, 0]).astype(o_ref.dtype)


def _nsa_combine(gl, o_cmp, o_slc, o_win):
    B, Hk, nq, d, cols = o_cmp.shape
    o_spec = pl.BlockSpec((1, 1, 1, d, cols), lambda b, h, i: (b, h, i, 0, 0))
    return pl.pallas_call(
        _nsa_combine_kernel,
        grid=(B, Hk, nq),
        in_specs=[pl.BlockSpec((1, 1, 1, 3, cols), lambda b, h, i: (b, h, i, 0, 0)), o_spec, o_spec, o_spec],
        out_specs=o_spec,
        out_shape=jax.ShapeDtypeStruct(o_cmp.shape, BF16),
        compiler_params=_params("parallel", "parallel", "parallel"),
        name="nsa_combine",
    )(gl, o_cmp, o_slc, o_win)


def _rwkv_prep_kernel(p_ref, mu_ref, w0_ref, ww2_ref, a0_ref, aw2_ref, gw2_ref, kk_ref, ka_ref, seg_ref,
                      r_o, lw_o, k_o, v_o, kk_o, b_o, g_o, last_ref):
    tb = p_ref.shape[1]
    width = r_o.shape[-1]

    @pl.when(pl.program_id(1) == 0)
    def _():
        last_ref[...] = jnp.zeros_like(last_ref)

    p = p_ref[0]
    row = lax.broadcasted_iota(jnp.int32, p.shape, 0)
    prev = jnp.where(row == 0, last_ref[...], pltpu.roll(p, 1, 0))
    last_ref[...] = p[tb - 1:tb]
    xs = p + (prev - p) * mu_ref[...]
    r = xs[:, 0:width]
    k = xs[:, width:2 * width]
    v = xs[:, 2 * width:3 * width]
    lora = xs[:, 3 * width:3 * width + 128]
    g_lo = xs[:, 3 * width + 128:]
    lw = -math.exp(-0.5) * jax.nn.sigmoid(w0_ref[...] + _dot(jnp.tanh(lora), ww2_ref[...], HI))
    a = jax.nn.sigmoid(a0_ref[...] + _dot(lora, aw2_ref[...], HI))
    g = _dot(jax.nn.sigmoid(g_lo), gw2_ref[...], HI)
    kk = k * kk_ref[...]
    norm = jnp.sqrt(_dot(kk * kk, seg_ref[...], HI))
    kk = kk / jnp.maximum(norm, 1e-12)
    r_o[0] = r
    lw_o[0] = lw
    k_o[0] = k * (1.0 + (a - 1.0) * ka_ref[...])
    v_o[0] = v
    kk_o[0] = kk
    b_o[0] = kk * a
    g_o[0] = g


def _rwkv_prep(p, consts, tb=256):
    B, S, P = p.shape
    width = consts[1].shape[-1]
    tb = _tile(S, tb)
    const_specs = [pl.BlockSpec(c.shape, lambda b, t: (0, 0)) for c in consts]
    o_spec = pl.BlockSpec((1, tb, width), lambda b, t: (b, t, 0))
    return pl.pallas_call(
        _rwkv_prep_kernel,
        grid=(B, S // tb),
        in_specs=[pl.BlockSpec((1, tb, P), lambda b, t: (b, t, 0))] + const_specs,
        out_specs=[o_spec] * 7,
        out_shape=[jax.ShapeDtypeStruct((B, S, width), F32)] * 7,
        scratch_shapes=[pltpu.VMEM((1, P), F32)],
        compiler_params=_params("parallel", "arbitrary"),
        name="rwkv_prep",
    )(p, *consts)


def _cumsum_rows(x):
    n = x.shape[0]
    row = lax.broadcasted_iota(jnp.int32, x.shape, 0)
    step = 1
    while step < n:
        x = x + jnp.where(row >= step, pltpu.roll(x, step, 0), 0.0)
        step *= 2
    return x


def _seg_sum(x, seg):
    rows = x.shape[0]
    hi = x.astype(BF16)
    lo = (x - hi.astype(F32)).astype(BF16)
    both = _dot(jnp.concatenate([hi, lo], axis=0), seg)
    return both[:rows] + both[rows:]


def _rwkv_group_chunk(rg, kkg, bd, kd, kc, bc, v, gamma, state, lane_masks, strict, incl, eye, chunk):
    stack = lambda x: jnp.concatenate([x * m for m in lane_masks], axis=0)
    kks, rs, vs = stack(kkg), stack(rg), stack(v)
    n, w = kks.shape
    big = _dot_nt(jnp.concatenate([kks, rs], axis=0), jnp.concatenate([stack(bd), stack(kd)], axis=0))
    a_mat = jnp.where(strict, big[:n, :n], 0.0)
    bk = jnp.where(strict, big[:n, n:], 0.0)
    ay = jnp.where(incl, big[n:, :n], 0.0)
    by = jnp.where(incl, big[n:, n:], 0.0)
    power = -a_mat
    inv = eye + power
    power = _dot(power, power)
    for _ in range(int(math.log2(chunk)) - 2):
        both = _dot(jnp.concatenate([power, inv], axis=0), power)
        power, inv = both[:n], inv + both[n:]
    inv = inv + _dot(inv, power)
    bv = _dot(jnp.concatenate([bk, by], axis=0), vs)
    tx = _dot(inv, jnp.concatenate([kks, bv[:n]], axis=1))
    ayx = _dot(ay, tx)
    q_hat = rs - ayx[:, :w]
    y_intra = bv[n:] - ayx[:, w:]
    qt = _dot_nt(jnp.concatenate([q_hat, tx[:, :w]], axis=0), state)
    ys = qt[:n] + y_intra
    us = qt[n:] + tx[:, w:]
    new_state = state * gamma + _dot_tn(jnp.concatenate([vs, us], axis=0),
                                        jnp.concatenate([stack(kc), -stack(bc)], axis=0))
    y = ys[:chunk]
    for h in range(1, len(lane_masks)):
        y = y + ys[h * chunk:(h + 1) * chunk]
    return y, new_state


def _rwkv_scan_kernel(r_ref, lw_ref, k_ref, v_ref, kk_ref, b_ref, g_ref, lnw_ref, lnb_ref, rk_ref, seg_ref,
                      o_ref, state_ref, *, heads, chunk, group):
    tb, width = r_ref.shape[1], r_ref.shape[2]
    n = width // heads
    groups, gw, gn = heads // group, group * n, group * chunk

    @pl.when(pl.program_id(1) == 0)
    def _():
        state_ref[...] = jnp.zeros_like(state_ref)

    lane = lax.broadcasted_iota(jnp.int32, (1, gw), 1)
    lane_masks = [jnp.where(jnp.logical_and(lane >= h * n, lane < (h + 1) * n), 1.0, 0.0) for h in range(group)]
    row_id = lax.broadcasted_iota(jnp.int32, (gn, gn), 0)
    col_id = lax.broadcasted_iota(jnp.int32, (gn, gn), 1)
    row, col = row_id & (chunk - 1), col_id & (chunk - 1)
    strict, incl = col < row, col <= row
    eye = jnp.where(row_id == col_id, 1.0, 0.0)
    states = [state_ref[p] for p in range(groups)]
    y_chunks = []
    for c in range(tb // chunk):
        rows = pl.ds(c * chunk, chunk)
        r, lw, k, v = r_ref[0, rows, :], lw_ref[0, rows, :], k_ref[0, rows, :], v_ref[0, rows, :]
        kk, bb = kk_ref[0, rows, :], b_ref[0, rows, :]
        g = _cumsum_rows(lw)
        g_last = g[chunk - 1:chunk]
        decay_out = jnp.exp(-g)
        decay_end = jnp.exp(g_last - g)
        rg, kkg = r * jnp.exp(g), kk * jnp.exp(g - lw)
        bd, kd = bb * decay_out, k * decay_out
        kc, bc = k * decay_end, bb * decay_end
        gamma = jnp.exp(g_last)
        ys = []
        for p in range(groups):
            sl = slice(p * gw, (p + 1) * gw)
            y, states[p] = _rwkv_group_chunk(rg[:, sl], kkg[:, sl], bd[:, sl], kd[:, sl], kc[:, sl], bc[:, sl],
                                             v[:, sl], gamma[:, sl], states[p], lane_masks, strict, incl, eye, chunk)
            ys.append(y)
        y_chunks.append(jnp.concatenate(ys, axis=1))
    for p in range(groups):
        state_ref[p] = states[p]
    y = jnp.concatenate(y_chunks, axis=0)
    seg = seg_ref[...]
    r, k, v = r_ref[0], k_ref[0], v_ref[0]
    sums = _seg_sum(jnp.concatenate([y, r * k * rk_ref[...]], axis=0), seg)
    dev = y - sums[:tb] * (1.0 / n)
    var = _seg_sum(dev * dev, seg) * (1.0 / n)
    yn = dev * lax.rsqrt(var + RWKV_GN_EPS) * lnw_ref[...] + lnb_ref[...]
    o_ref[0] = ((yn + sums[tb:] * v) * g_ref[0]).astype(o_ref.dtype)


def _rwkv_scan(r, lw, k, v, kk, bb, g, ln_w, ln_b, r_k, seg, tb=256, group=4):
    B, S, width = r.shape
    heads = RWKV_HEADS
    n = width // heads
    chunk = min(RWKV_CHUNK, S)
    assert chunk == n and heads % group == 0
    tb = _tile(S, tb)
    x_spec = pl.BlockSpec((1, tb, width), lambda b, t: (b, t, 0))
    vec_spec = pl.BlockSpec((1, width), lambda b, t: (0, 0))
    return pl.pallas_call(
        functools.partial(_rwkv_scan_kernel, heads=heads, chunk=chunk, group=group),
        grid=(B, S // tb),
        in_specs=[x_spec] * 7 + [vec_spec] * 3 + [pl.BlockSpec((width, width), lambda b, t: (0, 0))],
        out_specs=x_spec,
        out_shape=jax.ShapeDtypeStruct((B, S, width), BF16),
        scratch_shapes=[pltpu.VMEM((heads // group, group * n, group * n), F32)],
        compiler_params=_params("parallel", "arbitrary"),
        name="rwkv_scan",
    )(r, lw, k, v, kk, bb, g, ln_w, ln_b, r_k, seg.astype(BF16))


def _gla_kernel(q_ref, k_ref, v_ref, r_ref, al_ref, aw_ref, ab_ref, ng_ref, o_ref, state_ref,
                *, heads, chunk):
    tb = v_ref.shape[1]
    dk = q_ref.shape[-1]
    dv = v_ref.shape[-1] // heads

    @pl.when(pl.program_id(1) == 0)
    def _():
        state_ref[...] = jnp.zeros_like(state_ref)

    row = lax.broadcasted_iota(jnp.int32, (chunk, chunk), 0)
    col = lax.broadcasted_iota(jnp.int32, (chunk, chunk), 1)
    incl = col <= row
    states = [state_ref[h] for h in range(heads)]
    for c in range(tb // chunk):
        rows = pl.ds(c * chunk, chunk)
        x = _dot(al_ref[0, rows, :], aw_ref[...], HI) + ab_ref[...]
        log_a = (jnp.minimum(x, 0.0) - jnp.log(1.0 + jnp.exp(-jnp.abs(x)))) * (1.0 / GLA_TAU)
        b_all = _cumsum_rows(log_a)
        outs = []
        for h in range(heads):
            cols = pl.ds(h * dv, dv)
            b = b_all[:, h * dk:(h + 1) * dk]
            b_last = b[chunk - 1:chunk]
            q = q_ref[0, h, rows, :]
            k = k_ref[0, h, rows, :]
            v = v_ref[0, rows, cols]
            qe = q * (dk ** -0.5) * jnp.exp(b)
            attn = jnp.where(incl, _dot_nt(qe, k * jnp.exp(-b)), 0.0)
            o = _dot(attn, v) + _dot_nt(qe, states[h])
            states[h] = states[h] * jnp.exp(b_last) + _dot_tn(v, k * jnp.exp(b_last - b))
            outs.append(o * lax.rsqrt(jnp.mean(o * o, axis=-1, keepdims=True) + NORM_EPS))
        gate = r_ref[0, rows, :]
        o_ref[0, rows, :] = (jnp.concatenate(outs, axis=1) * ng_ref[...]
                             * (gate * jax.nn.sigmoid(gate))).astype(o_ref.dtype)
    for h in range(heads):
        state_ref[h] = states[h]


def _gla(q, k, v, r, alpha_lo, aw, ab, ng, tb=256):
    B, H, S, dk = q.shape
    width = v.shape[-1]
    lora = alpha_lo.shape[-1]
    chunk = min(GLA_CHUNK, S)
    tb = _tile(S, tb)
    qk_spec = pl.BlockSpec((1, H, tb, dk), lambda b, t: (b, 0, t, 0))
    x_spec = pl.BlockSpec((1, tb, width), lambda b, t: (b, t, 0))
    return pl.pallas_call(
        functools.partial(_gla_kernel, heads=H, chunk=chunk),
        grid=(B, S // tb),
        in_specs=[
            qk_spec, qk_spec, x_spec, x_spec,
            pl.BlockSpec((1, tb, lora), lambda b, t: (b, t, 0)),
            pl.BlockSpec(aw.shape, lambda b, t: (0, 0)),
            pl.BlockSpec(ab.shape, lambda b, t: (0, 0)),
            pl.BlockSpec(ng.shape, lambda b, t: (0, 0)),
        ],
        out_specs=x_spec,
        out_shape=jax.ShapeDtypeStruct((B, S, width), BF16),
        scratch_shapes=[pltpu.VMEM((H, width // H, dk), F32)],
        compiler_params=_params("parallel", "arbitrary"),
        name="gla",
    )(q, k, v, r, alpha_lo, aw, ab, ng)


def _rope_tables(S, d):
    inv = ROPE_THETA ** (-jnp.arange(0, d, 2, dtype=F32) / d)
    ang = jnp.arange(S).astype(F32)[:, None] * inv[None, :]
    cos, sin = jnp.cos(ang), jnp.sin(ang)
    return jnp.concatenate([cos, cos], axis=-1), jnp.concatenate([sin, sin], axis=-1)


def _rot_cols(w, d):
    K = w.shape[0]
    w = w.reshape(K, -1, 2, d // 2)
    return jnp.concatenate([-w[:, :, 1], w[:, :, 0]], axis=-1).reshape(K, -1)


def _nsa_constants(S):
    n_pieces = S // CMP_STRIDE
    n_cmp = (S - CMP_BLOCK) // CMP_STRIDE + 1
    n_sel = S // SLC_BLOCK
    start = np.arange(n_cmp)[:, None] * CMP_STRIDE
    end = start + CMP_BLOCK - 1
    blk = np.arange(n_sel)[None, :]
    overlap = np.zeros((n_pieces, n_sel), np.float32)
    overlap[:n_cmp] = ((start < (blk + 1) * SLC_BLOCK) & (end >= blk * SLC_BLOCK)).astype(np.float32)
    return jnp.asarray(overlap.T)


def _token_mixing(u, B, S, w_in, w_branch, cmp_pos, cmp_w1, cmp_w2, rwkv_mu, rwkv_w0, rwkv_w_w2, rwkv_a0,
                  rwkv_a_w2, rwkv_g_w2, rwkv_k_k, rwkv_k_a, rwkv_r_k, rwkv_ln_w, rwkv_ln_b, gla_alpha_w2,
                  gla_alpha_b, gla_norm_g, mla_q_norm, mla_w_uq, mla_kv_norm, mla_w_ukv):
    M, D = u.shape
    d = HEAD_DIM
    width = D // 4
    sizes = [width + 6 * NSA_KV_HEADS * d + 3 * NSA_HEADS, 3 * width + 256, 2 * (width // 2) + 2 * width + 16,
             384 + 128 + MLA_ROPE, 4 * D]
    offs = np.cumsum([0] + sizes)
    assert offs[-1] == w_in.shape[1]
    w_nsa, w_rwkv, w_gla, w_mla, w_gate = [w_in[:, offs[i]:offs[i + 1]] for i in range(5)]
    zeros = lambda n: jnp.zeros((D, n), F32)
    kvw = NSA_KV_HEADS * d

    wq = w_nsa[:, :width]
    wkc, wvc, wks, wvs, wkw, wvw = [w_nsa[:, width + i * kvw:width + (i + 1) * kvw] for i in range(6)]
    wgl = w_nsa[:, width + 6 * kvw:]
    wkr = w_mla[:, 512:512 + MLA_ROPE]
    n_rope = width + 3 * kvw + MLA_ROPE
    pad = (-n_rope) % 512
    w_a = jnp.concatenate([wq, wkc, wks, wkw, wkr, zeros(pad)], axis=1)
    w_b = jnp.concatenate([_rot_cols(wq, d), _rot_cols(wkc, d), _rot_cols(wks, d), _rot_cols(wkw, d),
                           _rot_cols(wkr, MLA_ROPE), zeros(pad)], axis=1)
    cos_d, sin_d = _rope_tables(S, d)
    cos_r, sin_r = _rope_tables(S, MLA_ROPE)
    reps = (width + 3 * kvw) // d
    cos_t = jnp.concatenate([jnp.tile(cos_d, (1, reps)), cos_r, jnp.ones((S, pad), F32)], axis=1)
    sin_t = jnp.concatenate([jnp.tile(sin_d, (1, reps)), sin_r, jnp.zeros((S, pad), F32)], axis=1)
    roped = _matmul(u, w_a, S=S, w_rot=w_b, cos=cos_t, sin=sin_t, name="proj_rope").reshape(B, S, -1)
    n_plain = 3 * kvw + 3 * NSA_HEADS
    plain = _matmul(u, jnp.concatenate([wvc, wvs, wvw, wgl, zeros((-n_plain) % 128)], axis=1), S=S,
                    name="proj_nsa_v")

    Hk, G = NSA_KV_HEADS, NSA_GROUP
    t_att = _tile(S, 256)
    nq = S // t_att
    cols_layout = lambda x, f: x.reshape(B, nq, t_att, Hk, G, f).transpose(0, 3, 1, 5, 4, 2).reshape(
        B, Hk, nq, f, G * t_att)
    kv_heads = lambda x: x.reshape(B, S, Hk, d).transpose(0, 2, 1, 3)
    v_tiles = lambda x: x.astype(BF16).reshape(B, nq, t_att, Hk, d).transpose(0, 3, 1, 4, 2)
    q_att = cols_layout((roped[..., :width] * d ** -0.5).astype(BF16), d)
    kc_in = kv_heads(roped[..., width:width + kvw])
    ks = kv_heads(roped[..., width + kvw:width + 2 * kvw].astype(BF16))
    kw = kv_heads(roped[..., width + 2 * kvw:width + 3 * kvw].astype(BF16))
    k_pe = roped[..., width + 3 * kvw:width + 3 * kvw + MLA_ROPE]
    plain3 = plain.reshape(B, S, -1)
    vc_in = kv_heads(plain3[..., :kvw])
    vs, vw = v_tiles(plain3[..., kvw:2 * kvw]), v_tiles(plain3[..., 2 * kvw:3 * kvw])
    gate_logits = cols_layout(plain3[..., 3 * kvw:3 * kvw + 3 * NSA_HEADS], 3)

    overlap_t = _nsa_constants(S)
    n_pieces = S // CMP_STRIDE
    kv_flat = jnp.stack([kc_in, vc_in]).reshape(2, B, Hk, n_pieces, CMP_STRIDE * d)
    cmp = _nsa_compress(kv_flat, cmp_pos.reshape(2, 1, CMP_BLOCK * d), cmp_w1, cmp_w2)
    o_cmp, sel = _nsa_cmp(q_att, cmp[0], cmp[1].transpose(0, 1, 3, 2), overlap_t)
    o_slc = _flash(q_att, ks, vs, mode="select", sel=sel)
    o_win = _flash(q_att, kw, vw, mode="window")
    y_nsa = _nsa_combine(gate_logits, o_cmp, o_slc, o_win)
    y_nsa = y_nsa.reshape(B, Hk, nq, d, G, t_att).transpose(0, 2, 5, 1, 4, 3).reshape(M, width)

    regroup = lambda t: jnp.concatenate(
        [t[..., 0:width], t[..., width + 64:2 * width + 64], t[..., 2 * width + 64:3 * width + 64],
         t[..., width:width + 64], t[..., 3 * width + 64:3 * width + 128], t[..., 3 * width + 128:]], axis=-1)
    p_rwkv = _matmul(u, regroup(w_rwkv), S=S, tn=896, name="proj_rwkv").reshape(B, S, -1)
    row = lambda t: t.reshape(1, -1)
    seg = jnp.asarray(np.kron(np.eye(RWKV_HEADS), np.ones((d, d))).astype(np.float32))
    lora_pad = jnp.zeros((64, width), F32)
    consts = [row(regroup(rwkv_mu)), row(rwkv_w0), jnp.concatenate([rwkv_w_w2, lora_pad], axis=0), row(rwkv_a0),
              jnp.concatenate([lora_pad, rwkv_a_w2], axis=0), rwkv_g_w2, row(rwkv_k_k), row(rwkv_k_a), seg]
    r, lw, k2, v, kk, bb, g = _rwkv_prep(p_rwkv, consts)
    y_rwkv = _rwkv_scan(r, lw, k2, v, kk, bb, g, row(rwkv_ln_w), row(rwkv_ln_b), row(rwkv_r_k), seg)
    y_rwkv = y_rwkv.reshape(M, width)

    dk_all = width // 2
    w_gla_p = jnp.concatenate([w_gla[:, :2 * dk_all + width], w_gla[:, 2 * dk_all + width + 16:],
                               w_gla[:, 2 * dk_all + width:2 * dk_all + width + 16], zeros(240)], axis=1)
    p_gla = _matmul(u, w_gla_p, S=S, tn=896, name="proj_gla").reshape(B, S, -1)
    dkh = dk_all // GLA_HEADS
    gla_heads = lambda t: t.reshape(B, S, GLA_HEADS, dkh).transpose(0, 2, 1, 3)
    y_gla = _gla(gla_heads(p_gla[..., :dk_all]), gla_heads(p_gla[..., dk_all:2 * dk_all]),
                 p_gla[..., 2 * dk_all:2 * dk_all + width], p_gla[..., 2 * dk_all + width:2 * dk_all + 2 * width],
                 p_gla[..., 2 * dk_all + 2 * width:2 * dk_all + 2 * width + 16],
                 gla_alpha_w2, row(gla_alpha_b), row(gla_norm_g)).reshape(M, width)

    c_q = _matmul(u, w_mla[:, :384], S=S, name="proj_mla_q")
    c_kv = _matmul(u, w_mla[:, 384:512], S=S, name="proj_mla_kv")
    H = MLA_HEADS
    wq3 = mla_w_uq.reshape(384, H, MLA_NOPE + MLA_ROPE)
    rp = wq3[:, :, MLA_NOPE:]
    half = MLA_ROPE // 2
    z = lambda n: jnp.zeros((384, H, n), F32)
    wq_a = jnp.concatenate([wq3, z(128 - MLA_NOPE - MLA_ROPE)], axis=-1).reshape(384, H * 128)
    wq_b = jnp.concatenate([z(MLA_NOPE), -rp[..., half:], rp[..., :half], z(128 - MLA_NOPE - MLA_ROPE)],
                           axis=-1).reshape(384, H * 128)
    tail = 128 - MLA_NOPE - MLA_ROPE
    cos_h = jnp.concatenate([jnp.ones((S, MLA_NOPE), F32), cos_r, jnp.ones((S, tail), F32)], axis=1)
    sin_h = jnp.concatenate([jnp.zeros((S, MLA_NOPE), F32), sin_r, jnp.zeros((S, tail), F32)], axis=1)
    q_m = _matmul(c_q, wq_a, S=S, g=mla_q_norm, w_rot=wq_b, cos=jnp.tile(cos_h, (1, H)), sin=jnp.tile(sin_h, (1, H)),
                  name="mla_q")
    t_mla = _tile(S, 512)
    nq_m = S // t_mla
    tiles_m = lambda x, f: x.reshape(B, nq_m, t_mla, H, f).transpose(0, 3, 1, 4, 2)
    q_m = tiles_m((q_m * (MLA_NOPE + MLA_ROPE) ** -0.5).astype(BF16), 128)
    kv_m = _matmul(c_kv, mla_w_ukv, S=S, g=mla_kv_norm, name="mla_kv").reshape(B, S, H, MLA_NOPE + width // H)
    k_m = jnp.concatenate([kv_m[..., :MLA_NOPE], jnp.broadcast_to(k_pe[:, :, None, :], (B, S, H, MLA_ROPE)),
                           jnp.zeros((B, S, H, tail), F32)], axis=-1).astype(BF16).transpose(0, 2, 1, 3)
    v_m = tiles_m(kv_m[..., MLA_NOPE:].astype(BF16), width // H)
    o_mla = _flash(q_m, k_m, v_m, mode="causal", out_dtype=BF16)
    y_mla = o_mla.transpose(0, 2, 4, 1, 3).reshape(M, width)

    return _merge(u, w_in, int(offs[4]), (y_nsa, y_rwkv, y_gla, y_mla), w_branch)


def kernel(x, c, ada_w, ada_b, pre_g, post_g, ffn_wg, ffn_wu, ffn_wd, mix_w_in, mix_w_branch, mix_w_out, nsa_cmp_pos, nsa_cmp_w1, nsa_cmp_w2, rwkv_mu, rwkv_w0, rwkv_w_w2, rwkv_a0, rwkv_a_w2, rwkv_g_w2, rwkv_k_k, rwkv_k_a, rwkv_r_k, rwkv_ln_w, rwkv_ln_b, gla_alpha_w2, gla_alpha_b, gla_norm_g, mla_q_norm, mla_w_uq, mla_kv_norm, mla_w_ukv):
    B, S, D = x.shape
    M = B * S
    depth = ada_w.shape[0]
    mod = _modulation(c, ada_w, ada_b)
    shifts, scales, gates = mod[:, :, :D], mod[:, :, D:2 * D], mod[:, :, 2 * D:]
    x2 = x.reshape(M, D)
    n_sub = 3 * depth
    u = _prenorm(x2, pre_g[0, 0], scales[0], shifts[0], S)
    for s in range(n_sub):
        l, i = divmod(s, 3)
        nxt = s + 1
        next_norm = (pre_g[nxt // 3, nxt % 3], scales[nxt], shifts[nxt]) if nxt < n_sub else None
        if i == 1:
            y = _token_mixing(
                u, B, S, mix_w_in[l], mix_w_branch[l], nsa_cmp_pos[l], nsa_cmp_w1[l], nsa_cmp_w2[l], rwkv_mu[l],
                rwkv_w0[l], rwkv_w_w2[l], rwkv_a0[l], rwkv_a_w2[l], rwkv_g_w2[l], rwkv_k_k[l], rwkv_k_a[l],
                rwkv_r_k[l], rwkv_ln_w[l], rwkv_ln_b[l], gla_alpha_w2[l], gla_alpha_b[l], gla_norm_g[l],
                mla_q_norm[l], mla_w_uq[l], mla_kv_norm[l], mla_w_ukv[l])
            out = _down_post(y, mix_w_out[l], x2, post_g[l, i], gates[s], 1.0, S, next_norm)
        else:
            j = 0 if i == 0 else 1
            h = _ffn_up(u, ffn_wg[l, j], ffn_wu[l, j])
            out = _down_post(h, ffn_wd[l, j], x2, post_g[l, i], gates[s], 0.5, S, next_norm)
        x2, u = out if next_norm is not None else (out, None)
    return x2.reshape(B, S, D)
```

```python
import functools
import math

import jax
import jax.numpy as jnp
import numpy as np
from jax import lax
from jax.experimental import pallas as pl
from jax.experimental.pallas import tpu as pltpu

F32 = jnp.float32
BF16 = jnp.bfloat16
HI = lax.Precision.HIGHEST

NORM_EPS = 1e-6
NEG_INF = -1e30
FORCE_SCORE = 1e30
ROPE_THETA = 10000.0
HEAD_DIM = 64

NSA_HEADS = 8
NSA_KV_HEADS = 2
NSA_GROUP = NSA_HEADS // NSA_KV_HEADS
CMP_BLOCK = 32
CMP_STRIDE = 16
SLC_BLOCK = 64
SLC_SHIFT = 6
SLC_TOPK = 16
WIN_SIZE = 512
RWKV_HEADS = 8
RWKV_GN_EPS = 64e-5
RWKV_CHUNK = 64
GLA_HEADS = 4
GLA_TAU = 16.0
GLA_CHUNK = 64
MLA_HEADS = 8
MLA_NOPE = 64
MLA_ROPE = 32

VMEM_LIMIT_BYTES = 56 * 1024 * 1024


def _params(*sem):
    return pltpu.CompilerParams(dimension_semantics=sem, vmem_limit_bytes=VMEM_LIMIT_BYTES)


def _dot(a, b, precision=None):
    return jnp.dot(a, b, preferred_element_type=F32, precision=precision)


def _dot_nt(a, b, precision=None):
    return lax.dot_general(a, b, (((1,), (1,)), ((), ())), preferred_element_type=F32, precision=precision)


def _dot_tn(a, b, precision=None):
    return lax.dot_general(a, b, (((0,), (0,)), ((), ())), preferred_element_type=F32, precision=precision)


def _rms(x):
    return x * lax.rsqrt(jnp.mean(x * x, axis=-1, keepdims=True) + NORM_EPS)


def _tile(n, t):
    t = min(n, t)
    assert n % t == 0, (n, t)
    return t


def _mod_kernel(c_ref, w_ref, b_ref, o_ref):
    c = c_ref[...]
    act = c * jax.nn.sigmoid(c)
    hi = act.astype(BF16)
    lo = (act - hi.astype(F32)).astype(BF16)
    w = w_ref[0].astype(BF16)
    o_ref[0] = _dot(hi, w) + _dot(lo, w) + b_ref[0]


def _modulation(c, ada_w, ada_b):
    B, D = c.shape
    n_sub = ada_w.shape[0] * ada_w.shape[1]
    w = ada_w.reshape(n_sub, D, 3 * D)
    b = ada_b.reshape(n_sub, 1, 3 * D)
    rows = 8
    cp = jnp.zeros((rows, D), F32).at[:B].set(c)
    tn = _tile(3 * D, 512)
    out = pl.pallas_call(
        _mod_kernel,
        grid=(n_sub, 3 * D // tn),
        in_specs=[
            pl.BlockSpec((rows, D), lambda s, j: (0, 0)),
            pl.BlockSpec((1, D, tn), lambda s, j: (s, 0, j)),
            pl.BlockSpec((1, 1, tn), lambda s, j: (s, 0, j)),
        ],
        out_specs=pl.BlockSpec((1, rows, tn), lambda s, j: (s, 0, j)),
        out_shape=jax.ShapeDtypeStruct((n_sub, rows, 3 * D), F32),
        compiler_params=_params("parallel", "parallel"),
        name="modulation",
    )(cp, w, b)
    return out[:, :B]


def _prenorm_kernel(x_ref, g_ref, sc_ref, sh_ref, o_ref):
    y = _rms(x_ref[...]) * g_ref[...]
    o_ref[...] = (y * (1.0 + sc_ref[0]) + sh_ref[0]).astype(o_ref.dtype)


def _prenorm(x2, g, scale, shift, S):
    M, D = x2.shape
    B = M // S
    tm = _tile(S, 512)
    per_b = S // tm
    return pl.pallas_call(
        _prenorm_kernel,
        grid=(M // tm,),
        in_specs=[
            pl.BlockSpec((tm, D), lambda i: (i, 0)),
            pl.BlockSpec((1, D), lambda i: (0, 0)),
            pl.BlockSpec((1, 1, D), lambda i: (i // per_b, 0, 0)),
            pl.BlockSpec((1, 1, D), lambda i: (i // per_b, 0, 0)),
        ],
        out_specs=pl.BlockSpec((tm, D), lambda i: (i, 0)),
        out_shape=jax.ShapeDtypeStruct((M, D), BF16),
        compiler_params=_params("parallel"),
        name="prenorm",
    )(x2, g.reshape(1, D), scale.reshape(B, 1, D), shift.reshape(B, 1, D))


def _mm_kernel(*refs, norm, rope):
    it = iter(refs)
    a_ref = next(it)
    g_ref = next(it) if norm else None
    w_ref = next(it)
    if rope:
        w2_ref, cos_ref, sin_ref = next(it), next(it), next(it)
    o_ref = next(it)
    a = a_ref[...]
    if norm:
        a = _rms(a.astype(F32)) * g_ref[...]
    a = a.astype(BF16)
    out = _dot(a, w_ref[...])
    if rope:
        out = out * cos_ref[...] + _dot(a, w2_ref[...]) * sin_ref[...]
    o_ref[...] = out.astype(o_ref.dtype)


def _matmul(a, w, *, S, g=None, w_rot=None, cos=None, sin=None, out_dtype=F32, tm=1024, tn=512, name="matmul"):
    M, K = a.shape
    N = w.shape[1]
    tm = _tile(S, tm)
    tn = _tile(N, tn)
    per_b = S // tm
    norm, rope = g is not None, w_rot is not None
    in_specs = [pl.BlockSpec((tm, K), lambda i, j: (i, 0))]
    args = [a]
    if norm:
        in_specs.append(pl.BlockSpec((1, K), lambda i, j: (0, 0)))
        args.append(g.reshape(1, K))
    in_specs.append(pl.BlockSpec((K, tn), lambda i, j: (0, j)))
    args.append(w.astype(BF16))
    if rope:
        in_specs += [
            pl.BlockSpec((K, tn), lambda i, j: (0, j)),
            pl.BlockSpec((tm, tn), lambda i, j: (i % per_b, j)),
            pl.BlockSpec((tm, tn), lambda i, j: (i % per_b, j)),
        ]
        args += [w_rot.astype(BF16), cos, sin]
    return pl.pallas_call(
        functools.partial(_mm_kernel, norm=norm, rope=rope),
        grid=(M // tm, N // tn),
        in_specs=in_specs,
        out_specs=pl.BlockSpec((tm, tn), lambda i, j: (i, j)),
        out_shape=jax.ShapeDtypeStruct((M, N), out_dtype),
        compiler_params=_params("parallel", "parallel"),
        name=name,
    )(*args)


def _ffn_up_kernel(u_ref, wg_ref, wu_ref, o_ref, wg_bf, wu_bf):
    @pl.when(pl.program_id(1) == 0)
    def _():
        wg_bf[...] = wg_ref[...].astype(BF16)
        wu_bf[...] = wu_ref[...].astype(BF16)

    u = u_ref[...]
    a = _dot(u, wg_bf[...])
    b = _dot(u, wu_bf[...])
    o_ref[...] = (a * jax.nn.sigmoid(a) * b).astype(o_ref.dtype)


def _ffn_up(u, wg, wu, tm=1024, tn=512):
    M, K = u.shape
    N = wg.shape[1]
    tm, tn = _tile(M, tm), _tile(N, tn)
    return pl.pallas_call(
        _ffn_up_kernel,
        grid=(N // tn, M // tm),
        in_specs=[
            pl.BlockSpec((tm, K), lambda j, i: (i, 0)),
            pl.BlockSpec((K, tn), lambda j, i: (0, j)),
            pl.BlockSpec((K, tn), lambda j, i: (0, j)),
        ],
        out_specs=pl.BlockSpec((tm, tn), lambda j, i: (i, j)),
        out_shape=jax.ShapeDtypeStruct((M, N), BF16),
        scratch_shapes=[pltpu.VMEM((K, tn), BF16), pltpu.VMEM((K, tn), BF16)],
        compiler_params=_params("parallel", "arbitrary"),
        name="ffn_up",
    )(u, wg, wu)


def _cast_kernel(x_ref, o_ref):
    o_ref[...] = x_ref[...].astype(o_ref.dtype)


def _to_bf16(w, rows=512):
    R, C = w.shape
    rows = _tile(R, rows)
    return pl.pallas_call(
        _cast_kernel,
        grid=(R // rows,),
        in_specs=[pl.BlockSpec((rows, C), lambda i: (i, 0))],
        out_specs=pl.BlockSpec((rows, C), lambda i: (i, 0)),
        out_shape=jax.ShapeDtypeStruct((R, C), BF16),
        compiler_params=_params("parallel"),
        name="cast_bf16",
    )(w)


def _down_post_kernel(h_ref, w_ref, x_ref, g_ref, gate_ref, *rest, res_w, with_next):
    y = _rms(_dot(h_ref[...], w_ref[...])) * g_ref[...]
    x_new = x_ref[...] + res_w * gate_ref[0] * y
    if with_next:
        gn_ref, sc_ref, sh_ref, o_ref, u_ref = rest
        u_ref[...] = (_rms(x_new) * gn_ref[...] * (1.0 + sc_ref[0]) + sh_ref[0]).astype(u_ref.dtype)
    else:
        (o_ref,) = rest
    o_ref[...] = x_new


def _down_post(h, wd, x2, g, gate, res_w, S, next_norm=None, tm=256):
    M, K = h.shape
    D = wd.shape[1]
    B = M // S
    tm = _tile(S, tm)
    per_b = S // tm
    row_spec = pl.BlockSpec((tm, D), lambda i: (i, 0))
    vec_spec = pl.BlockSpec((1, D), lambda i: (0, 0))
    batch_spec = pl.BlockSpec((1, 1, D), lambda i: (i // per_b, 0, 0))
    in_specs = [
        pl.BlockSpec((tm, K), lambda i: (i, 0)),
        pl.BlockSpec((K, D), lambda i: (0, 0), pipeline_mode=pl.Buffered(1)),
        row_spec, vec_spec, batch_spec,
    ]
    args = [h, _to_bf16(wd), x2, g.reshape(1, D), gate.reshape(B, 1, D)]
    out_specs, out_shape = row_spec, jax.ShapeDtypeStruct((M, D), F32)
    if next_norm is not None:
        g_next, scale_next, shift_next = next_norm
        in_specs += [vec_spec, batch_spec, batch_spec]
        args += [g_next.reshape(1, D), scale_next.reshape(B, 1, D), shift_next.reshape(B, 1, D)]
        out_specs, out_shape = [row_spec, row_spec], [out_shape, jax.ShapeDtypeStruct((M, D), BF16)]
    return pl.pallas_call(
        functools.partial(_down_post_kernel, res_w=res_w, with_next=next_norm is not None),
        grid=(M // tm,),
        in_specs=in_specs,
        out_specs=out_specs,
        out_shape=out_shape,
        compiler_params=_params("parallel"),
        name="down_post",
    )(*args)


def _merge_kernel(u_ref, g0_ref, g1_ref, g2_ref, g3_ref, y0_ref, y1_ref, y2_ref, y3_ref, wb_ref, o_ref,
                  wg_bf, wb_bf):
    @pl.when(pl.program_id(1) == 0)
    def _():
        for i, g_ref in enumerate((g0_ref, g1_ref, g2_ref, g3_ref)):
            wg_bf[i] = g_ref[...].astype(BF16)
        wb_bf[...] = wb_ref[...].astype(BF16)

    u = u_ref[...]
    acc = None
    for i, y_ref in enumerate((y0_ref, y1_ref, y2_ref, y3_ref)):
        gate = jax.nn.sigmoid(_dot(u, wg_bf[i]))
        term = gate * _dot(y_ref[...], wb_bf[i])
        acc = term if acc is None else acc + term
    o_ref[...] = acc.astype(o_ref.dtype)


def _merge(u, w_gate, ys, w_branch, tm=512, tn=256):
    M, D = u.shape
    W = w_branch.shape[1]
    tm, tn = _tile(M, tm), _tile(D, tn)
    per_branch = D // tn
    y_spec = pl.BlockSpec((tm, W), lambda j, i: (i, 0))
    gate_specs = [pl.BlockSpec((D, tn), lambda j, i, b=b: (0, b * per_branch + j)) for b in range(4)]
    return pl.pallas_call(
        _merge_kernel,
        grid=(D // tn, M // tm),
        in_specs=[pl.BlockSpec((tm, D), lambda j, i: (i, 0))] + gate_specs + [y_spec] * 4 + [
            pl.BlockSpec((4, W, tn), lambda j, i: (0, 0, j))],
        out_specs=pl.BlockSpec((tm, tn), lambda j, i: (i, j)),
        out_shape=jax.ShapeDtypeStruct((M, D), BF16),
        scratch_shapes=[pltpu.VMEM((4, D, tn), BF16), pltpu.VMEM((4, W, tn), BF16)],
        compiler_params=_params("parallel", "arbitrary"),
        name="merge",
    )(u, w_gate, w_gate, w_gate, w_gate, *ys, w_branch)


def _flash_kernel(*refs, mode, t, G):
    if mode == "select":
        q_ref, k_ref, v_ref, sel_ref, o_ref, m_ref, l_ref, acc_ref = refs
    else:
        q_ref, k_ref, v_ref, o_ref, m_ref, l_ref, acc_ref = refs
    i = pl.program_id(2)
    cols = G * t
    q = q_ref[0, 0, 0]
    k_off = lax.broadcasted_iota(jnp.int32, (t, cols), 0)
    q_off = lax.broadcasted_iota(jnp.int32, (t, cols), 1) & (t - 1)
    blocks_per_tile = t // SLC_BLOCK

    def scores(j, mask):
        start = pl.multiple_of(j * t, t)
        s = _dot(k_ref[0, 0, pl.ds(start, t), :], q)
        if mode == "select":
            parts = []
            for kb in range(blocks_per_tile):
                picked = sel_ref[0, 0, pl.ds(j * blocks_per_tile + kb, 1), :]
                picked = jnp.concatenate([picked] * G, axis=1) if G > 1 else picked
                parts.append(jnp.where(picked > 0.5, s[kb * SLC_BLOCK:(kb + 1) * SLC_BLOCK], NEG_INF))
            s = jnp.concatenate(parts, axis=0)
        if mask is not None:
            s = jnp.where(mask, s, NEG_INF)
        return s, v_ref[0, 0, j]

    s, v = scores(i, k_off <= q_off)
    m = jnp.max(s, axis=0, keepdims=True)
    p = jnp.exp(s - m)
    m_ref[...] = m
    l_ref[...] = jnp.sum(p, axis=0, keepdims=True)
    acc_ref[...] = _dot(v, p.astype(BF16))

    def accumulate(tiles, mask=None):
        parts = [scores(j, mask) for j in tiles]
        s = jnp.concatenate([part[0] for part in parts], axis=0) if len(parts) > 1 else parts[0][0]
        v = jnp.concatenate([part[1] for part in parts], axis=1) if len(parts) > 1 else parts[0][1]
        m_prev = m_ref[...]
        m_new = jnp.maximum(m_prev, jnp.max(s, axis=0, keepdims=True))
        p = jnp.exp(s - m_new)
        alpha = jnp.exp(m_prev - m_new)
        l_ref[...] = alpha * l_ref[...] + jnp.sum(p, axis=0, keepdims=True)
        acc_ref[...] = alpha * acc_ref[...] + _dot(v, p.astype(BF16))
        m_ref[...] = m_new

    if mode == "window":
        n_back = WIN_SIZE // t
        for back in range(1, n_back + 1):
            mask = (k_off > q_off) if back == n_back else None

            @pl.when(i >= back)
            def _(back=back, mask=mask):
                accumulate([i - back], mask)
    else:
        def body(pair, carry):
            accumulate([2 * pair, 2 * pair + 1])
            return carry

        lax.fori_loop(0, jnp.right_shift(i, 1), body, 0)

        @pl.when((i & 1) == 1)
        def _():
            accumulate([i - 1])

    o_ref[0, 0, 0] = (acc_ref[...] * (1.0 / l_ref[...])).astype(o_ref.dtype)


def _flash(q, k, v, *, mode, sel=None, out_dtype=F32):
    B, Hk, nq, Dq, cols = q.shape
    S = k.shape[2]
    Dv, t = v.shape[3], v.shape[4]
    G = cols // t
    assert t & (t - 1) == 0 and t % SLC_BLOCK == 0 and nq * t == S
    if mode == "window":
        assert WIN_SIZE % t == 0
    in_specs = [
        pl.BlockSpec((1, 1, 1, Dq, cols), lambda b, h, i: (b, h, i, 0, 0)),
        pl.BlockSpec((1, 1, S, Dq), lambda b, h, i: (b, h, 0, 0)),
        pl.BlockSpec((1, 1, nq, Dv, t), lambda b, h, i: (b, h, 0, 0, 0)),
    ]
    args = [q, k, v]
    if mode == "select":
        in_specs.append(pl.BlockSpec((1, 1, sel.shape[2], t), lambda b, h, i: (b, h, 0, i)))
        args.append(sel)
    return pl.pallas_call(
        functools.partial(_flash_kernel, mode=mode, t=t, G=G),
        grid=(B, Hk, nq),
        in_specs=in_specs,
        out_specs=pl.BlockSpec((1, 1, 1, Dv, cols), lambda b, h, i: (b, h, i, 0, 0)),
        out_shape=jax.ShapeDtypeStruct((B, Hk, nq, Dv, cols), out_dtype),
        scratch_shapes=[pltpu.VMEM((1, cols), F32), pltpu.VMEM((1, cols), F32), pltpu.VMEM((Dv, cols), F32)],
        compiler_params=_params("parallel", "parallel", "arbitrary"),
        name="attn_" + mode,
    )(*args)


def _nsa_compress_kernel(t_ref, pos_ref, w1_ref, w2_ref, o_ref):
    n_pieces, half = t_ref.shape[-2], t_ref.shape[-1]
    for which in range(2):
        t = t_ref[which, 0, 0]
        w1 = w1_ref[which]
        first = _dot(t, w1[:half], HI)
        second = _dot(t, w1[half:], HI)
        pos_bias = _dot(pos_ref[which], w1, HI)
        pre = first + pltpu.roll(second, n_pieces - 1, 0) + pos_bias
        o_ref[which, 0, 0] = _dot(jax.nn.gelu(pre), w2_ref[which], HI)


def _nsa_compress(kv_flat, pos, w1, w2):
    _, B, Hk, n_pieces, half = kv_flat.shape
    hid, d = w2.shape[1], w2.shape[2]
    return pl.pallas_call(
        _nsa_compress_kernel,
        grid=(B, Hk),
        in_specs=[
            pl.BlockSpec((2, 1, 1, n_pieces, half), lambda b, h: (0, b, h, 0, 0)),
            pl.BlockSpec((2, 1, 2 * half), lambda b, h: (0, 0, 0)),
            pl.BlockSpec((2, 2 * half, hid), lambda b, h: (0, 0, 0)),
            pl.BlockSpec((2, hid, d), lambda b, h: (0, 0, 0)),
        ],
        out_specs=pl.BlockSpec((2, 1, 1, n_pieces, d), lambda b, h: (0, b, h, 0, 0)),
        out_shape=jax.ShapeDtypeStruct((2, B, Hk, n_pieces, d), F32),
        compiler_params=_params("parallel", "parallel"),
        name="nsa_compress",
    )(kv_flat, pos, w1, w2)


def _nsa_cmp_kernel(q_ref, kc_ref, vct_ref, ovt_ref, o_ref, sel_ref, *, t, G):
    i = pl.program_id(2)
    cols = G * t
    n_cmp = kc_ref.shape[-2]
    n_sel = ovt_ref.shape[0]
    kc = kc_ref[0, 0]
    kc_hi = kc.astype(BF16)
    kc_lo = (kc - kc_hi.astype(F32)).astype(BF16)
    q = q_ref[0, 0, 0]
    s = _dot(kc_hi, q) + _dot(kc_lo, q)
    t_pos = i * t + (lax.broadcasted_iota(jnp.int32, (n_cmp, cols), 1) & (t - 1))
    blk_end = lax.broadcasted_iota(jnp.int32, (n_cmp, cols), 0) * CMP_STRIDE + (CMP_BLOCK - 1)
    mask = blk_end <= t_pos
    s = jnp.where(mask, s, NEG_INF)
    e = jnp.exp(s - jnp.max(s, axis=0, keepdims=True))
    p = jnp.where(mask, e * (1.0 / jnp.sum(e, axis=0, keepdims=True)), 0.0)
    o_ref[0, 0, 0] = _dot(vct_ref[0, 0].astype(BF16), p.astype(BF16))

    p_group = p[:, 0:t]
    for g in range(1, G):
        p_group = p_group + p[:, g * t:(g + 1) * t]
    pg_hi = p_group.astype(BF16)
    pg_lo = (p_group - pg_hi.astype(F32)).astype(BF16)
    ovt = ovt_ref[...].astype(BF16)
    imp = _dot(ovt, pg_hi) + _dot(ovt, pg_lo)
    blk = lax.broadcasted_iota(jnp.int32, (n_sel, t), 0)
    cur = jnp.right_shift(i * t + lax.broadcasted_iota(jnp.int32, (n_sel, t), 1), SLC_SHIFT)
    imp = jnp.where(blk <= cur, jnp.where(blk == 0, FORCE_SCORE, jnp.where(blk >= cur - 1, FORCE_SCORE, imp)),
                    NEG_INF)
    rank = jnp.zeros((n_sel, t), F32)
    for c in range(n_sel):
        other = imp[c:c + 1, :]
        later = jnp.where(blk > c, 1.0, 0.0)
        rank = rank + jnp.where(other > imp, 1.0, 0.0) + jnp.where(other == imp, later, 0.0)
    sel_ref[0, 0] = jnp.where(rank < float(min(SLC_TOPK, n_sel)), 1.0, 0.0)


def _nsa_cmp(q, kc, vct, overlap_t):
    B, Hk, nq, d, cols = q.shape
    n_cmp = kc.shape[2]
    n_sel = overlap_t.shape[0]
    t = n_sel * SLC_BLOCK // nq
    G = cols // t
    o_spec = pl.BlockSpec((1, 1, 1, d, cols), lambda b, h, i: (b, h, i, 0, 0))
    return pl.pallas_call(
        functools.partial(_nsa_cmp_kernel, t=t, G=G),
        grid=(B, Hk, nq),
        in_specs=[
            o_spec,
            pl.BlockSpec((1, 1, n_cmp, d), lambda b, h, i: (b, h, 0, 0)),
            pl.BlockSpec((1, 1, d, n_cmp), lambda b, h, i: (b, h, 0, 0)),
            pl.BlockSpec((n_sel, n_cmp), lambda b, h, i: (0, 0)),
        ],
        out_specs=[o_spec, pl.BlockSpec((1, 1, n_sel, t), lambda b, h, i: (b, h, 0, i))],
        out_shape=[
            jax.ShapeDtypeStruct((B, Hk, nq, d, cols), F32),
            jax.ShapeDtypeStruct((B, Hk, n_sel, nq * t), F32),
        ],
        compiler_params=_params("parallel", "parallel", "parallel"),
        name="nsa_cmp",
    )(q, kc, vct, overlap_t)


def _nsa_combine_kernel(gl_ref, oc_ref, os_ref, ow_ref, o_ref):
    g = jax.nn.sigmoid(gl_ref[0, 0, 0])
    o_ref[0, 0, 0] = (g[0:1] * oc_ref[0, 0, 0] + g[1:2] * os_ref[0, 0, 0]
                      + g[2:3] * ow_ref[0, 0, 0]).astype(o_ref.dtype)


def _nsa_combine(gl, o_cmp, o_slc, o_win):
    B, Hk, nq, d, cols = o_cmp.shape
    o_spec = pl.BlockSpec((1, 1, 1, d, cols), lambda b, h, i: (b, h, i, 0, 0))
    return pl.pallas_call(
        _nsa_combine_kernel,
        grid=(B, Hk, nq),
        in_specs=[pl.BlockSpec((1, 1, 1, 3, cols), lambda b, h, i: (b, h, i, 0, 0)), o_spec, o_spec, o_spec],
        out_specs=o_spec,
        out_shape=jax.ShapeDtypeStruct(o_cmp.shape, BF16),
        compiler_params=_params("parallel", "parallel", "parallel"),
        name="nsa_combine",
    )(gl, o_cmp, o_slc, o_win)


def _rwkv_prep_kernel(p_ref, mu_ref, w0_ref, ww2_ref, a0_ref, aw2_ref, gw2_ref, kk_ref, ka_ref, seg_ref,
                      r_o, lw_o, k_o, v_o, kk_o, b_o, g_o, last_ref):
    tb = p_ref.shape[1]
    width = r_o.shape[-1]

    @pl.when(pl.program_id(1) == 0)
    def _():
        last_ref[...] = jnp.zeros_like(last_ref)

    p = p_ref[0]
    row = lax.broadcasted_iota(jnp.int32, p.shape, 0)
    prev = jnp.where(row == 0, last_ref[...], pltpu.roll(p, 1, 0))
    last_ref[...] = p[tb - 1:tb]
    xs = p + (prev - p) * mu_ref[...]
    r = xs[:, 0:width]
    k = xs[:, width:2 * width]
    v = xs[:, 2 * width:3 * width]
    lora = xs[:, 3 * width:3 * width + 128]
    g_lo = xs[:, 3 * width + 128:]
    lw = -math.exp(-0.5) * jax.nn.sigmoid(w0_ref[...] + _dot(jnp.tanh(lora), ww2_ref[...], HI))
    a = jax.nn.sigmoid(a0_ref[...] + _dot(lora, aw2_ref[...], HI))
    g = _dot(jax.nn.sigmoid(g_lo), gw2_ref[...], HI)
    kk = k * kk_ref[...]
    norm = jnp.sqrt(_dot(kk * kk, seg_ref[...], HI))
    kk = kk / jnp.maximum(norm, 1e-12)
    r_o[0] = r
    lw_o[0] = lw
    k_o[0] = k * (1.0 + (a - 1.0) * ka_ref[...])
    v_o[0] = v
    kk_o[0] = kk
    b_o[0] = kk * a
    g_o[0] = g


def _rwkv_prep(p, consts, tb=256):
    B, S, P = p.shape
    width = consts[1].shape[-1]
    tb = _tile(S, tb)
    const_specs = [pl.BlockSpec(c.shape, lambda b, t: (0, 0)) for c in consts]
    o_spec = pl.BlockSpec((1, tb, width), lambda b, t: (b, t, 0))
    return pl.pallas_call(
        _rwkv_prep_kernel,
        grid=(B, S // tb),
        in_specs=[pl.BlockSpec((1, tb, P), lambda b, t: (b, t, 0))] + const_specs,
        out_specs=[o_spec] * 7,
        out_shape=[jax.ShapeDtypeStruct((B, S, width), F32)] * 7,
        scratch_shapes=[pltpu.VMEM((1, P), F32)],
        compiler_params=_params("parallel", "arbitrary"),
        name="rwkv_prep",
    )(p, *consts)


def _cumsum_rows(x):
    n = x.shape[0]
    row = lax.broadcasted_iota(jnp.int32, x.shape, 0)
    step = 1
    while step < n:
        x = x + jnp.where(row >= step, pltpu.roll(x, step, 0), 0.0)
        step *= 2
    return x


def _seg_sum(x, seg):
    rows = x.shape[0]
    hi = x.astype(BF16)
    lo = (x - hi.astype(F32)).astype(BF16)
    both = _dot(jnp.concatenate([hi, lo], axis=0), seg)
    return both[:rows] + both[rows:]


def _rwkv_group_chunk(rg, kkg, bd, kd, kc, bc, v, gamma, state, lane_masks, strict, incl, eye, chunk):
    stack = lambda x: jnp.concatenate([x * m for m in lane_masks], axis=0)
    kks, rs, vs = stack(kkg), stack(rg), stack(v)
    n, w = kks.shape
    big = _dot_nt(jnp.concatenate([kks, rs], axis=0), jnp.concatenate([stack(bd), stack(kd)], axis=0))
    a_mat = jnp.where(strict, big[:n, :n], 0.0)
    bk = jnp.where(strict, big[:n, n:], 0.0)
    ay = jnp.where(incl, big[n:, :n], 0.0)
    by = jnp.where(incl, big[n:, n:], 0.0)
    power = -a_mat
    inv = eye + power
    power = _dot(power, power)
    for _ in range(int(math.log2(chunk)) - 2):
        both = _dot(jnp.concatenate([power, inv], axis=0), power)
        power, inv = both[:n], inv + both[n:]
    inv = inv + _dot(inv, power)
    bv = _dot(jnp.concatenate([bk, by], axis=0), vs)
    tx = _dot(inv, jnp.concatenate([kks, bv[:n]], axis=1))
    ayx = _dot(ay, tx)
    q_hat = rs - ayx[:, :w]
    y_intra = bv[n:] - ayx[:, w:]
    qt = _dot_nt(jnp.concatenate([q_hat, tx[:, :w]], axis=0), state)
    ys = qt[:n] + y_intra
    us = qt[n:] + tx[:, w:]
    new_state = state * gamma + _dot_tn(jnp.concatenate([vs, us], axis=0),
                                        jnp.concatenate([stack(kc), -stack(bc)], axis=0))
    y = ys[:chunk]
    for h in range(1, len(lane_masks)):
        y = y + ys[h * chunk:(h + 1) * chunk]
    return y, new_state


def _rwkv_scan_kernel(r_ref, lw_ref, k_ref, v_ref, kk_ref, b_ref, g_ref, lnw_ref, lnb_ref, rk_ref, seg_ref,
                      o_ref, state_ref, *, heads, chunk, group):
    tb, width = r_ref.shape[1], r_ref.shape[2]
    n = width // heads
    groups, gw, gn = heads // group, group * n, group * chunk

    @pl.when(pl.program_id(1) == 0)
    def _():
        state_ref[...] = jnp.zeros_like(state_ref)

    lane = lax.broadcasted_iota(jnp.int32, (1, gw), 1)
    lane_masks = [jnp.where(jnp.logical_and(lane >= h * n, lane < (h + 1) * n), 1.0, 0.0) for h in range(group)]
    row_id = lax.broadcasted_iota(jnp.int32, (gn, gn), 0)
    col_id = lax.broadcasted_iota(jnp.int32, (gn, gn), 1)
    row, col = row_id & (chunk - 1), col_id & (chunk - 1)
    strict, incl = col < row, col <= row
    eye = jnp.where(row_id == col_id, 1.0, 0.0)
    states = [state_ref[p] for p in range(groups)]
    y_chunks = []
    for c in range(tb // chunk):
        rows = pl.ds(c * chunk, chunk)
        r, lw, k, v = r_ref[0, rows, :], lw_ref[0, rows, :], k_ref[0, rows, :], v_ref[0, rows, :]
        kk, bb = kk_ref[0, rows, :], b_ref[0, rows, :]
        g = _cumsum_rows(lw)
        g_last = g[chunk - 1:chunk]
        decay_out = jnp.exp(-g)
        decay_end = jnp.exp(g_last - g)
        rg, kkg = r * jnp.exp(g), kk * jnp.exp(g - lw)
        bd, kd = bb * decay_out, k * decay_out
        kc, bc = k * decay_end, bb * decay_end
        gamma = jnp.exp(g_last)
        ys = []
        for p in range(groups):
            sl = slice(p * gw, (p + 1) * gw)
            y, states[p] = _rwkv_group_chunk(rg[:, sl], kkg[:, sl], bd[:, sl], kd[:, sl], kc[:, sl], bc[:, sl],
                                             v[:, sl], gamma[:, sl], states[p], lane_masks, strict, incl, eye, chunk)
            ys.append(y)
        y_chunks.append(jnp.concatenate(ys, axis=1))
    for p in range(groups):
        state_ref[p] = states[p]
    y = jnp.concatenate(y_chunks, axis=0)
    seg = seg_ref[...]
    r, k, v = r_ref[0], k_ref[0], v_ref[0]
    sums = _seg_sum(jnp.concatenate([y, r * k * rk_ref[...]], axis=0), seg)
    dev = y - sums[:tb] * (1.0 / n)
    var = _seg_sum(dev * dev, seg) * (1.0 / n)
    yn = dev * lax.rsqrt(var + RWKV_GN_EPS) * lnw_ref[...] + lnb_ref[...]
    o_ref[0] = ((yn + sums[tb:] * v) * g_ref[0]).astype(o_ref.dtype)


def _rwkv_scan(r, lw, k, v, kk, bb, g, ln_w, ln_b, r_k, seg, tb=256, group=4):
    B, S, width = r.shape
    heads = RWKV_HEADS
    n = width // heads
    chunk = min(RWKV_CHUNK, S)
    assert chunk == n and heads % group == 0
    tb = _tile(S, tb)
    x_spec = pl.BlockSpec((1, tb, width), lambda b, t: (b, t, 0))
    vec_spec = pl.BlockSpec((1, width), lambda b, t: (0, 0))
    return pl.pallas_call(
        functools.partial(_rwkv_scan_kernel, heads=heads, chunk=chunk, group=group),
        grid=(B, S // tb),
        in_specs=[x_spec] * 7 + [vec_spec] * 3 + [pl.BlockSpec((width, width), lambda b, t: (0, 0))],
        out_specs=x_spec,
        out_shape=jax.ShapeDtypeStruct((B, S, width), BF16),
        scratch_shapes=[pltpu.VMEM((heads // group, group * n, group * n), F32)],
        compiler_params=_params("parallel", "arbitrary"),
        name="rwkv_scan",
    )(r, lw, k, v, kk, bb, g, ln_w, ln_b, r_k, seg.astype(BF16))


def _gla_kernel(q_ref, k_ref, v_ref, r_ref, al_ref, aw_ref, ab_ref, ng_ref, o_ref, state_ref,
                *, heads, chunk):
    tb = v_ref.shape[1]
    dk = q_ref.shape[-1]
    dv = v_ref.shape[-1] // heads

    @pl.when(pl.program_id(1) == 0)
    def _():
        state_ref[...] = jnp.zeros_like(state_ref)

    row = lax.broadcasted_iota(jnp.int32, (chunk, chunk), 0)
    col = lax.broadcasted_iota(jnp.int32, (chunk, chunk), 1)
    incl = col <= row
    states = [state_ref[h] for h in range(heads)]
    for c in range(tb // chunk):
        rows = pl.ds(c * chunk, chunk)
        x = _dot(al_ref[0, rows, :], aw_ref[...], HI) + ab_ref[...]
        log_a = (jnp.minimum(x, 0.0) - jnp.log(1.0 + jnp.exp(-jnp.abs(x)))) * (1.0 / GLA_TAU)
        b_all = _cumsum_rows(log_a)
        outs = []
        for h in range(heads):
            cols = pl.ds(h * dv, dv)
            b = b_all[:, h * dk:(h + 1) * dk]
            b_last = b[chunk - 1:chunk]
            q = q_ref[0, h, rows, :]
            k = k_ref[0, h, rows, :]
            v = v_ref[0, rows, cols]
            qs = q * (dk ** -0.5)
            qe = qs * jnp.exp(b)
            b_mid = b[chunk // 2:chunk // 2 + 1]
            attn = jnp.where(incl, _dot_nt(qs * jnp.exp(b - b_mid), k * jnp.exp(b_mid - b)), 0.0)
            o = _dot(attn, v) + _dot_nt(qe, states[h])
            states[h] = states[h] * jnp.exp(b_last) + _dot_tn(v, k * jnp.exp(b_last - b))
            outs.append(o * lax.rsqrt(jnp.mean(o * o, axis=-1, keepdims=True) + NORM_EPS))
        gate = r_ref[0, rows, :]
        o_ref[0, rows, :] = (jnp.concatenate(outs, axis=1) * ng_ref[...]
                             * (gate * jax.nn.sigmoid(gate))).astype(o_ref.dtype)
    for h in range(heads):
        state_ref[h] = states[h]


def _gla(q, k, v, r, alpha_lo, aw, ab, ng, tb=256):
    B, H, S, dk = q.shape
    width = v.shape[-1]
    lora = alpha_lo.shape[-1]
    chunk = min(GLA_CHUNK, S)
    tb = _tile(S, tb)
    qk_spec = pl.BlockSpec((1, H, tb, dk), lambda b, t: (b, 0, t, 0))
    x_spec = pl.BlockSpec((1, tb, width), lambda b, t: (b, t, 0))
    return pl.pallas_call(
        functools.partial(_gla_kernel, heads=H, chunk=chunk),
        grid=(B, S // tb),
        in_specs=[
            qk_spec, qk_spec, x_spec, x_spec,
            pl.BlockSpec((1, tb, lora), lambda b, t: (b, t, 0)),
            pl.BlockSpec(aw.shape, lambda b, t: (0, 0)),
            pl.BlockSpec(ab.shape, lambda b, t: (0, 0)),
            pl.BlockSpec(ng.shape, lambda b, t: (0, 0)),
        ],
        out_specs=x_spec,
        out_shape=jax.ShapeDtypeStruct((B, S, width), BF16),
        scratch_shapes=[pltpu.VMEM((H, width // H, dk), F32)],
        compiler_params=_params("parallel", "arbitrary"),
        name="gla",
    )(q, k, v, r, alpha_lo, aw, ab, ng)


def _rope_tables(S, d):
    inv = ROPE_THETA ** (-jnp.arange(0, d, 2, dtype=F32) / d)
    ang = jnp.arange(S).astype(F32)[:, None] * inv[None, :]
    cos, sin = jnp.cos(ang), jnp.sin(ang)
    return jnp.concatenate([cos, cos], axis=-1), jnp.concatenate([sin, sin], axis=-1)


def _rot_cols(w, d):
    K = w.shape[0]
    w = w.reshape(K, -1, 2, d // 2)
    return jnp.concatenate([-w[:, :, 1], w[:, :, 0]], axis=-1).reshape(K, -1)


def _nsa_constants(S):
    n_pieces = S // CMP_STRIDE
    n_cmp = (S - CMP_BLOCK) // CMP_STRIDE + 1
    n_sel = S // SLC_BLOCK
    start = np.arange(n_cmp)[:, None] * CMP_STRIDE
    end = start + CMP_BLOCK - 1
    blk = np.arange(n_sel)[None, :]
    overlap = np.zeros((n_pieces, n_sel), np.float32)
    overlap[:n_cmp] = ((start < (blk + 1) * SLC_BLOCK) & (end >= blk * SLC_BLOCK)).astype(np.float32)
    return jnp.asarray(overlap.T)


def _token_mixing(u, B, S, w_in, w_branch, cmp_pos, cmp_w1, cmp_w2, rwkv_mu, rwkv_w0, rwkv_w_w2, rwkv_a0,
                  rwkv_a_w2, rwkv_g_w2, rwkv_k_k, rwkv_k_a, rwkv_r_k, rwkv_ln_w, rwkv_ln_b, gla_alpha_w2,
                  gla_alpha_b, gla_norm_g, mla_q_norm, mla_w_uq, mla_kv_norm, mla_w_ukv):
    M, D = u.shape
    d = HEAD_DIM
    width = D // 4
    sizes = [width + 6 * NSA_KV_HEADS * d + 3 * NSA_HEADS, 3 * width + 256, 2 * (width // 2) + 2 * width + 16,
             384 + 128 + MLA_ROPE, 4 * D]
    offs = np.cumsum([0] + sizes)
    assert offs[-1] == w_in.shape[1]
    w_nsa, w_rwkv, w_gla, w_mla, w_gate = [w_in[:, offs[i]:offs[i + 1]] for i in range(5)]
    zeros = lambda n: jnp.zeros((D, n), F32)
    kvw = NSA_KV_HEADS * d

    wq = w_nsa[:, :width]
    wkc, wvc, wks, wvs, wkw, wvw = [w_nsa[:, width + i * kvw:width + (i + 1) * kvw] for i in range(6)]
    wgl = w_nsa[:, width + 6 * kvw:]
    wkr = w_mla[:, 512:512 + MLA_ROPE]
    n_rope = width + 3 * kvw + MLA_ROPE
    pad = (-n_rope) % 512
    w_a = jnp.concatenate([wq, wkc, wks, wkw, wkr, zeros(pad)], axis=1)
    w_b = jnp.concatenate([_rot_cols(wq, d), _rot_cols(wkc, d), _rot_cols(wks, d), _rot_cols(wkw, d),
                           _rot_cols(wkr, MLA_ROPE), zeros(pad)], axis=1)
    cos_d, sin_d = _rope_tables(S, d)
    cos_r, sin_r = _rope_tables(S, MLA_ROPE)
    reps = (width + 3 * kvw) // d
    cos_t = jnp.concatenate([jnp.tile(cos_d, (1, reps)), cos_r, jnp.ones((S, pad), F32)], axis=1)
    sin_t = jnp.concatenate([jnp.tile(sin_d, (1, reps)), sin_r, jnp.zeros((S, pad), F32)], axis=1)
    roped = _matmul(u, w_a, S=S, w_rot=w_b, cos=cos_t, sin=sin_t, name="proj_rope").reshape(B, S, -1)
    n_plain = 3 * kvw + 3 * NSA_HEADS
    plain = _matmul(u, jnp.concatenate([wvc, wvs, wvw, wgl, zeros((-n_plain) % 128)], axis=1), S=S,
                    name="proj_nsa_v")

    Hk, G = NSA_KV_HEADS, NSA_GROUP
    t_att = _tile(S, 256)
    nq = S // t_att
    cols_layout = lambda x, f: x.reshape(B, nq, t_att, Hk, G, f).transpose(0, 3, 1, 5, 4, 2).reshape(
        B, Hk, nq, f, G * t_att)
    kv_heads = lambda x: x.reshape(B, S, Hk, d).transpose(0, 2, 1, 3)
    v_tiles = lambda x: x.astype(BF16).reshape(B, nq, t_att, Hk, d).transpose(0, 3, 1, 4, 2)
    q_att = cols_layout((roped[..., :width] * d ** -0.5).astype(BF16), d)
    kc_in = kv_heads(roped[..., width:width + kvw])
    ks = kv_heads(roped[..., width + kvw:width + 2 * kvw].astype(BF16))
    kw = kv_heads(roped[..., width + 2 * kvw:width + 3 * kvw].astype(BF16))
    k_pe = roped[..., width + 3 * kvw:width + 3 * kvw + MLA_ROPE]
    plain3 = plain.reshape(B, S, -1)
    vc_in = kv_heads(plain3[..., :kvw])
    vs, vw = v_tiles(plain3[..., kvw:2 * kvw]), v_tiles(plain3[..., 2 * kvw:3 * kvw])
    gate_logits = cols_layout(plain3[..., 3 * kvw:3 * kvw + 3 * NSA_HEADS], 3)

    overlap_t = _nsa_constants(S)
    n_pieces = S // CMP_STRIDE
    kv_flat = jnp.stack([kc_in, vc_in]).reshape(2, B, Hk, n_pieces, CMP_STRIDE * d)
    cmp = _nsa_compress(kv_flat, cmp_pos.reshape(2, 1, CMP_BLOCK * d), cmp_w1, cmp_w2)
    o_cmp, sel = _nsa_cmp(q_att, cmp[0], cmp[1].transpose(0, 1, 3, 2), overlap_t)
    o_slc = _flash(q_att, ks, vs, mode="select", sel=sel)
    o_win = _flash(q_att, kw, vw, mode="window")
    y_nsa = _nsa_combine(gate_logits, o_cmp, o_slc, o_win)
    y_nsa = y_nsa.reshape(B, Hk, nq, d, G, t_att).transpose(0, 2, 5, 1, 4, 3).reshape(M, width)

    regroup = lambda t: jnp.concatenate(
        [t[..., 0:width], t[..., width + 64:2 * width + 64], t[..., 2 * width + 64:3 * width + 64],
         t[..., width:width + 64], t[..., 3 * width + 64:3 * width + 128], t[..., 3 * width + 128:]], axis=-1)
    p_rwkv = _matmul(u, regroup(w_rwkv), S=S, tn=896, name="proj_rwkv").reshape(B, S, -1)
    row = lambda t: t.reshape(1, -1)
    seg = jnp.asarray(np.kron(np.eye(RWKV_HEADS), np.ones((d, d))).astype(np.float32))
    lora_pad = jnp.zeros((64, width), F32)
    consts = [row(regroup(rwkv_mu)), row(rwkv_w0), jnp.concatenate([rwkv_w_w2, lora_pad], axis=0), row(rwkv_a0),
              jnp.concatenate([lora_pad, rwkv_a_w2], axis=0), rwkv_g_w2, row(rwkv_k_k), row(rwkv_k_a), seg]
    r, lw, k2, v, kk, bb, g = _rwkv_prep(p_rwkv, consts)
    y_rwkv = _rwkv_scan(r, lw, k2, v, kk, bb, g, row(rwkv_ln_w), row(rwkv_ln_b), row(rwkv_r_k), seg)
    y_rwkv = y_rwkv.reshape(M, width)

    dk_all = width // 2
    w_gla_p = jnp.concatenate([w_gla[:, :2 * dk_all + width], w_gla[:, 2 * dk_all + width + 16:],
                               w_gla[:, 2 * dk_all + width:2 * dk_all + width + 16], zeros(240)], axis=1)
    p_gla = _matmul(u, w_gla_p, S=S, tn=896, name="proj_gla").reshape(B, S, -1)
    dkh = dk_all // GLA_HEADS
    gla_heads = lambda t: t.reshape(B, S, GLA_HEADS, dkh).transpose(0, 2, 1, 3)
    y_gla = _gla(gla_heads(p_gla[..., :dk_all]), gla_heads(p_gla[..., dk_all:2 * dk_all]),
                 p_gla[..., 2 * dk_all:2 * dk_all + width], p_gla[..., 2 * dk_all + width:2 * dk_all + 2 * width],
                 p_gla[..., 2 * dk_all + 2 * width:2 * dk_all + 2 * width + 16],
                 gla_alpha_w2, row(gla_alpha_b), row(gla_norm_g)).reshape(M, width)

    c_q = _matmul(u, w_mla[:, :384], S=S, name="proj_mla_q")
    c_kv = _matmul(u, w_mla[:, 384:512], S=S, name="proj_mla_kv")
    H = MLA_HEADS
    wq3 = mla_w_uq.reshape(384, H, MLA_NOPE + MLA_ROPE)
    rp = wq3[:, :, MLA_NOPE:]
    half = MLA_ROPE // 2
    z = lambda n: jnp.zeros((384, H, n), F32)
    wq_a = jnp.concatenate([wq3, z(128 - MLA_NOPE - MLA_ROPE)], axis=-1).reshape(384, H * 128)
    wq_b = jnp.concatenate([z(MLA_NOPE), -rp[..., half:], rp[..., :half], z(128 - MLA_NOPE - MLA_ROPE)],
                           axis=-1).reshape(384, H * 128)
    tail = 128 - MLA_NOPE - MLA_ROPE
    cos_h = jnp.concatenate([jnp.ones((S, MLA_NOPE), F32), cos_r, jnp.ones((S, tail), F32)], axis=1)
    sin_h = jnp.concatenate([jnp.zeros((S, MLA_NOPE), F32), sin_r, jnp.zeros((S, tail), F32)], axis=1)
    q_m = _matmul(c_q, wq_a, S=S, g=mla_q_norm, w_rot=wq_b, cos=jnp.tile(cos_h, (1, H)), sin=jnp.tile(sin_h, (1, H)),
                  name="mla_q")
    t_mla = _tile(S, 512)
    nq_m = S // t_mla
    tiles_m = lambda x, f: x.reshape(B, nq_m, t_mla, H, f).transpose(0, 3, 1, 4, 2)
    q_m = tiles_m((q_m * (MLA_NOPE + MLA_ROPE) ** -0.5).astype(BF16), 128)
    kv_m = _matmul(c_kv, mla_w_ukv, S=S, g=mla_kv_norm, name="mla_kv").reshape(B, S, H, MLA_NOPE + width // H)
    k_m = jnp.concatenate([kv_m[..., :MLA_NOPE], jnp.broadcast_to(k_pe[:, :, None, :], (B, S, H, MLA_ROPE)),
                           jnp.zeros((B, S, H, tail), F32)], axis=-1).astype(BF16).transpose(0, 2, 1, 3)
    v_m = tiles_m(kv_m[..., MLA_NOPE:].astype(BF16), width // H)
    o_mla = _flash(q_m, k_m, v_m, mode="causal", out_dtype=BF16)
    y_mla = o_mla.transpose(0, 2, 4, 1, 3).reshape(M, width)

    return _merge(u, w_gate, (y_nsa, y_rwkv, y_gla, y_mla), w_branch)


def kernel(x, c, ada_w, ada_b, pre_g, post_g, ffn_wg, ffn_wu, ffn_wd, mix_w_in, mix_w_branch, mix_w_out, nsa_cmp_pos, nsa_cmp_w1, nsa_cmp_w2, rwkv_mu, rwkv_w0, rwkv_w_w2, rwkv_a0, rwkv_a_w2, rwkv_g_w2, rwkv_k_k, rwkv_k_a, rwkv_r_k, rwkv_ln_w, rwkv_ln_b, gla_alpha_w2, gla_alpha_b, gla_norm_g, mla_q_norm, mla_w_uq, mla_kv_norm, mla_w_ukv):
    B, S, D = x.shape
    M = B * S
    depth = ada_w.shape[0]
    mod = _modulation(c, ada_w, ada_b)
    shifts, scales, gates = mod[:, :, :D], mod[:, :, D:2 * D], mod[:, :, 2 * D:]
    x2 = x.reshape(M, D)
    n_sub = 3 * depth
    u = _prenorm(x2, pre_g[0, 0], scales[0], shifts[0], S)
    for s in range(n_sub):
        l, i = divmod(s, 3)
        nxt = s + 1
        next_norm = (pre_g[nxt // 3, nxt % 3], scales[nxt], shifts[nxt]) if nxt < n_sub else None
        if i == 1:
            y = _token_mixing(
                u, B, S, mix_w_in[l], mix_w_branch[l], nsa_cmp_pos[l], nsa_cmp_w1[l], nsa_cmp_w2[l], rwkv_mu[l],
                rwkv_w0[l], rwkv_w_w2[l], rwkv_a0[l], rwkv_a_w2[l], rwkv_g_w2[l], rwkv_k_k[l], rwkv_k_a[l],
                rwkv_r_k[l], rwkv_ln_w[l], rwkv_ln_b[l], gla_alpha_w2[l], gla_alpha_b[l], gla_norm_g[l],
                mla_q_norm[l], mla_w_uq[l], mla_kv_norm[l], mla_w_ukv[l])
            out = _down_post(y, mix_w_out[l], x2, post_g[l, i], gates[s], 1.0, S, next_norm)
        else:
            j = 0 if i == 0 else 1
            h = _ffn_up(u, ffn_wg[l, j], ffn_wu[l, j])
            out = _down_post(h, ffn_wd[l, j], x2, post_g[l, i], gates[s], 0.5, S, next_norm)
        x2, u = out if next_norm is not None else (out, None)
    return x2.reshape(B, S, D)
```

```python
import functools
import math

import jax
import jax.numpy as jnp
import numpy as np
from jax import lax
from jax.experimental import pallas as pl
from jax.experimental.pallas import tpu as pltpu

F32 = jnp.float32
BF16 = jnp.bfloat16
HI = lax.Precision.HIGHEST

NORM_EPS = 1e-6
NEG_INF = -1e30
FORCE_SCORE = 1e30
ROPE_THETA = 10000.0
HEAD_DIM = 64

NSA_HEADS = 8
NSA_KV_HEADS = 2
NSA_GROUP = NSA_HEADS // NSA_KV_HEADS
CMP_BLOCK = 32
CMP_STRIDE = 16
SLC_BLOCK = 64
SLC_SHIFT = 6
SLC_TOPK = 16
WIN_SIZE = 512
RWKV_HEADS = 8
RWKV_GN_EPS = 64e-5
RWKV_CHUNK = 64
GLA_HEADS = 4
GLA_TAU = 16.0
GLA_CHUNK = 64
MLA_HEADS = 8
MLA_NOPE = 64
MLA_ROPE = 32

VMEM_LIMIT_BYTES = 56 * 1024 * 1024


def _params(*sem):
    return pltpu.CompilerParams(dimension_semantics=sem, vmem_limit_bytes=VMEM_LIMIT_BYTES)


def _dot(a, b, precision=None):
    return jnp.dot(a, b, preferred_element_type=F32, precision=precision)


def _dot_nt(a, b, precision=None):
    return lax.dot_general(a, b, (((1,), (1,)), ((), ())), preferred_element_type=F32, precision=precision)


def _dot_tn(a, b, precision=None):
    return lax.dot_general(a, b, (((0,), (0,)), ((), ())), preferred_element_type=F32, precision=precision)


def _rms(x):
    return x * lax.rsqrt(jnp.mean(x * x, axis=-1, keepdims=True) + NORM_EPS)


def _tile(n, t):
    t = min(n, t)
    assert n % t == 0, (n, t)
    return t


def _mod_kernel(c_ref, w_ref, b_ref, o_ref):
    c = c_ref[...]
    act = c * jax.nn.sigmoid(c)
    hi = act.astype(BF16)
    lo = (act - hi.astype(F32)).astype(BF16)
    w = w_ref[0].astype(BF16)
    o_ref[0] = _dot(hi, w) + _dot(lo, w) + b_ref[0]


def _modulation(c, ada_w, ada_b):
    B, D = c.shape
    n_sub = ada_w.shape[0] * ada_w.shape[1]
    w = ada_w.reshape(n_sub, D, 3 * D)
    b = ada_b.reshape(n_sub, 1, 3 * D)
    rows = 8
    cp = jnp.zeros((rows, D), F32).at[:B].set(c)
    tn = _tile(3 * D, 512)
    out = pl.pallas_call(
        _mod_kernel,
        grid=(n_sub, 3 * D // tn),
        in_specs=[
            pl.BlockSpec((rows, D), lambda s, j: (0, 0)),
            pl.BlockSpec((1, D, tn), lambda s, j: (s, 0, j)),
            pl.BlockSpec((1, 1, tn), lambda s, j: (s, 0, j)),
        ],
        out_specs=pl.BlockSpec((1, rows, tn), lambda s, j: (s, 0, j)),
        out_shape=jax.ShapeDtypeStruct((n_sub, rows, 3 * D), F32),
        compiler_params=_params("parallel", "parallel"),
        name="modulation",
    )(cp, w, b)
    return out[:, :B]


def _prenorm_kernel(x_ref, g_ref, sc_ref, sh_ref, o_ref):
    y = _rms(x_ref[...]) * g_ref[...]
    o_ref[...] = (y * (1.0 + sc_ref[0]) + sh_ref[0]).astype(o_ref.dtype)


def _prenorm(x2, g, scale, shift, S):
    M, D = x2.shape
    B = M // S
    tm = _tile(S, 512)
    per_b = S // tm
    return pl.pallas_call(
        _prenorm_kernel,
        grid=(M // tm,),
        in_specs=[
            pl.BlockSpec((tm, D), lambda i: (i, 0)),
            pl.BlockSpec((1, D), lambda i: (0, 0)),
            pl.BlockSpec((1, 1, D), lambda i: (i // per_b, 0, 0)),
            pl.BlockSpec((1, 1, D), lambda i: (i // per_b, 0, 0)),
        ],
        out_specs=pl.BlockSpec((tm, D), lambda i: (i, 0)),
        out_shape=jax.ShapeDtypeStruct((M, D), BF16),
        compiler_params=_params("parallel"),
        name="prenorm",
    )(x2, g.reshape(1, D), scale.reshape(B, 1, D), shift.reshape(B, 1, D))


def _mm_kernel(*refs, norm, rope):
    it = iter(refs)
    a_ref = next(it)
    g_ref = next(it) if norm else None
    w_ref = next(it)
    if rope:
        w2_ref, cos_ref, sin_ref = next(it), next(it), next(it)
    o_ref = next(it)
    a = a_ref[...]
    if norm:
        a = _rms(a.astype(F32)) * g_ref[...]
    a = a.astype(BF16)
    out = _dot(a, w_ref[...])
    if rope:
        out = out * cos_ref[...] + _dot(a, w2_ref[...]) * sin_ref[...]
    o_ref[...] = out.astype(o_ref.dtype)


def _matmul(a, w, *, S, g=None, w_rot=None, cos=None, sin=None, out_dtype=F32, tm=1024, tn=512, name="matmul"):
    M, K = a.shape
    N = w.shape[1]
    tm = _tile(S, tm)
    tn = _tile(N, tn)
    per_b = S // tm
    norm, rope = g is not None, w_rot is not None
    in_specs = [pl.BlockSpec((tm, K), lambda i, j: (i, 0))]
    args = [a]
    if norm:
        in_specs.append(pl.BlockSpec((1, K), lambda i, j: (0, 0)))
        args.append(g.reshape(1, K))
    in_specs.append(pl.BlockSpec((K, tn), lambda i, j: (0, j)))
    args.append(w.astype(BF16))
    if rope:
        in_specs += [
            pl.BlockSpec((K, tn), lambda i, j: (0, j)),
            pl.BlockSpec((tm, tn), lambda i, j: (i % per_b, j)),
            pl.BlockSpec((tm, tn), lambda i, j: (i % per_b, j)),
        ]
        args += [w_rot.astype(BF16), cos, sin]
    return pl.pallas_call(
        functools.partial(_mm_kernel, norm=norm, rope=rope),
        grid=(M // tm, N // tn),
        in_specs=in_specs,
        out_specs=pl.BlockSpec((tm, tn), lambda i, j: (i, j)),
        out_shape=jax.ShapeDtypeStruct((M, N), out_dtype),
        compiler_params=_params("parallel", "parallel"),
        name=name,
    )(*args)


def _ffn_up_kernel(u_ref, wg_ref, wu_ref, o_ref, wg_bf, wu_bf):
    @pl.when(pl.program_id(1) == 0)
    def _():
        wg_bf[...] = wg_ref[...].astype(BF16)
        wu_bf[...] = wu_ref[...].astype(BF16)

    u = u_ref[...]
    a = _dot(u, wg_bf[...])
    b = _dot(u, wu_bf[...])
    o_ref[...] = (a * jax.nn.sigmoid(a) * b).astype(o_ref.dtype)


def _ffn_up(u, wg, wu, tm=1024, tn=512):
    M, K = u.shape
    N = wg.shape[1]
    tm, tn = _tile(M, tm), _tile(N, tn)
    return pl.pallas_call(
        _ffn_up_kernel,
        grid=(N // tn, M // tm),
        in_specs=[
            pl.BlockSpec((tm, K), lambda j, i: (i, 0)),
            pl.BlockSpec((K, tn), lambda j, i: (0, j)),
            pl.BlockSpec((K, tn), lambda j, i: (0, j)),
        ],
        out_specs=pl.BlockSpec((tm, tn), lambda j, i: (i, j)),
        out_shape=jax.ShapeDtypeStruct((M, N), BF16),
        scratch_shapes=[pltpu.VMEM((K, tn), BF16), pltpu.VMEM((K, tn), BF16)],
        compiler_params=_params("parallel", "arbitrary"),
        name="ffn_up",
    )(u, wg, wu)


def _cast_kernel(x_ref, o_ref):
    o_ref[...] = x_ref[...].astype(o_ref.dtype)


def _to_bf16(w, rows=512):
    R, C = w.shape
    rows = _tile(R, rows)
    return pl.pallas_call(
        _cast_kernel,
        grid=(R // rows,),
        in_specs=[pl.BlockSpec((rows, C), lambda i: (i, 0))],
        out_specs=pl.BlockSpec((rows, C), lambda i: (i, 0)),
        out_shape=jax.ShapeDtypeStruct((R, C), BF16),
        compiler_params=_params("parallel"),
        name="cast_bf16",
    )(w)


def _down_post_kernel(h_ref, w_ref, x_ref, g_ref, gate_ref, *rest, res_w, with_next):
    y = _rms(_dot(h_ref[...], w_ref[...])) * g_ref[...]
    x_new = x_ref[...] + res_w * gate_ref[0] * y
    if with_next:
        gn_ref, sc_ref, sh_ref, o_ref, u_ref = rest
        u_ref[...] = (_rms(x_new) * gn_ref[...] * (1.0 + sc_ref[0]) + sh_ref[0]).astype(u_ref.dtype)
    else:
        (o_ref,) = rest
    o_ref[...] = x_new


def _down_post(h, wd, x2, g, gate, res_w, S, next_norm=None, tm=256):
    M, K = h.shape
    D = wd.shape[1]
    B = M // S
    tm = _tile(S, tm)
    per_b = S // tm
    row_spec = pl.BlockSpec((tm, D), lambda i: (i, 0))
    vec_spec = pl.BlockSpec((1, D), lambda i: (0, 0))
    batch_spec = pl.BlockSpec((1, 1, D), lambda i: (i // per_b, 0, 0))
    in_specs = [
        pl.BlockSpec((tm, K), lambda i: (i, 0)),
        pl.BlockSpec((K, D), lambda i: (0, 0), pipeline_mode=pl.Buffered(1)),
        row_spec, vec_spec, batch_spec,
    ]
    args = [h, _to_bf16(wd), x2, g.reshape(1, D), gate.reshape(B, 1, D)]
    out_specs, out_shape = row_spec, jax.ShapeDtypeStruct((M, D), F32)
    if next_norm is not None:
        g_next, scale_next, shift_next = next_norm
        in_specs += [vec_spec, batch_spec, batch_spec]
        args += [g_next.reshape(1, D), scale_next.reshape(B, 1, D), shift_next.reshape(B, 1, D)]
        out_specs, out_shape = [row_spec, row_spec], [out_shape, jax.ShapeDtypeStruct((M, D), BF16)]
    return pl.pallas_call(
        functools.partial(_down_post_kernel, res_w=res_w, with_next=next_norm is not None),
        grid=(M // tm,),
        in_specs=in_specs,
        out_specs=out_specs,
        out_shape=out_shape,
        compiler_params=_params("parallel"),
        name="down_post",
    )(*args)


def _merge_kernel(u_ref, g0_ref, g1_ref, g2_ref, g3_ref, y0_ref, y1_ref, y2_ref, y3_ref, wb_ref, o_ref,
                  wg_bf, wb_bf):
    @pl.when(pl.program_id(1) == 0)
    def _():
        for i, g_ref in enumerate((g0_ref, g1_ref, g2_ref, g3_ref)):
            wg_bf[i] = g_ref[...].astype(BF16)
        wb_bf[...] = wb_ref[...].astype(BF16)

    u = u_ref[...]
    acc = None
    for i, y_ref in enumerate((y0_ref, y1_ref, y2_ref, y3_ref)):
        gate = jax.nn.sigmoid(_dot(u, wg_bf[i]))
        term = gate * _dot(y_ref[...], wb_bf[i])
        acc = term if acc is None else acc + term
    o_ref[...] = acc.astype(o_ref.dtype)


def _merge(u, w_gate, ys, w_branch, tm=512, tn=256):
    M, D = u.shape
    W = w_branch.shape[1]
    tm, tn = _tile(M, tm), _tile(D, tn)
    per_branch = D // tn
    y_spec = pl.BlockSpec((tm, W), lambda j, i: (i, 0))
    gate_specs = [pl.BlockSpec((D, tn), lambda j, i, b=b: (0, b * per_branch + j)) for b in range(4)]
    return pl.pallas_call(
        _merge_kernel,
        grid=(D // tn, M // tm),
        in_specs=[pl.BlockSpec((tm, D), lambda j, i: (i, 0))] + gate_specs + [y_spec] * 4 + [
            pl.BlockSpec((4, W, tn), lambda j, i: (0, 0, j))],
        out_specs=pl.BlockSpec((tm, tn), lambda j, i: (i, j)),
        out_shape=jax.ShapeDtypeStruct((M, D), BF16),
        scratch_shapes=[pltpu.VMEM((4, D, tn), BF16), pltpu.VMEM((4, W, tn), BF16)],
        compiler_params=_params("parallel", "arbitrary"),
        name="merge",
    )(u, w_gate, w_gate, w_gate, w_gate, *ys, w_branch)


def _flash_kernel(*refs, mode, t, G):
    if mode == "select":
        q_ref, k_ref, v_ref, sel_ref, o_ref, m_ref, l_ref, acc_ref = refs
    else:
        q_ref, k_ref, v_ref, o_ref, m_ref, l_ref, acc_ref = refs
    i = pl.program_id(2)
    cols = G * t
    q = q_ref[0, 0, 0]
    k_off = lax.broadcasted_iota(jnp.int32, (t, cols), 0)
    q_off = lax.broadcasted_iota(jnp.int32, (t, cols), 1) & (t - 1)
    blocks_per_tile = t // SLC_BLOCK

    def scores(j, mask):
        start = pl.multiple_of(j * t, t)
        s = _dot(k_ref[0, 0, pl.ds(start, t), :], q)
        if mode == "select":
            parts = []
            for kb in range(blocks_per_tile):
                picked = sel_ref[0, 0, pl.ds(j * blocks_per_tile + kb, 1), :]
                picked = jnp.concatenate([picked] * G, axis=1) if G > 1 else picked
                parts.append(jnp.where(picked > 0.5, s[kb * SLC_BLOCK:(kb + 1) * SLC_BLOCK], NEG_INF))
            s = jnp.concatenate(parts, axis=0)
        if mask is not None:
            s = jnp.where(mask, s, NEG_INF)
        return s, v_ref[0, 0, j]

    s, v = scores(i, k_off <= q_off)
    m = jnp.max(s, axis=0, keepdims=True)
    p = jnp.exp(s - m)
    m_ref[...] = m
    l_ref[...] = jnp.sum(p, axis=0, keepdims=True)
    acc_ref[...] = _dot(v, p.astype(BF16))

    def accumulate(tiles, mask=None):
        parts = [scores(j, mask) for j in tiles]
        s = jnp.concatenate([part[0] for part in parts], axis=0) if len(parts) > 1 else parts[0][0]
        v = jnp.concatenate([part[1] for part in parts], axis=1) if len(parts) > 1 else parts[0][1]
        m_prev = m_ref[...]
        m_new = jnp.maximum(m_prev, jnp.max(s, axis=0, keepdims=True))
        p = jnp.exp(s - m_new)
        alpha = jnp.exp(m_prev - m_new)
        l_ref[...] = alpha * l_ref[...] + jnp.sum(p, axis=0, keepdims=True)
        acc_ref[...] = alpha * acc_ref[...] + _dot(v, p.astype(BF16))
        m_ref[...] = m_new

    if mode == "window":
        n_back = WIN_SIZE // t
        for back in range(1, n_back + 1):
            mask = (k_off > q_off) if back == n_back else None

            @pl.when(i >= back)
            def _(back=back, mask=mask):
                accumulate([i - back], mask)
    else:
        def body(pair, carry):
            accumulate([2 * pair, 2 * pair + 1])
            return carry

        lax.fori_loop(0, jnp.right_shift(i, 1), body, 0)

        @pl.when((i & 1) == 1)
        def _():
            accumulate([i - 1])

    o_ref[0, 0, 0] = (acc_ref[...] * (1.0 / l_ref[...])).astype(o_ref.dtype)


def _flash(q, k, v, *, mode, sel=None, out_dtype=F32):
    B, Hk, nq, Dq, cols = q.shape
    S = k.shape[2]
    Dv, t = v.shape[3], v.shape[4]
    G = cols // t
    assert t & (t - 1) == 0 and t % SLC_BLOCK == 0 and nq * t == S
    if mode == "window":
        assert WIN_SIZE % t == 0
    in_specs = [
        pl.BlockSpec((1, 1, 1, Dq, cols), lambda b, h, i: (b, h, i, 0, 0)),
        pl.BlockSpec((1, 1, S, Dq), lambda b, h, i: (b, h, 0, 0)),
        pl.BlockSpec((1, 1, nq, Dv, t), lambda b, h, i: (b, h, 0, 0, 0)),
    ]
    args = [q, k, v]
    if mode == "select":
        in_specs.append(pl.BlockSpec((1, 1, sel.shape[2], t), lambda b, h, i: (b, h, 0, i)))
        args.append(sel)
    return pl.pallas_call(
        functools.partial(_flash_kernel, mode=mode, t=t, G=G),
        grid=(B, Hk, nq),
        in_specs=in_specs,
        out_specs=pl.BlockSpec((1, 1, 1, Dv, cols), lambda b, h, i: (b, h, i, 0, 0)),
        out_shape=jax.ShapeDtypeStruct((B, Hk, nq, Dv, cols), out_dtype),
        scratch_shapes=[pltpu.VMEM((1, cols), F32), pltpu.VMEM((1, cols), F32), pltpu.VMEM((Dv, cols), F32)],
        compiler_params=_params("parallel", "parallel", "arbitrary"),
        name="attn_" + mode,
    )(*args)


def _nsa_compress_kernel(t_ref, pos_ref, w1_ref, w2_ref, o_ref):
    n_pieces, half = t_ref.shape[-2], t_ref.shape[-1]
    for which in range(2):
        t = t_ref[which, 0, 0]
        w1 = w1_ref[which]
        first = _dot(t, w1[:half], HI)
        second = _dot(t, w1[half:], HI)
        pos_bias = _dot(pos_ref[which], w1, HI)
        pre = first + pltpu.roll(second, n_pieces - 1, 0) + pos_bias
        o_ref[which, 0, 0] = _dot(jax.nn.gelu(pre), w2_ref[which], HI)


def _nsa_compress(kv_flat, pos, w1, w2):
    _, B, Hk, n_pieces, half = kv_flat.shape
    hid, d = w2.shape[1], w2.shape[2]
    return pl.pallas_call(
        _nsa_compress_kernel,
        grid=(B, Hk),
        in_specs=[
            pl.BlockSpec((2, 1, 1, n_pieces, half), lambda b, h: (0, b, h, 0, 0)),
            pl.BlockSpec((2, 1, 2 * half), lambda b, h: (0, 0, 0)),
            pl.BlockSpec((2, 2 * half, hid), lambda b, h: (0, 0, 0)),
            pl.BlockSpec((2, hid, d), lambda b, h: (0, 0, 0)),
        ],
        out_specs=pl.BlockSpec((2, 1, 1, n_pieces, d), lambda b, h: (0, b, h, 0, 0)),
        out_shape=jax.ShapeDtypeStruct((2, B, Hk, n_pieces, d), F32),
        compiler_params=_params("parallel", "parallel"),
        name="nsa_compress",
    )(kv_flat, pos, w1, w2)


def _nsa_cmp_kernel(q_ref, kc_ref, vct_ref, ovt_ref, o_ref, sel_ref, *, t, G):
    i = pl.program_id(2)
    cols = G * t
    n_cmp = kc_ref.shape[-2]
    n_sel = ovt_ref.shape[0]
    kc = kc_ref[0, 0]
    kc_hi = kc.astype(BF16)
    kc_lo = (kc - kc_hi.astype(F32)).astype(BF16)
    q = q_ref[0, 0, 0]
    s = _dot(kc_hi, q) + _dot(kc_lo, q)
    t_pos = i * t + (lax.broadcasted_iota(jnp.int32, (n_cmp, cols), 1) & (t - 1))
    blk_end = lax.broadcasted_iota(jnp.int32, (n_cmp, cols), 0) * CMP_STRIDE + (CMP_BLOCK - 1)
    mask = blk_end <= t_pos
    s = jnp.where(mask, s, NEG_INF)
    e = jnp.exp(s - jnp.max(s, axis=0, keepdims=True))
    p = jnp.where(mask, e * (1.0 / jnp.sum(e, axis=0, keepdims=True)), 0.0)
    o_ref[0, 0, 0] = _dot(vct_ref[0, 0].astype(BF16), p.astype(BF16))

    p_group = p[:, 0:t]
    for g in range(1, G):
        p_group = p_group + p[:, g * t:(g + 1) * t]
    pg_hi = p_group.astype(BF16)
    pg_lo = (p_group - pg_hi.astype(F32)).astype(BF16)
    ovt = ovt_ref[...].astype(BF16)
    imp = _dot(ovt, pg_hi) + _dot(ovt, pg_lo)
    blk = lax.broadcasted_iota(jnp.int32, (n_sel, t), 0)
    cur = jnp.right_shift(i * t + lax.broadcasted_iota(jnp.int32, (n_sel, t), 1), SLC_SHIFT)
    imp = jnp.where(blk <= cur, jnp.where(blk == 0, FORCE_SCORE, jnp.where(blk >= cur - 1, FORCE_SCORE, imp)),
                    NEG_INF)
    rank = jnp.zeros((n_sel, t), F32)
    for c in range(n_sel):
        other = imp[c:c + 1, :]
        later = jnp.where(blk > c, 1.0, 0.0)
        rank = rank + jnp.where(other > imp, 1.0, 0.0) + jnp.where(other == imp, later, 0.0)
    sel_ref[0, 0] = jnp.where(rank < float(min(SLC_TOPK, n_sel)), 1.0, 0.0)


def _nsa_cmp(q, kc, vct, overlap_t):
    B, Hk, nq, d, cols = q.shape
    n_cmp = kc.shape[2]
    n_sel = overlap_t.shape[0]
    t = n_sel * SLC_BLOCK // nq
    G = cols // t
    o_spec = pl.BlockSpec((1, 1, 1, d, cols), lambda b, h, i: (b, h, i, 0, 0))
    return pl.pallas_call(
        functools.partial(_nsa_cmp_kernel, t=t, G=G),
        grid=(B, Hk, nq),
        in_specs=[
            o_spec,
            pl.BlockSpec((1, 1, n_cmp, d), lambda b, h, i: (b, h, 0, 0)),
            pl.BlockSpec((1, 1, d, n_cmp), lambda b, h, i: (b, h, 0, 0)),
            pl.BlockSpec((n_sel, n_cmp), lambda b, h, i: (0, 0)),
        ],
        out_specs=[o_spec, pl.BlockSpec((1, 1, n_sel, t), lambda b, h, i: (b, h, 0, i))],
        out_shape=[
            jax.ShapeDtypeStruct((B, Hk, nq, d, cols), F32),
            jax.ShapeDtypeStruct((B, Hk, n_sel, nq * t), F32),
        ],
        compiler_params=_params("parallel", "parallel", "parallel"),
        name="nsa_cmp",
    )(q, kc, vct, overlap_t)


def _nsa_combine_kernel(gl_ref, oc_ref, os_ref, ow_ref, o_ref):
    g = jax.nn.sigmoid(gl_ref[0, 0, 0])
    o_ref[0, 0, 0] = (g[0:1] * oc_ref[0, 0, 0] + g[1:2] * os_ref[0, 0, 0]
                      + g[2:3] * ow_ref[0, 0, 0]).astype(o_ref.dtype)


def _nsa_combine(gl, o_cmp, o_slc, o_win):
    B, Hk, nq, d, cols = o_cmp.shape
    o_spec = pl.BlockSpec((1, 1, 1, d, cols), lambda b, h, i: (b, h, i, 0, 0))
    return pl.pallas_call(
        _nsa_combine_kernel,
        grid=(B, Hk, nq),
        in_specs=[pl.BlockSpec((1, 1, 1, 3, cols), lambda b, h, i: (b, h, i, 0, 0)), o_spec, o_spec, o_spec],
        out_specs=o_spec,
        out_shape=jax.ShapeDtypeStruct(o_cmp.shape, BF16),
        compiler_params=_params("parallel", "parallel", "parallel"),
        name="nsa_combine",
    )(gl, o_cmp, o_slc, o_win)


def _rwkv_prep_kernel(p_ref, mu_ref, w0_ref, ww2_ref, a0_ref, aw2_ref, gw2_ref, kk_ref, ka_ref, seg_ref,
                      r_o, lw_o, k_o, v_o, kk_o, b_o, g_o, last_ref):
    tb = p_ref.shape[1]
    width = r_o.shape[-1]

    @pl.when(pl.program_id(1) == 0)
    def _():
        last_ref[...] = jnp.zeros_like(last_ref)

    p = p_ref[0]
    row = lax.broadcasted_iota(jnp.int32, p.shape, 0)
    prev = jnp.where(row == 0, last_ref[...], pltpu.roll(p, 1, 0))
    last_ref[...] = p[tb - 1:tb]
    xs = p + (prev - p) * mu_ref[...]
    r = xs[:, 0:width]
    k = xs[:, width:2 * width]
    v = xs[:, 2 * width:3 * width]
    lora = xs[:, 3 * width:3 * width + 128]
    g_lo = xs[:, 3 * width + 128:]
    lw = -math.exp(-0.5) * jax.nn.sigmoid(w0_ref[...] + _dot(jnp.tanh(lora), ww2_ref[...], HI))
    a = jax.nn.sigmoid(a0_ref[...] + _dot(lora, aw2_ref[...], HI))
    g = _dot(jax.nn.sigmoid(g_lo), gw2_ref[...], HI)
    kk = k * kk_ref[...]
    norm = jnp.sqrt(_dot(kk * kk, seg_ref[...], HI))
    kk = kk / jnp.maximum(norm, 1e-12)
    r_o[0] = r
    lw_o[0] = lw
    k_o[0] = k * (1.0 + (a - 1.0) * ka_ref[...])
    v_o[0] = v
    kk_o[0] = kk
    b_o[0] = kk * a
    g_o[0] = g


def _rwkv_prep(p, consts, tb=256):
    B, S, P = p.shape
    width = consts[1].shape[-1]
    tb = _tile(S, tb)
    const_specs = [pl.BlockSpec(c.shape, lambda b, t: (0, 0)) for c in consts]
    o_spec = pl.BlockSpec((1, tb, width), lambda b, t: (b, t, 0))
    return pl.pallas_call(
        _rwkv_prep_kernel,
        grid=(B, S // tb),
        in_specs=[pl.BlockSpec((1, tb, P), lambda b, t: (b, t, 0))] + const_specs,
        out_specs=[o_spec] * 7,
        out_shape=[jax.ShapeDtypeStruct((B, S, width), F32)] * 7,
        scratch_shapes=[pltpu.VMEM((1, P), F32)],
        compiler_params=_params("parallel", "arbitrary"),
        name="rwkv_prep",
    )(p, *consts)


def _cumsum_rows(x):
    n = x.shape[0]
    row = lax.broadcasted_iota(jnp.int32, x.shape, 0)
    step = 1
    while step < n:
        x = x + jnp.where(row >= step, pltpu.roll(x, step, 0), 0.0)
        step *= 2
    return x


def _seg_sum(x, seg):
    rows = x.shape[0]
    hi = x.astype(BF16)
    lo = (x - hi.astype(F32)).astype(BF16)
    both = _dot(jnp.concatenate([hi, lo], axis=0), seg)
    return both[:rows] + both[rows:]


def _rwkv_group_chunk(rg, kkg, bd, kd, kc, bc, v, gamma, state, lane_masks, strict, incl, eye, chunk):
    stack = lambda x: jnp.concatenate([x * m for m in lane_masks], axis=0)
    kks, rs, vs = stack(kkg), stack(rg), stack(v)
    n, w = kks.shape
    kks_b, vs = kks.astype(BF16), vs.astype(BF16)
    big = _dot_nt(jnp.concatenate([kks_b, rs.astype(BF16)], axis=0),
                  jnp.concatenate([stack(bd), stack(kd)], axis=0).astype(BF16))
    a_mat = jnp.where(strict, big[:n, :n], 0.0)
    bk = jnp.where(strict, big[:n, n:], 0.0)
    ay = jnp.where(incl, big[n:, :n], 0.0)
    by = jnp.where(incl, big[n:, n:], 0.0)
    power = -a_mat
    inv = eye + power
    power = _dot(power, power)
    for _ in range(int(math.log2(chunk)) - 2):
        both = _dot(jnp.concatenate([power, inv], axis=0), power)
        power, inv = both[:n], inv + both[n:]
    inv = inv + _dot(inv, power)
    bv = _dot(jnp.concatenate([bk, by], axis=0).astype(BF16), vs)
    tx = _dot(inv.astype(BF16), jnp.concatenate([kks_b, bv[:n].astype(BF16)], axis=1))
    ayx = _dot(ay, tx)
    q_hat = rs - ayx[:, :w]
    y_intra = bv[n:] - ayx[:, w:]
    qt = _dot_nt(jnp.concatenate([q_hat, tx[:, :w]], axis=0), state)
    ys = qt[:n] + y_intra
    us = qt[n:] + tx[:, w:]
    new_state = state * gamma + _dot_tn(jnp.concatenate([vs, us], axis=0),
                                        jnp.concatenate([stack(kc), -stack(bc)], axis=0))
    y = ys[:chunk]
    for h in range(1, len(lane_masks)):
        y = y + ys[h * chunk:(h + 1) * chunk]
    return y, new_state


def _rwkv_scan_kernel(r_ref, lw_ref, k_ref, v_ref, kk_ref, b_ref, g_ref, lnw_ref, lnb_ref, rk_ref, seg_ref,
                      o_ref, state_ref, *, heads, chunk, group):
    tb, width = r_ref.shape[1], r_ref.shape[2]
    n = width // heads
    groups, gw, gn = heads // group, group * n, group * chunk

    @pl.when(pl.program_id(1) == 0)
    def _():
        state_ref[...] = jnp.zeros_like(state_ref)

    lane = lax.broadcasted_iota(jnp.int32, (1, gw), 1)
    lane_masks = [jnp.where(jnp.logical_and(lane >= h * n, lane < (h + 1) * n), 1.0, 0.0) for h in range(group)]
    row_id = lax.broadcasted_iota(jnp.int32, (gn, gn), 0)
    col_id = lax.broadcasted_iota(jnp.int32, (gn, gn), 1)
    row, col = row_id & (chunk - 1), col_id & (chunk - 1)
    strict, incl = col < row, col <= row
    eye = jnp.where(row_id == col_id, 1.0, 0.0)
    states = [state_ref[p] for p in range(groups)]
    y_chunks = []
    for c in range(tb // chunk):
        rows = pl.ds(c * chunk, chunk)
        r, lw, k, v = r_ref[0, rows, :], lw_ref[0, rows, :], k_ref[0, rows, :], v_ref[0, rows, :]
        kk, bb = kk_ref[0, rows, :], b_ref[0, rows, :]
        g = _cumsum_rows(lw)
        g_last = g[chunk - 1:chunk]
        decay_out = jnp.exp(-g)
        decay_end = jnp.exp(g_last - g)
        rg, kkg = r * jnp.exp(g), kk * jnp.exp(g - lw)
        bd, kd = bb * decay_out, k * decay_out
        kc, bc = k * decay_end, bb * decay_end
        gamma = jnp.exp(g_last)
        ys = []
        for p in range(groups):
            sl = slice(p * gw, (p + 1) * gw)
            y, states[p] = _rwkv_group_chunk(rg[:, sl], kkg[:, sl], bd[:, sl], kd[:, sl], kc[:, sl], bc[:, sl],
                                             v[:, sl], gamma[:, sl], states[p], lane_masks, strict, incl, eye, chunk)
            ys.append(y)
        y_chunks.append(jnp.concatenate(ys, axis=1))
    for p in range(groups):
        state_ref[p] = states[p]
    y = jnp.concatenate(y_chunks, axis=0)
    seg = seg_ref[...]
    r, k, v = r_ref[0], k_ref[0], v_ref[0]
    sums = _seg_sum(jnp.concatenate([y, r * k * rk_ref[...]], axis=0), seg)
    dev = y - sums[:tb] * (1.0 / n)
    var = _seg_sum(dev * dev, seg) * (1.0 / n)
    yn = dev * lax.rsqrt(var + RWKV_GN_EPS) * lnw_ref[...] + lnb_ref[...]
    o_ref[0] = ((yn + sums[tb:] * v) * g_ref[0]).astype(o_ref.dtype)


def _rwkv_scan(r, lw, k, v, kk, bb, g, ln_w, ln_b, r_k, seg, tb=256, group=4):
    B, S, width = r.shape
    heads = RWKV_HEADS
    n = width // heads
    chunk = min(RWKV_CHUNK, S)
    assert chunk == n and heads % group == 0
    tb = _tile(S, tb)
    x_spec = pl.BlockSpec((1, tb, width), lambda b, t: (b, t, 0))
    vec_spec = pl.BlockSpec((1, width), lambda b, t: (0, 0))
    return pl.pallas_call(
        functools.partial(_rwkv_scan_kernel, heads=heads, chunk=chunk, group=group),
        grid=(B, S // tb),
        in_specs=[x_spec] * 7 + [vec_spec] * 3 + [pl.BlockSpec((width, width), lambda b, t: (0, 0))],
        out_specs=x_spec,
        out_shape=jax.ShapeDtypeStruct((B, S, width), BF16),
        scratch_shapes=[pltpu.VMEM((heads // group, group * n, group * n), F32)],
        compiler_params=_params("parallel", "arbitrary"),
        name="rwkv_scan",
    )(r, lw, k, v, kk, bb, g, ln_w, ln_b, r_k, seg.astype(BF16))


def _gla_kernel(q_ref, k_ref, v_ref, r_ref, al_ref, aw_ref, ab_ref, ng_ref, o_ref, state_ref,
                *, heads, chunk):
    tb = v_ref.shape[1]
    dk = q_ref.shape[-1]
    dv = v_ref.shape[-1] // heads

    @pl.when(pl.program_id(1) == 0)
    def _():
        state_ref[...] = jnp.zeros_like(state_ref)

    row = lax.broadcasted_iota(jnp.int32, (chunk, chunk), 0)
    col = lax.broadcasted_iota(jnp.int32, (chunk, chunk), 1)
    incl = col <= row
    states = [state_ref[h] for h in range(heads)]
    for c in range(tb // chunk):
        rows = pl.ds(c * chunk, chunk)
        x = _dot(al_ref[0, rows, :], aw_ref[...], HI) + ab_ref[...]
        log_a = (jnp.minimum(x, 0.0) - jnp.log(1.0 + jnp.exp(-jnp.abs(x)))) * (1.0 / GLA_TAU)
        b_all = _cumsum_rows(log_a)
        outs = []
        for h in range(heads):
            cols = pl.ds(h * dv, dv)
            b = b_all[:, h * dk:(h + 1) * dk]
            b_last = b[chunk - 1:chunk]
            q = q_ref[0, h, rows, :]
            k = k_ref[0, h, rows, :]
            v = v_ref[0, rows, cols]
            qs = q * (dk ** -0.5)
            qe = qs * jnp.exp(b)
            b_mid = b[chunk // 2:chunk // 2 + 1]
            attn = jnp.where(incl, _dot_nt(qs * jnp.exp(b - b_mid), k * jnp.exp(b_mid - b)), 0.0)
            o = _dot(attn, v) + _dot_nt(qe, states[h])
            states[h] = states[h] * jnp.exp(b_last) + _dot_tn(v, k * jnp.exp(b_last - b))
            outs.append(o * lax.rsqrt(jnp.mean(o * o, axis=-1, keepdims=True) + NORM_EPS))
        gate = r_ref[0, rows, :]
        o_ref[0, rows, :] = (jnp.concatenate(outs, axis=1) * ng_ref[...]
                             * (gate * jax.nn.sigmoid(gate))).astype(o_ref.dtype)
    for h in range(heads):
        state_ref[h] = states[h]


def _gla(q, k, v, r, alpha_lo, aw, ab, ng, tb=256):
    B, H, S, dk = q.shape
    width = v.shape[-1]
    lora = alpha_lo.shape[-1]
    chunk = min(GLA_CHUNK, S)
    tb = _tile(S, tb)
    qk_spec = pl.BlockSpec((1, H, tb, dk), lambda b, t: (b, 0, t, 0))
    x_spec = pl.BlockSpec((1, tb, width), lambda b, t: (b, t, 0))
    return pl.pallas_call(
        functools.partial(_gla_kernel, heads=H, chunk=chunk),
        grid=(B, S // tb),
        in_specs=[
            qk_spec, qk_spec, x_spec, x_spec,
            pl.BlockSpec((1, tb, lora), lambda b, t: (b, t, 0)),
            pl.BlockSpec(aw.shape, lambda b, t: (0, 0)),
            pl.BlockSpec(ab.shape, lambda b, t: (0, 0)),
            pl.BlockSpec(ng.shape, lambda b, t: (0, 0)),
        ],
        out_specs=x_spec,
        out_shape=jax.ShapeDtypeStruct((B, S, width), BF16),
        scratch_shapes=[pltpu.VMEM((H, width // H, dk), F32)],
        compiler_params=_params("parallel", "arbitrary"),
        name="gla",
    )(q, k, v, r, alpha_lo, aw, ab, ng)


def _rope_tables(S, d):
    inv = ROPE_THETA ** (-jnp.arange(0, d, 2, dtype=F32) / d)
    ang = jnp.arange(S).astype(F32)[:, None] * inv[None, :]
    cos, sin = jnp.cos(ang), jnp.sin(ang)
    return jnp.concatenate([cos, cos], axis=-1), jnp.concatenate([sin, sin], axis=-1)


def _rot_cols(w, d):
    K = w.shape[0]
    w = w.reshape(K, -1, 2, d // 2)
    return jnp.concatenate([-w[:, :, 1], w[:, :, 0]], axis=-1).reshape(K, -1)


def _nsa_constants(S):
    n_pieces = S // CMP_STRIDE
    n_cmp = (S - CMP_BLOCK) // CMP_STRIDE + 1
    n_sel = S // SLC_BLOCK
    start = np.arange(n_cmp)[:, None] * CMP_STRIDE
    end = start + CMP_BLOCK - 1
    blk = np.arange(n_sel)[None, :]
    overlap = np.zeros((n_pieces, n_sel), np.float32)
    overlap[:n_cmp] = ((start < (blk + 1) * SLC_BLOCK) & (end >= blk * SLC_BLOCK)).astype(np.float32)
    return jnp.asarray(overlap.T)


def _token_mixing(u, B, S, w_in, w_branch, cmp_pos, cmp_w1, cmp_w2, rwkv_mu, rwkv_w0, rwkv_w_w2, rwkv_a0,
                  rwkv_a_w2, rwkv_g_w2, rwkv_k_k, rwkv_k_a, rwkv_r_k, rwkv_ln_w, rwkv_ln_b, gla_alpha_w2,
                  gla_alpha_b, gla_norm_g, mla_q_norm, mla_w_uq, mla_kv_norm, mla_w_ukv):
    M, D = u.shape
    d = HEAD_DIM
    width = D // 4
    sizes = [width + 6 * NSA_KV_HEADS * d + 3 * NSA_HEADS, 3 * width + 256, 2 * (width // 2) + 2 * width + 16,
             384 + 128 + MLA_ROPE, 4 * D]
    offs = np.cumsum([0] + sizes)
    assert offs[-1] == w_in.shape[1]
    w_nsa, w_rwkv, w_gla, w_mla, w_gate = [w_in[:, offs[i]:offs[i + 1]] for i in range(5)]
    zeros = lambda n: jnp.zeros((D, n), F32)
    kvw = NSA_KV_HEADS * d

    wq = w_nsa[:, :width]
    wkc, wvc, wks, wvs, wkw, wvw = [w_nsa[:, width + i * kvw:width + (i + 1) * kvw] for i in range(6)]
    wgl = w_nsa[:, width + 6 * kvw:]
    wkr = w_mla[:, 512:512 + MLA_ROPE]
    n_rope = width + 3 * kvw + MLA_ROPE
    pad = (-n_rope) % 512
    w_a = jnp.concatenate([wq, wkc, wks, wkw, wkr, zeros(pad)], axis=1)
    w_b = jnp.concatenate([_rot_cols(wq, d), _rot_cols(wkc, d), _rot_cols(wks, d), _rot_cols(wkw, d),
                           _rot_cols(wkr, MLA_ROPE), zeros(pad)], axis=1)
    cos_d, sin_d = _rope_tables(S, d)
    cos_r, sin_r = _rope_tables(S, MLA_ROPE)
    reps = (width + 3 * kvw) // d
    cos_t = jnp.concatenate([jnp.tile(cos_d, (1, reps)), cos_r, jnp.ones((S, pad), F32)], axis=1)
    sin_t = jnp.concatenate([jnp.tile(sin_d, (1, reps)), sin_r, jnp.zeros((S, pad), F32)], axis=1)
    roped = _matmul(u, w_a, S=S, w_rot=w_b, cos=cos_t, sin=sin_t, name="proj_rope").reshape(B, S, -1)
    n_plain = 3 * kvw + 3 * NSA_HEADS
    plain = _matmul(u, jnp.concatenate([wvc, wvs, wvw, wgl, zeros((-n_plain) % 128)], axis=1), S=S,
                    name="proj_nsa_v")

    Hk, G = NSA_KV_HEADS, NSA_GROUP
    t_att = _tile(S, 256)
    nq = S // t_att
    cols_layout = lambda x, f: x.reshape(B, nq, t_att, Hk, G, f).transpose(0, 3, 1, 5, 4, 2).reshape(
        B, Hk, nq, f, G * t_att)
    kv_heads = lambda x: x.reshape(B, S, Hk, d).transpose(0, 2, 1, 3)
    v_tiles = lambda x: x.astype(BF16).reshape(B, nq, t_att, Hk, d).transpose(0, 3, 1, 4, 2)
    q_att = cols_layout((roped[..., :width] * d ** -0.5).astype(BF16), d)
    kc_in = kv_heads(roped[..., width:width + kvw])
    ks = kv_heads(roped[..., width + kvw:width + 2 * kvw].astype(BF16))
    kw = kv_heads(roped[..., width + 2 * kvw:width + 3 * kvw].astype(BF16))
    k_pe = roped[..., width + 3 * kvw:width + 3 * kvw + MLA_ROPE]
    plain3 = plain.reshape(B, S, -1)
    vc_in = kv_heads(plain3[..., :kvw])
    vs, vw = v_tiles(plain3[..., kvw:2 * kvw]), v_tiles(plain3[..., 2 * kvw:3 * kvw])
    gate_logits = cols_layout(plain3[..., 3 * kvw:3 * kvw + 3 * NSA_HEADS], 3)

    overlap_t = _nsa_constants(S)
    n_pieces = S // CMP_STRIDE
    kv_flat = jnp.stack([kc_in, vc_in]).reshape(2, B, Hk, n_pieces, CMP_STRIDE * d)
    cmp = _nsa_compress(kv_flat, cmp_pos.reshape(2, 1, CMP_BLOCK * d), cmp_w1, cmp_w2)
    o_cmp, sel = _nsa_cmp(q_att, cmp[0], cmp[1].transpose(0, 1, 3, 2), overlap_t)
    o_slc = _flash(q_att, ks, vs, mode="select", sel=sel)
    o_win = _flash(q_att, kw, vw, mode="window")
    y_nsa = _nsa_combine(gate_logits, o_cmp, o_slc, o_win)
    y_nsa = y_nsa.reshape(B, Hk, nq, d, G, t_att).transpose(0, 2, 5, 1, 4, 3).reshape(M, width)

    regroup = lambda t: jnp.concatenate(
        [t[..., 0:width], t[..., width + 64:2 * width + 64], t[..., 2 * width + 64:3 * width + 64],
         t[..., width:width + 64], t[..., 3 * width + 64:3 * width + 128], t[..., 3 * width + 128:]], axis=-1)
    p_rwkv = _matmul(u, regroup(w_rwkv), S=S, tn=896, name="proj_rwkv").reshape(B, S, -1)
    row = lambda t: t.reshape(1, -1)
    seg = jnp.asarray(np.kron(np.eye(RWKV_HEADS), np.ones((d, d))).astype(np.float32))
    lora_pad = jnp.zeros((64, width), F32)
    consts = [row(regroup(rwkv_mu)), row(rwkv_w0), jnp.concatenate([rwkv_w_w2, lora_pad], axis=0), row(rwkv_a0),
              jnp.concatenate([lora_pad, rwkv_a_w2], axis=0), rwkv_g_w2, row(rwkv_k_k), row(rwkv_k_a), seg]
    r, lw, k2, v, kk, bb, g = _rwkv_prep(p_rwkv, consts)
    y_rwkv = _rwkv_scan(r, lw, k2, v, kk, bb, g, row(rwkv_ln_w), row(rwkv_ln_b), row(rwkv_r_k), seg)
    y_rwkv = y_rwkv.reshape(M, width)

    dk_all = width // 2
    w_gla_p = jnp.concatenate([w_gla[:, :2 * dk_all + width], w_gla[:, 2 * dk_all + width + 16:],
                               w_gla[:, 2 * dk_all + width:2 * dk_all + width + 16], zeros(240)], axis=1)
    p_gla = _matmul(u, w_gla_p, S=S, tn=896, name="proj_gla").reshape(B, S, -1)
    dkh = dk_all // GLA_HEADS
    gla_heads = lambda t: t.reshape(B, S, GLA_HEADS, dkh).transpose(0, 2, 1, 3)
    y_gla = _gla(gla_heads(p_gla[..., :dk_all]), gla_heads(p_gla[..., dk_all:2 * dk_all]),
                 p_gla[..., 2 * dk_all:2 * dk_all + width], p_gla[..., 2 * dk_all + width:2 * dk_all + 2 * width],
                 p_gla[..., 2 * dk_all + 2 * width:2 * dk_all + 2 * width + 16],
                 gla_alpha_w2, row(gla_alpha_b), row(gla_norm_g)).reshape(M, width)

    c_q = _matmul(u, w_mla[:, :384], S=S, name="proj_mla_q")
    c_kv = _matmul(u, w_mla[:, 384:512], S=S, name="proj_mla_kv")
    H = MLA_HEADS
    wq3 = mla_w_uq.reshape(384, H, MLA_NOPE + MLA_ROPE)
    rp = wq3[:, :, MLA_NOPE:]
    half = MLA_ROPE // 2
    z = lambda n: jnp.zeros((384, H, n), F32)
    wq_a = jnp.concatenate([wq3, z(128 - MLA_NOPE - MLA_ROPE)], axis=-1).reshape(384, H * 128)
    wq_b = jnp.concatenate([z(MLA_NOPE), -rp[..., half:], rp[..., :half], z(128 - MLA_NOPE - MLA_ROPE)],
                           axis=-1).reshape(384, H * 128)
    tail = 128 - MLA_NOPE - MLA_ROPE
    cos_h = jnp.concatenate([jnp.ones((S, MLA_NOPE), F32), cos_r, jnp.ones((S, tail), F32)], axis=1)
    sin_h = jnp.concatenate([jnp.zeros((S, MLA_NOPE), F32), sin_r, jnp.zeros((S, tail), F32)], axis=1)
    q_m = _matmul(c_q, wq_a, S=S, g=mla_q_norm, w_rot=wq_b, cos=jnp.tile(cos_h, (1, H)), sin=jnp.tile(sin_h, (1, H)),
                  name="mla_q")
    t_mla = _tile(S, 512)
    nq_m = S // t_mla
    tiles_m = lambda x, f: x.reshape(B, nq_m, t_mla, H, f).transpose(0, 3, 1, 4, 2)
    q_m = tiles_m((q_m * (MLA_NOPE + MLA_ROPE) ** -0.5).astype(BF16), 128)
    kv_m = _matmul(c_kv, mla_w_ukv, S=S, g=mla_kv_norm, name="mla_kv").reshape(B, S, H, MLA_NOPE + width // H)
    k_m = jnp.concatenate([kv_m[..., :MLA_NOPE], jnp.broadcast_to(k_pe[:, :, None, :], (B, S, H, MLA_ROPE)),
                           jnp.zeros((B, S, H, tail), F32)], axis=-1).astype(BF16).transpose(0, 2, 1, 3)
    v_m = tiles_m(kv_m[..., MLA_NOPE:].astype(BF16), width // H)
    o_mla = _flash(q_m, k_m, v_m, mode="causal", out_dtype=BF16)
    y_mla = o_mla.transpose(0, 2, 4, 1, 3).reshape(M, width)

    return _merge(u, w_gate, (y_nsa, y_rwkv, y_gla, y_mla), w_branch)


def kernel(x, c, ada_w, ada_b, pre_g, post_g, ffn_wg, ffn_wu, ffn_wd, mix_w_in, mix_w_branch, mix_w_out, nsa_cmp_pos, nsa_cmp_w1, nsa_cmp_w2, rwkv_mu, rwkv_w0, rwkv_w_w2, rwkv_a0, rwkv_a_w2, rwkv_g_w2, rwkv_k_k, rwkv_k_a, rwkv_r_k, rwkv_ln_w, rwkv_ln_b, gla_alpha_w2, gla_alpha_b, gla_norm_g, mla_q_norm, mla_w_uq, mla_kv_norm, mla_w_ukv):
    B, S, D = x.shape
    M = B * S
    depth = ada_w.shape[0]
    mod = _modulation(c, ada_w, ada_b)
    shifts, scales, gates = mod[:, :, :D], mod[:, :, D:2 * D], mod[:, :, 2 * D:]
    x2 = x.reshape(M, D)
    n_sub = 3 * depth
    u = _prenorm(x2, pre_g[0, 0], scales[0], shifts[0], S)
    for s in range(n_sub):
        l, i = divmod(s, 3)
        nxt = s + 1
        next_norm = (pre_g[nxt // 3, nxt % 3], scales[nxt], shifts[nxt]) if nxt < n_sub else None
        if i == 1:
            y = _token_mixing(
                u, B, S, mix_w_in[l], mix_w_branch[l], nsa_cmp_pos[l], nsa_cmp_w1[l], nsa_cmp_w2[l], rwkv_mu[l],
                rwkv_w0[l], rwkv_w_w2[l], rwkv_a0[l], rwkv_a_w2[l], rwkv_g_w2[l], rwkv_k_k[l], rwkv_k_a[l],
                rwkv_r_k[l], rwkv_ln_w[l], rwkv_ln_b[l], gla_alpha_w2[l], gla_alpha_b[l], gla_norm_g[l],
                mla_q_norm[l], mla_w_uq[l], mla_kv_norm[l], mla_w_ukv[l])
            out = _down_post(y, mix_w_out[l], x2, post_g[l, i], gates[s], 1.0, S, next_norm)
        else:
            j = 0 if i == 0 else 1
            h = _ffn_up(u, ffn_wg[l, j], ffn_wu[l, j])
            out = _down_post(h, ffn_wd[l, j], x2, post_g[l, i], gates[s], 0.5, S, next_norm)
        x2, u = out if next_norm is not None else (out, None)
    return x2.reshape(B, S, D)
```
